```python
import math
import jax, jax.numpy as jnp
from jax import lax
import numpy as np

D_MODEL = 1024
BATCH = 16
SEQ = 4096
DEPTH = 2
DEC_BATCH = 32
DEC_SEQ = 64
PAST_LEN = 4096

CHUNK = 64
N_AB = (DEPTH + 1) // 2
N_C = DEPTH // 2
MLA_HEADS = 8
Q_LORA = 384
KV_LORA = 256
NOPE_DIM = 64
ROPE_DIM = 32
V_DIM = 64
MLA_WIDTH = MLA_HEADS * V_DIM
ROPE_BASE = 10000.0
Q_BLOCK = 128
ATTN_SCALE = (NOPE_DIM + ROPE_DIM) ** -0.5
S5_WIDTH = 512
S5_GROUP = 16
S5_GROUPS = S5_WIDTH // S5_GROUP
S5_STATE = 64
DT_MIN = 1e-3
DT_MAX = 1e-1
CONV_WIDTH = 31
CONV_CH = D_MODEL
EPS = 1e-6
IN_AB = Q_LORA + KV_LORA + ROPE_DIM + MLA_WIDTH + 2 * S5_WIDTH
AB_SPLITS = (Q_LORA, Q_LORA + KV_LORA, Q_LORA + KV_LORA + ROPE_DIM,
             Q_LORA + KV_LORA + ROPE_DIM + MLA_WIDTH,
             Q_LORA + KV_LORA + ROPE_DIM + MLA_WIDTH + S5_WIDTH)

kernel_name = 'hybrid_stream_mla_s5_conformer_step'


def rms_norm(x, g):
    xf = x.astype(jnp.float32)
    y = xf * lax.rsqrt(jnp.mean(xf * xf, axis=-1, keepdims=True) + EPS)
    return (y * g.astype(jnp.float32)).astype(x.dtype)


def layer_norm(x, g, b):
    xf = x.astype(jnp.float32)
    mu = jnp.mean(xf, axis=-1, keepdims=True)
    xc = xf - mu
    var = jnp.mean(xc * xc, axis=-1, keepdims=True)
    return (xc * lax.rsqrt(var + EPS) * g.astype(jnp.float32) + b.astype(jnp.float32)).astype(x.dtype)


def rope(x, pos):
    half = ROPE_DIM // 2
    inv = ROPE_BASE ** (-jnp.arange(half, dtype=jnp.float32) / half)
    ang = pos.astype(jnp.float32)[:, None] * inv[None, :]
    cos = jnp.cos(ang)[:, None, :]
    sin = jnp.sin(ang)[:, None, :]
    xf = x.astype(jnp.float32)
    x1, x2 = xf[..., :half], xf[..., half:]
    return jnp.concatenate([x1 * cos - x2 * sin, x2 * cos + x1 * sin], axis=-1).astype(x.dtype)


def attend(q_n, q_r, k_n, k_r, v, mask):
    s = (jnp.einsum('bqhd,bkhd->bhqk', q_n, k_n)
         + jnp.einsum('bqhr,bkr->bhqk', q_r, k_r)).astype(jnp.float32) * ATTN_SCALE
    if mask is not None:
        s = jnp.where(mask, s, -1e30)
    p = jax.nn.softmax(s, axis=-1).astype(v.dtype)
    return jnp.einsum('bhqk,bkhd->bqhd', p, v)


def attend_chunk_causal(q_n, q_r, k_n, k_r, v):
    B, T, H, _ = q_n.shape
    nb = T // Q_BLOCK
    key_chunk = jnp.arange(T) // CHUNK

    def one_block(args):
        qn_b, qr_b, start = args
        q_chunk = (start + jnp.arange(Q_BLOCK)) // CHUNK
        mask = key_chunk[None, :] <= q_chunk[:, None]
        return attend(qn_b, qr_b, k_n, k_r, v, mask)

    qn_blocks = jnp.moveaxis(q_n.reshape(B, nb, Q_BLOCK, H, NOPE_DIM), 1, 0)
    qr_blocks = jnp.moveaxis(q_r.reshape(B, nb, Q_BLOCK, H, ROPE_DIM), 1, 0)
    starts = jnp.arange(nb, dtype=jnp.int32) * Q_BLOCK
    out = lax.map(one_block, (qn_blocks, qr_blocks, starts))
    return jnp.moveaxis(out, 0, 1).reshape(B, T, H, V_DIM)


def _linear_combine(e1, e2):
    a1, b1 = e1
    a2, b2 = e2
    return a1 * a2, a2 * b1 + b2


def s5_scan(u, s0_re, s0_im, lam_re, lam_im, log_dt, b_re, b_im, c_re, c_im, d_skip):
    f32 = jnp.float32
    B, T, _ = u.shape
    uf = u.astype(f32)
    ug = uf.reshape(B, T, S5_GROUPS, S5_GROUP)
    lam = lax.complex(lam_re.astype(f32), lam_im.astype(f32))
    dt = jnp.exp(log_dt.astype(f32))[:, None]
    lam_bar = jnp.exp(lam * dt)
    b_bar = ((lam_bar - 1.0) / lam)[..., None] * lax.complex(b_re.astype(f32), b_im.astype(f32))
    bu = jnp.einsum('gpn,btgn->btgp', b_bar, ug.astype(jnp.complex64))
    if s0_re is not None:
        s0 = lax.complex(s0_re.astype(f32), s0_im.astype(f32))
        bu = bu.at[:, 0].add(lam_bar * s0)
    a = jnp.broadcast_to(lam_bar, bu.shape)
    _, states = lax.associative_scan(_linear_combine, (a, bu), axis=1)
    c = lax.complex(c_re.astype(f32), c_im.astype(f32))
    y = jnp.real(jnp.einsum('gnp,btgp->btgn', c, states)).reshape(B, T, S5_WIDTH) + d_skip.astype(f32) * uf
    last = states[:, -1]
    return y.astype(u.dtype), jnp.real(last), jnp.imag(last)


def ab_layer(x, pos, past_lat, past_kr, past_re, past_im, prm):
    (norm_g, w_in, g_q_lat, w_uq, g_kv_lat, w_uk, w_uv, g_q_nope, g_q_rope, g_k_nope, g_k_rope,
     lam_re, lam_im, log_dt, b_re, b_im, c_re, c_im, d_skip, w_glu, b_glu, w_out) = prm
    B, T, _ = x.shape
    h = rms_norm(x, norm_g)
    q_lat, c_kv, k_r, gate_mla, u, gate_s5 = jnp.split(h @ w_in, AB_SPLITS, axis=-1)
    c_kv = rms_norm(c_kv, g_kv_lat)
    k_r = rope(rms_norm(k_r, g_k_rope)[:, :, None, :], pos)[:, :, 0]
    q = (rms_norm(q_lat, g_q_lat) @ w_uq).reshape(B, T, MLA_HEADS, NOPE_DIM + ROPE_DIM)
    q_n = rms_norm(q[..., :NOPE_DIM], g_q_nope)
    q_r = rope(rms_norm(q[..., NOPE_DIM:], g_q_rope), pos)
    if past_lat is None:
        lat_all, kr_all = c_kv, k_r
    else:
        lat_all = jnp.concatenate([past_lat.astype(c_kv.dtype), c_kv], axis=1)
        kr_all = jnp.concatenate([past_kr.astype(k_r.dtype), k_r], axis=1)
    k_n = rms_norm(jnp.einsum('bkc,chd->bkhd', lat_all, w_uk), g_k_nope)
    v = jnp.einsum('bkc,chd->bkhd', lat_all, w_uv)
    if past_lat is None:
        o_mla = attend_chunk_causal(q_n, q_r, k_n, kr_all, v)
    else:
        o_mla = attend(q_n, q_r, k_n, kr_all, v, None)
    y_s5, fin_re, fin_im = s5_scan(u, past_re, past_im, lam_re, lam_im, log_dt, b_re, b_im, c_re, c_im, d_skip)
    z = jax.nn.gelu(y_s5)
    z = z * jax.nn.sigmoid(z @ w_glu + b_glu)
    mixed = jnp.concatenate([o_mla.reshape(B, T, MLA_WIDTH) * jax.nn.silu(gate_mla),
                             z * jax.nn.silu(gate_s5)], axis=-1)
    return x + mixed @ w_out, c_kv, k_r, fin_re, fin_im


def conv_layer(x, past, prm):
    norm_g, w_in, conv_w, conv_b, ln_g, ln_b, w_out = prm
    h = rms_norm(x, norm_g)
    a, b, gate = jnp.split(h @ w_in, 3, axis=-1)
    v = a * jax.nn.sigmoid(b)
    if past is None:
        pad = jnp.zeros((x.shape[0], CONV_WIDTH - 1, CONV_CH), v.dtype)
    else:
        pad = past.astype(v.dtype)
    vp = jnp.concatenate([pad, v], axis=1)
    y = lax.conv_general_dilated(vp, conv_w.astype(v.dtype)[:, None, :], (1,), 'VALID',
                                 dimension_numbers=('NWC', 'WIO', 'NWC'),
                                 feature_group_count=CONV_CH) + conv_b
    y = jax.nn.silu(layer_norm(y, ln_g, ln_b))
    return x + (y * jax.nn.silu(gate)) @ w_out, vp[:, -(CONV_WIDTH - 1):]


def setup_inputs(seed: int = 0) -> dict:
    key = jax.random.key(seed)
    k = jax.random.split(key, 40)
    f32 = jnp.float32

    def nrm(i, shape, scale):
        return jax.random.normal(k[i], shape, f32) * scale

    def gain(i, shape):
        return 1.0 + 0.05 * jax.random.normal(k[i], shape, f32)

    n_idx = jnp.arange(S5_STATE, dtype=f32)
    return {
        'x_prompt': nrm(0, (BATCH, SEQ, D_MODEL), 1.0),
        'x_sample': nrm(1, (DEC_BATCH, DEC_SEQ, D_MODEL), 1.0),
        'cache_mla_latent': nrm(2, (N_AB, DEC_BATCH, PAST_LEN, KV_LORA), 1.0),
        'cache_mla_krope': nrm(3, (N_AB, DEC_BATCH, PAST_LEN, ROPE_DIM), 1.0),
        'state_s5_re': nrm(4, (N_AB, DEC_BATCH, S5_GROUPS, S5_STATE), 0.3),
        'state_s5_im': nrm(5, (N_AB, DEC_BATCH, S5_GROUPS, S5_STATE), 0.3),
        'state_conv': nrm(6, (N_C, DEC_BATCH, CONV_WIDTH - 1, CONV_CH), 0.5),
        'norm_ab': gain(7, (N_AB, D_MODEL)),
        'w_in_ab': nrm(8, (N_AB, D_MODEL, IN_AB), D_MODEL ** -0.5),
        'g_q_lat': gain(9, (N_AB, Q_LORA)),
        'w_uq': nrm(10, (N_AB, Q_LORA, MLA_HEADS * (NOPE_DIM + ROPE_DIM)), Q_LORA ** -0.5),
        'g_kv_lat': gain(11, (N_AB, KV_LORA)),
        'w_uk': nrm(12, (N_AB, KV_LORA, MLA_HEADS, NOPE_DIM), KV_LORA ** -0.5),
        'w_uv': nrm(13, (N_AB, KV_LORA, MLA_HEADS, V_DIM), KV_LORA ** -0.5),
        'g_q_nope': gain(14, (N_AB, NOPE_DIM)),
        'g_q_rope': gain(15, (N_AB, ROPE_DIM)),
        'g_k_nope': gain(16, (N_AB, NOPE_DIM)),
        'g_k_rope': gain(17, (N_AB, ROPE_DIM)),
        's5_lam_re': -0.5 + nrm(18, (N_AB, S5_GROUPS, S5_STATE), 0.01),
        's5_lam_im': math.pi * n_idx + nrm(19, (N_AB, S5_GROUPS, S5_STATE), 0.01),
        's5_log_dt': jax.random.uniform(k[20], (N_AB, S5_GROUPS), f32, math.log(DT_MIN), math.log(DT_MAX)),
        's5_b_re': nrm(21, (N_AB, S5_GROUPS, S5_STATE, S5_GROUP), (2 * S5_GROUP) ** -0.5),
        's5_b_im': nrm(22, (N_AB, S5_GROUPS, S5_STATE, S5_GROUP), (2 * S5_GROUP) ** -0.5),
        's5_c_re': nrm(23, (N_AB, S5_GROUPS, S5_GROUP, S5_STATE), 0.5),
        's5_c_im': nrm(24, (N_AB, S5_GROUPS, S5_GROUP, S5_STATE), 0.5),
        's5_d': nrm(25, (N_AB, S5_WIDTH), 0.5),
        's5_w_glu': nrm(26, (N_AB, S5_WIDTH, S5_WIDTH), S5_WIDTH ** -0.5),
        's5_b_glu': nrm(27, (N_AB, S5_WIDTH), 0.02),
        'w_out_ab': nrm(28, (N_AB, MLA_WIDTH + S5_WIDTH, D_MODEL), 0.5 * (MLA_WIDTH + S5_WIDTH) ** -0.5),
        'norm_c': gain(29, (N_C, D_MODEL)),
        'w_in_c': nrm(30, (N_C, D_MODEL, 3 * CONV_CH), D_MODEL ** -0.5),
        'conv_w': nrm(31, (N_C, CONV_WIDTH, CONV_CH), CONV_WIDTH ** -0.5),
        'conv_b': nrm(32, (N_C, CONV_CH), 0.02),
        'ln_g': gain(33, (N_C, CONV_CH)),
        'ln_b': nrm(34, (N_C, CONV_CH), 0.02),
        'w_out_c': nrm(35, (N_C, CONV_CH, D_MODEL), 0.5 * CONV_CH ** -0.5),
    }


def reference(x_prompt, x_sample, cache_mla_latent, cache_mla_krope, state_s5_re, state_s5_im, state_conv,
              norm_ab, w_in_ab, g_q_lat, w_uq, g_kv_lat, w_uk, w_uv, g_q_nope, g_q_rope, g_k_nope, g_k_rope,
              s5_lam_re, s5_lam_im, s5_log_dt, s5_b_re, s5_b_im, s5_c_re, s5_c_im, s5_d, s5_w_glu, s5_b_glu,
              w_out_ab, norm_c, w_in_c, conv_w, conv_b, ln_g, ln_b, w_out_c):
    pos_p = jnp.arange(x_prompt.shape[1], dtype=jnp.int32)
    pos_s = cache_mla_latent.shape[2] + jnp.arange(x_sample.shape[1], dtype=jnp.int32)
    yp, ys = x_prompt, x_sample
    lat_p, kr_p, re_p, im_p, conv_p = [], [], [], [], []
    lat_s, kr_s, re_s, im_s, conv_s = [], [], [], [], []
    for layer in range(DEPTH):
        i = layer // 2
        if layer % 2 == 0:
            prm = (norm_ab[i], w_in_ab[i], g_q_lat[i], w_uq[i], g_kv_lat[i], w_uk[i], w_uv[i],
                   g_q_nope[i], g_q_rope[i], g_k_nope[i], g_k_rope[i],
                   s5_lam_re[i], s5_lam_im[i], s5_log_dt[i], s5_b_re[i], s5_b_im[i], s5_c_re[i], s5_c_im[i],
                   s5_d[i], s5_w_glu[i], s5_b_glu[i], w_out_ab[i])
            yp, c_p, r_p, sr_p, si_p = ab_layer(yp, pos_p, None, None, None, None, prm)
            ys, c_s, r_s, sr_s, si_s = ab_layer(ys, pos_s, cache_mla_latent[i], cache_mla_krope[i],
                                                state_s5_re[i], state_s5_im[i], prm)
            lat_p.append(c_p); kr_p.append(r_p); re_p.append(sr_p); im_p.append(si_p)
            lat_s.append(c_s); kr_s.append(r_s); re_s.append(sr_s); im_s.append(si_s)
        else:
            prm = (norm_c[i], w_in_c[i], conv_w[i], conv_b[i], ln_g[i], ln_b[i], w_out_c[i])
            yp, cv_p = conv_layer(yp, None, prm)
            ys, cv_s = conv_layer(ys, state_conv[i], prm)
            conv_p.append(cv_p); conv_s.append(cv_s)
    new_lat_p = jnp.stack(lat_p)
    new_kr_p = jnp.stack(kr_p)
    new_re_p = jnp.stack(re_p)
    new_im_p = jnp.stack(im_p)
    new_conv_p = jnp.stack(conv_p)
    new_lat_s = jnp.stack(lat_s)
    new_kr_s = jnp.stack(kr_s)
    new_re_s = jnp.stack(re_s)
    new_im_s = jnp.stack(im_s)
    new_conv_s = jnp.stack(conv_s)
    return (yp, ys, new_lat_p, new_kr_p, new_re_p, new_im_p, new_conv_p,
            new_lat_s, new_kr_s, new_re_s, new_im_s, new_conv_s)
```

```python
import functools
import math

import numpy as np
import jax
import jax.numpy as jnp
from jax import lax
from jax.experimental import pallas as pl
from jax.experimental.pallas import tpu as pltpu

F32 = jnp.float32
BF16 = jnp.bfloat16

D_MODEL = 1024
CHUNK = 64
MLA_HEADS = 8
Q_LORA = 384
KV_LORA = 256
NOPE_DIM = 64
ROPE_DIM = 32
V_DIM = 64
MLA_WIDTH = MLA_HEADS * V_DIM
ROPE_BASE = 10000.0
ATTN_SCALE = (NOPE_DIM + ROPE_DIM) ** -0.5
S5_WIDTH = 512
S5_GROUP = 16
S5_GROUPS = S5_WIDTH // S5_GROUP
S5_STATE = 64
CONV_WIDTH = 31
EPS = 1e-6
NEG_INF = -1e30

LANES = 128
SUBLANES = 8
HEAD_PAD = MLA_HEADS * LANES
S5_HALF_GROUPS = S5_GROUPS // 2
S5_HALF_STATE = S5_HALF_GROUPS * S5_STATE
S5_SUB = 32
CONV_HALO = 32
V7X_VMEM_BYTES = 64 * 1024 * 1024


def _vmem_limit(block_bytes, scratch_bytes=0, temp_bytes=0):
    est = 2 * block_bytes + scratch_bytes + temp_bytes
    return int(min(max(est, 16 * 1024 * 1024), V7X_VMEM_BYTES - 8 * 1024 * 1024))


def _nbytes(shape, dtype):
    return int(np.prod(shape)) * jnp.dtype(dtype).itemsize


def _rms(x, g):
    return x * lax.rsqrt(jnp.mean(x * x, axis=-1, keepdims=True) + EPS) * g


def _silu(x):
    return x * jax.nn.sigmoid(x)


def _dot(a, b):
    return jnp.dot(a, b, preferred_element_type=F32)


def _dot_nt(a, b):
    return lax.dot_general(a, b, (((1,), (1,)), ((), ())), preferred_element_type=F32)


def _ab_in_kernel(x_ref, ng_ref, win_ref, gql_ref, wuq_ref, gkv_ref, wuk_ref, wuv_ref, e2_ref,
                  cq_ref, sq_ref, ck_ref, sk_ref, gq_ref, gqs_ref, gk_ref, gks_ref, gkn_ref,
                  lat_ref, kr_ref, qp_ref, kp_ref, v_ref, gm_ref, u_ref, gs_ref):
    x = x_ref[...]
    h = _rms(x, ng_ref[...]).astype(BF16)

    def proj(lo, hi):
        return _dot(h, win_ref[:, lo:hi])

    q_lat = proj(0, Q_LORA)
    c_kv = proj(Q_LORA, Q_LORA + KV_LORA)
    krg = proj(640, 768)
    gm_ref[...] = proj(768, 1280)
    u_ref[...] = proj(1280, 1792)
    gs_ref[...] = proj(1792, 2304)

    c_n = _rms(c_kv, gkv_ref[...])
    lat_ref[...] = c_n
    cb = c_n.astype(BF16)

    lane = lax.broadcasted_iota(jnp.int32, (1, LANES), 1)
    ms = jnp.sum(jnp.where(lane < ROPE_DIM, krg * krg, 0.0), axis=-1, keepdims=True) * (1.0 / ROPE_DIM)
    kr = lax.rsqrt(ms + EPS) * (krg * (ck_ref[...] * gk_ref[...])
                                + pltpu.roll(krg, LANES - ROPE_DIM, 1) * (sk_ref[...] * gks_ref[...]))
    kr_ref[...] = kr[:, :ROPE_DIM]
    kr_at64 = pltpu.roll(kr, NOPE_DIM, 1)

    qn = _rms(q_lat, gql_ref[...]).astype(BF16)
    qa_tab = cq_ref[...] * (gq_ref[...] * ATTN_SCALE)
    qb_tab = sq_ref[...] * (gqs_ref[...] * ATTN_SCALE)
    for pair in range(MLA_HEADS // 2):
        lo = 2 * LANES * pair
        qa = _dot(qn, wuq_ref[:, lo:lo + 2 * LANES])
        qs = qa * lax.rsqrt(_dot((qa * qa).astype(BF16), e2_ref[...]) + EPS)
        ka = _dot(cb, wuk_ref[:, lo:lo + 2 * LANES])
        ks = ka * lax.rsqrt(_dot((ka * ka).astype(BF16), e2_ref[...]) + EPS)
        for j in range(2):
            sl = slice(LANES * j, LANES * (j + 1))
            s = qs[:, sl]
            qp_ref[:, lo + LANES * j:lo + LANES * (j + 1)] = (
                s * qa_tab + pltpu.roll(s, LANES - ROPE_DIM, 1) * qb_tab).astype(BF16)
            kp_ref[:, lo + LANES * j:lo + LANES * (j + 1)] = (
                ks[:, sl] * gkn_ref[...] + kr_at64).astype(BF16)
    v_ref[...] = _dot(cb, wuv_ref[...]).astype(BF16)


def _ab_in(x2d, pos_tabs, w, *, seq_len, tm):
    n_tok = x2d.shape[0]
    if tm > seq_len:
        pos_tabs = tuple(jnp.tile(p, (tm // seq_len, 1)) for p in pos_tabs)
    n_pos = max(seq_len // tm, 1)
    tok = lambda i: (i, 0)
    pos = lambda i: (i % n_pos, 0)
    const = lambda i: (0, 0)

    def full(a):
        return pl.BlockSpec(a.shape, const)

    cq, sq, ck, sk = pos_tabs
    ins = [x2d, w['norm_g'], w['w_in'], w['g_q_lat'], w['w_uq'], w['g_kv'], w['w_uk'], w['w_uv'], w['e2'],
           cq, sq, ck, sk, w['gq'], w['gqs'], w['gk'], w['gks'], w['gkn']]
    in_specs = [pl.BlockSpec((tm, D_MODEL), tok)] + [full(a) for a in ins[1:9]] \
        + [pl.BlockSpec((tm, LANES), pos)] * 4 + [full(a) for a in ins[13:]]
    outs = [((n_tok, KV_LORA), F32), ((n_tok, ROPE_DIM), F32), ((n_tok, HEAD_PAD), BF16),
            ((n_tok, HEAD_PAD), BF16), ((n_tok, HEAD_PAD), BF16), ((n_tok, MLA_WIDTH), F32),
            ((n_tok, S5_WIDTH), F32), ((n_tok, S5_WIDTH), F32)]
    out_specs = [pl.BlockSpec((tm, s[1]), tok) for s, _ in outs]
    out_shape = [jax.ShapeDtypeStruct(s, d) for s, d in outs]
    blk = sum(_nbytes(a.shape, a.dtype) for a in ins[1:9]) + _nbytes((tm, D_MODEL), F32) \
        + 4 * _nbytes((tm, LANES), F32) + sum(_nbytes((tm, s[1]), d) for s, d in outs)
    return pl.pallas_call(
        _ab_in_kernel, grid=(n_tok // tm,), in_specs=in_specs, out_specs=out_specs, out_shape=out_shape,
        compiler_params=pltpu.CompilerParams(
            dimension_semantics=("arbitrary",),
            vmem_limit_bytes=_vmem_limit(blk, temp_bytes=8 * _nbytes((tm, D_MODEL), F32))),
        name="ab_in")(*ins)


def _attn_prompt_kernel(q_ref, k_ref, v_ref, g_ref, o_ref, *, tq, seq_len):
    nq = seq_len // tq
    row = lax.broadcasted_iota(jnp.int32, (tq, tq), 0) // CHUNK
    col = lax.broadcasted_iota(jnp.int32, (tq, tq), 1) // CHUNK
    diag_mask = col <= row

    def q_body(qi, carry_unused):
        r0 = pl.multiple_of(qi * tq, tq)
        outs = []
        for hh in range(2):
            hs = slice(hh * LANES, (hh + 1) * LANES)
            q = q_ref[0, pl.ds(r0, tq), hs]

            def step(k0, carry, masked):
                m, l, acc = carry
                k = k_ref[0, pl.ds(k0, tq), hs]
                v = v_ref[0, pl.ds(k0, tq), hs]
                s = _dot_nt(q, k)
                if masked:
                    s = jnp.where(diag_mask, s, NEG_INF)
                m_new = jnp.maximum(m, jnp.max(s, axis=-1, keepdims=True))
                p = jnp.exp(s - m_new)
                alpha = jnp.exp(m - m_new)
                l = alpha * l + jnp.sum(p, axis=-1, keepdims=True)
                acc = alpha * acc + _dot(p.astype(BF16), v)
                return m_new, l, acc

            init = (jnp.full((tq, 1), -jnp.inf, F32), jnp.zeros((tq, 1), F32), jnp.zeros((tq, LANES), F32))
            carry = lax.fori_loop(
                0, qi, lambda j, c: step(pl.multiple_of(j * tq, tq), c, False), init)
            m, l, acc = step(r0, carry, True)
            outs.append(acc / l)
        o = outs[0] + pltpu.roll(outs[1], V_DIM, 1)
        g = g_ref[0, pl.ds(r0, tq), :]
        o_ref[0, pl.ds(r0, tq), :] = (o * _silu(g)).astype(BF16)
        return carry_unused

    lax.fori_loop(0, nq, q_body, 0)


def _attn_prompt(qp, kp, vp, gate, *, tq):
    b, t, _ = qp.shape
    pair = lambda i, j: (i, 0, j)
    blk = 3 * _nbytes((t, 2 * LANES), BF16) + _nbytes((t, LANES), F32) + _nbytes((t, LANES), BF16)
    return pl.pallas_call(
        functools.partial(_attn_prompt_kernel, tq=tq, seq_len=t),
        grid=(b, MLA_HEADS // 2),
        in_specs=[pl.BlockSpec((1, t, 2 * LANES), pair)] * 3 + [pl.BlockSpec((1, t, LANES), pair)],
        out_specs=pl.BlockSpec((1, t, LANES), pair),
        out_shape=jax.ShapeDtypeStruct((b, t, MLA_WIDTH), BF16),
        compiler_params=pltpu.CompilerParams(
            dimension_semantics=("arbitrary", "arbitrary"),
            vmem_limit_bytes=_vmem_limit(blk, temp_bytes=8 * _nbytes((tq, tq), F32))),
        name="attn_prompt")(qp, kp, vp, gate)


def _attn_sample_kernel(q_ref, lat_ref, kr_ref, kpn_ref, vn_ref, g_ref, wuk_ref, wuv_ref, e2_ref, gkn_ref,
                        place_ref, o_ref, m_sc, l_sc, acc_sc, *, n_past_blocks):
    j = pl.program_id(1)

    @pl.when(j == 0)
    def _():
        m_sc[...] = jnp.full(m_sc.shape, -jnp.inf, F32)
        l_sc[...] = jnp.zeros(l_sc.shape, F32)
        acc_sc[...] = jnp.zeros(acc_sc.shape, F32)

    def update(h, kp, v):
        q = q_ref[0, :, h * LANES:(h + 1) * LANES]
        s = _dot_nt(q, kp)
        m_old = m_sc[h][:, :1]
        m_new = jnp.maximum(m_old, jnp.max(s, axis=-1, keepdims=True))
        p = jnp.exp(s - m_new)
        alpha = jnp.exp(m_old - m_new)
        l_sc[h] = jnp.broadcast_to(alpha * l_sc[h][:, :1] + jnp.sum(p, axis=-1, keepdims=True), l_sc.shape[1:])
        acc_sc[h] = alpha * acc_sc[h] + _dot(p.astype(BF16), v)
        m_sc[h] = jnp.broadcast_to(m_new, m_sc.shape[1:])

    @pl.when(j < n_past_blocks)
    def _():
        latb = lat_ref[0].astype(BF16)
        kr128 = _dot(kr_ref[0].astype(BF16), place_ref[...])
        for pair in range(MLA_HEADS // 2):
            lo = 2 * LANES * pair
            ka = _dot(latb, wuk_ref[:, lo:lo + 2 * LANES])
            ks = ka * lax.rsqrt(_dot((ka * ka).astype(BF16), e2_ref[...]) + EPS)
            va = _dot(latb, wuv_ref[:, lo:lo + 2 * LANES]).astype(BF16)
            for jj in range(2):
                sl = slice(LANES * jj, LANES * (jj + 1))
                kp = (ks[:, sl] * gkn_ref[...] + kr128).astype(BF16)
                update(2 * pair + jj, kp, va[:, sl])

    @pl.when(j == n_past_blocks)
    def _():
        for h in range(MLA_HEADS):
            hs = slice(h * LANES, (h + 1) * LANES)
            update(h, kpn_ref[0, :, hs], vn_ref[0, :, hs])
        for pair in range(MLA_HEADS // 2):
            o0 = acc_sc[2 * pair] / l_sc[2 * pair][:, :1]
            o1 = acc_sc[2 * pair + 1] / l_sc[2 * pair + 1][:, :1]
            g = g_ref[0, :, pair * LANES:(pair + 1) * LANES]
            o_ref[0, :, pair * LANES:(pair + 1) * LANES] = (
                (o0 + pltpu.roll(o1, V_DIM, 1)) * _silu(g)).astype(BF16)


def _attn_sample(qp, past_lat, past_kr, kp_new, v_new, gate, w, *, tk):
    b, t, _ = qp.shape
    n_past_blocks = past_lat.shape[1] // tk
    cur = lambda i, j: (i, 0, 0)
    past = lambda i, j: (i, jnp.minimum(j, n_past_blocks - 1), 0)
    const = lambda i, j: (0, 0)
    consts = [w['w_uk'], w['w_uv'], w['e2'], w['gkn'], w['place']]
    blk = _nbytes((tk, KV_LORA), F32) + _nbytes((tk, LANES), F32) + 3 * _nbytes((t, HEAD_PAD), BF16) \
        + 2 * _nbytes((t, MLA_WIDTH), F32) + sum(_nbytes(a.shape, a.dtype) for a in consts)
    scratch = 3 * _nbytes((MLA_HEADS, t, LANES), F32)
    return pl.pallas_call(
        functools.partial(_attn_sample_kernel, n_past_blocks=n_past_blocks),
        grid=(b, n_past_blocks + 1),
        in_specs=[pl.BlockSpec((1, t, HEAD_PAD), cur),
                  pl.BlockSpec((1, tk, KV_LORA), past),
                  pl.BlockSpec((1, tk, ROPE_DIM), past),
                  pl.BlockSpec((1, t, HEAD_PAD), cur),
                  pl.BlockSpec((1, t, HEAD_PAD), cur),
                  pl.BlockSpec((1, t, MLA_WIDTH), cur)] + [pl.BlockSpec(a.shape, const) for a in consts],
        out_specs=pl.BlockSpec((1, t, MLA_WIDTH), cur),
        out_shape=jax.ShapeDtypeStruct((b, t, MLA_WIDTH), BF16),
        scratch_shapes=[pltpu.VMEM((MLA_HEADS, t, LANES), F32)] * 3,
        compiler_params=pltpu.CompilerParams(
            dimension_semantics=("arbitrary", "arbitrary"),
            vmem_limit_bytes=_vmem_limit(blk, scratch, temp_bytes=8 * _nbytes((tk, 2 * LANES), F32))),
        name="attn_sample")(qp, past_lat, past_kr, kp_new, v_new, gate, *consts)


def _s5_prep_kernel(lr_ref, li_ref, ldt_ref, br_ref, bi_ref, pos_ref, neg_ref, lam1_ref, bbar_ref, *, sub):
    lr = lr_ref[...]
    li = li_ref[...]
    dt = jnp.exp(ldt_ref[...])
    ar = lr * dt
    ai = li * dt
    k = lax.broadcasted_iota(jnp.int32, (sub, 1), 0).astype(F32)
    mag_p = jnp.exp(k * ar)
    mag_n = jnp.exp(-k * ar)
    ang = k * ai
    c = jnp.cos(ang)
    s = jnp.sin(ang)
    pos_ref[0] = mag_p * c
    pos_ref[1] = mag_p * s
    neg_ref[0] = mag_n * c
    neg_ref[1] = -mag_n * s
    mag1 = jnp.exp(ar)
    l1r = mag1 * jnp.cos(ai)
    l1i = mag1 * jnp.sin(ai)
    lam1_ref[0:1, :] = l1r
    lam1_ref[1:2, :] = l1i
    inv = 1.0 / (lr * lr + li * li)
    nr = l1r - 1.0
    cr = (nr * lr + l1i * li) * inv
    ci = (l1i * lr - nr * li) * inv
    br = br_ref[...]
    bi = bi_ref[...]
    bbar_ref[0] = cr * br - ci * bi
    bbar_ref[1] = cr * bi + ci * br


def _s5_prep(lam_re, lam_im, log_dt, b_re, b_im, *, sub):
    gp = S5_GROUPS * S5_STATE
    lr = lam_re.reshape(1, gp)
    li = lam_im.reshape(1, gp)
    ldt = jnp.broadcast_to(log_dt[:, None], (S5_GROUPS, S5_STATE)).reshape(1, gp)
    br = jnp.transpose(b_re, (2, 0, 1)).reshape(S5_GROUP, gp)
    bi = jnp.transpose(b_im, (2, 0, 1)).reshape(S5_GROUP, gp)
    return pl.pallas_call(
        functools.partial(_s5_prep_kernel, sub=sub),
        out_shape=[jax.ShapeDtypeStruct((2, sub, gp), F32), jax.ShapeDtypeStruct((2, sub, gp), F32),
                   jax.ShapeDtypeStruct((2, gp), F32), jax.ShapeDtypeStruct((2, S5_GROUP, gp), F32)],
        name="s5_prep")(lr, li, ldt, br, bi)


def _s5_kernel(u_ref, gs_ref, s0_ref, bblk_ref, cblk_ref, tri_ref, neg_ref, pos_ref, lam1_ref, d_ref,
               wglu_ref, bglu_ref, o_ref, sfin_ref, carry_sc, x_sc, s_sc, *, tm, sub):
    ti = pl.program_id(1)
    hs = S5_HALF_STATE

    @pl.when(ti == 0)
    def _():
        carry_sc[...] = s0_ref[0]

    u = u_ref[0]
    ub = u.astype(BF16)
    ys = []
    for hf in range(2):
        lanes = slice(hf * hs, (hf + 1) * hs)
        bu = _dot(ub[:, hf * 256:(hf + 1) * 256], bblk_ref[hf])
        nre = neg_ref[0, :, lanes]
        nim = neg_ref[1, :, lanes]
        for c in range(tm // sub):
            rs = slice(c * sub, (c + 1) * sub)
            bre = bu[rs, :hs]
            bim = bu[rs, hs:]
            x_sc[rs, :hs] = (nre * bre - nim * bim).astype(BF16)
            x_sc[rs, hs:] = (nre * bim + nim * bre).astype(BF16)
        cs = _dot(tri_ref[...], x_sc[...])
        cre = carry_sc[:, 2 * hf * hs:(2 * hf + 1) * hs]
        cim = carry_sc[:, (2 * hf + 1) * hs:(2 * hf + 2) * hs]
        l1r = lam1_ref[0:1, lanes]
        l1i = lam1_ref[1:2, lanes]
        pre = pos_ref[0, :, lanes]
        pim = pos_ref[1, :, lanes]
        for c in range(tm // sub):
            rs = slice(c * sub, (c + 1) * sub)
            tre = cs[rs, :hs] + (l1r * cre - l1i * cim)
            tim = cs[rs, hs:] + (l1r * cim + l1i * cre)
            sre = pre * tre - pim * tim
            sim = pre * tim + pim * tre
            s_sc[rs, :hs] = sre.astype(BF16)
            s_sc[rs, hs:] = sim.astype(BF16)
            cre = sre[sub - 1:sub, :]
            cim = sim[sub - 1:sub, :]
        carry_sc[:, 2 * hf * hs:(2 * hf + 1) * hs] = cre
        carry_sc[:, (2 * hf + 1) * hs:(2 * hf + 2) * hs] = cim
        ys.append(_dot(s_sc[...], cblk_ref[hf]))
    y = jnp.concatenate(ys, axis=-1) + d_ref[...] * u
    z = jax.nn.gelu(y)
    z = z * jax.nn.sigmoid(_dot(z.astype(BF16), wglu_ref[...]) + bglu_ref[...])
    o_ref[0] = (z * _silu(gs_ref[0])).astype(BF16)

    @pl.when(ti == pl.num_programs(1) - 1)
    def _():
        sfin_ref[0] = carry_sc[...]


def _s5(u, gate, s0, w, *, tm):
    b, t, _ = u.shape
    sub = S5_SUB
    tokb = lambda i, j: (i, j, 0)
    perb = lambda i, j: (i, 0, 0)
    c2 = lambda i, j: (0, 0)
    c3 = lambda i, j: (0, 0, 0)
    tri = w['tri'][:tm, :tm]
    consts = [w['bblk'], w['cblk'], tri, w['neg'], w['pos'], w['lam1'], w['d'], w['w_glu'], w['b_glu']]
    blk = 2 * _nbytes((tm, S5_WIDTH), F32) + _nbytes((tm, S5_WIDTH), BF16) + 2 * _nbytes((1, 4 * S5_HALF_STATE), F32) \
        + sum(_nbytes(a.shape, a.dtype) for a in consts)
    scratch = _nbytes((8, 4 * S5_HALF_STATE), F32) + 2 * _nbytes((tm, 2 * S5_HALF_STATE), BF16)
    return pl.pallas_call(
        functools.partial(_s5_kernel, tm=tm, sub=sub),
        grid=(b, t // tm),
        in_specs=[pl.BlockSpec((1, tm, S5_WIDTH), tokb), pl.BlockSpec((1, tm, S5_WIDTH), tokb),
                  pl.BlockSpec((1, 1, 4 * S5_HALF_STATE), perb)]
                 + [pl.BlockSpec(a.shape, c3 if a.ndim == 3 else c2) for a in consts],
        out_specs=[pl.BlockSpec((1, tm, S5_WIDTH), tokb), pl.BlockSpec((1, 1, 4 * S5_HALF_STATE), perb)],
        out_shape=[jax.ShapeDtypeStruct((b, t, S5_WIDTH), BF16),
                   jax.ShapeDtypeStruct((b, 1, 4 * S5_HALF_STATE), F32)],
        scratch_shapes=[pltpu.VMEM((1, 4 * S5_HALF_STATE), F32),
                        pltpu.VMEM((tm, 2 * S5_HALF_STATE), BF16),
                        pltpu.VMEM((tm, 2 * S5_HALF_STATE), BF16)],
        compiler_params=pltpu.CompilerParams(
            dimension_semantics=("arbitrary", "arbitrary"),
            vmem_limit_bytes=_vmem_limit(blk, scratch, temp_bytes=4 * _nbytes((tm, 2 * S5_HALF_STATE), F32))),
        name="s5_scan")(u, gate, s0, *consts)


def _ab_out_kernel(a_ref, s_ref, x_ref, wa_ref, ws_ref, o_ref):
    o_ref[...] = _dot(a_ref[...], wa_ref[...]) + _dot(s_ref[...], ws_ref[...]) + x_ref[...]


def _ab_out(mla, s5, x2d, w, *, tm):
    n_tok = x2d.shape[0]
    tok = lambda i: (i, 0)
    const = lambda i: (0, 0)
    blk = 2 * _nbytes((tm, MLA_WIDTH), BF16) + 2 * _nbytes((tm, D_MODEL), F32) \
        + 2 * _nbytes((MLA_WIDTH, D_MODEL), BF16)
    return pl.pallas_call(
        _ab_out_kernel, grid=(n_tok // tm,),
        in_specs=[pl.BlockSpec((tm, MLA_WIDTH), tok), pl.BlockSpec((tm, S5_WIDTH), tok),
                  pl.BlockSpec((tm, D_MODEL), tok),
                  pl.BlockSpec((MLA_WIDTH, D_MODEL), const), pl.BlockSpec((S5_WIDTH, D_MODEL), const)],
        out_specs=pl.BlockSpec((tm, D_MODEL), tok),
        out_shape=jax.ShapeDtypeStruct((n_tok, D_MODEL), F32),
        compiler_params=pltpu.CompilerParams(
            dimension_semantics=("arbitrary",),
            vmem_limit_bytes=_vmem_limit(blk, temp_bytes=2 * _nbytes((tm, D_MODEL), F32))),
        name="ab_out")(mla, s5, x2d, w['w_out_a'], w['w_out_s'])


def _conv_kernel(x_ref, past_ref, ng_ref, win_ref, cw_ref, cb_ref, lng_ref, lnb_ref, wout_ref,
                 y_ref, newc_ref, vext_sc, yc_sc, *, tm):
    ti = pl.program_id(1)

    @pl.when(ti == 0)
    def _():
        vext_sc[0:CONV_HALO, :] = past_ref[0]

    x = x_ref[0]
    h = _rms(x, ng_ref[...]).astype(BF16)
    a = _dot(h, win_ref[:, 0:D_MODEL])
    b = _dot(h, win_ref[:, D_MODEL:2 * D_MODEL])
    vext_sc[CONV_HALO:CONV_HALO + tm, :] = a * jax.nn.sigmoid(b)

    rows = 32
    width = 2 * LANES

    def row_block(i, carry):
        r0 = pl.multiple_of(i * rows, rows)
        for lb in range(D_MODEL // width):
            ls = slice(lb * width, (lb + 1) * width)
            acc = None
            for r in range(SUBLANES):
                z = None
                for a in range(CONV_HALO // SUBLANES):
                    lag = SUBLANES * a + r
                    start = r0 + (CONV_HALO - SUBLANES - SUBLANES * a)
                    term = cw_ref[lag:lag + 1, ls] * vext_sc[pl.ds(start, rows + SUBLANES), ls]
                    z = term if z is None else z + term
                part = z[SUBLANES - r:SUBLANES - r + rows, :]
                acc = part if acc is None else acc + part
            yc_sc[pl.ds(r0, rows), ls] = acc
        return carry

    lax.fori_loop(0, tm // rows, row_block, 0)

    yc = yc_sc[...] + cb_ref[...]
    mu = jnp.mean(yc, axis=-1, keepdims=True)
    xc = yc - mu
    var = jnp.mean(xc * xc, axis=-1, keepdims=True)
    yn = xc * lax.rsqrt(var + EPS) * lng_ref[...] + lnb_ref[...]
    gate = _dot(h, win_ref[:, 2 * D_MODEL:3 * D_MODEL])
    mixed = (_silu(yn) * _silu(gate)).astype(BF16)
    y_ref[0] = _dot(mixed, wout_ref[...]) + x

    tail = vext_sc[tm:tm + CONV_HALO, :]
    vext_sc[0:CONV_HALO, :] = tail
    newc_ref[0] = tail


def _conv_layer(x, past32, w, *, tm):
    b, t, _ = x.shape
    tokb = lambda i, j: (i, j, 0)
    perb = lambda i, j: (i, 0, 0)
    const = lambda i, j: (0, 0)
    consts = [w['norm_g'], w['w_in'], w['conv_w'], w['conv_b'], w['ln_g'], w['ln_b'], w['w_out']]
    blk = 2 * _nbytes((tm, D_MODEL), F32) + 2 * _nbytes((CONV_HALO, D_MODEL), F32) \
        + sum(_nbytes(a.shape, a.dtype) for a in consts)
    scratch = _nbytes((tm + CONV_HALO, D_MODEL), F32) + _nbytes((tm, D_MODEL), F32)
    return pl.pallas_call(
        functools.partial(_conv_kernel, tm=tm),
        grid=(b, t // tm),
        in_specs=[pl.BlockSpec((1, tm, D_MODEL), tokb), pl.BlockSpec((1, CONV_HALO, D_MODEL), perb)]
                 + [pl.BlockSpec(a.shape, const) for a in consts],
        out_specs=[pl.BlockSpec((1, tm, D_MODEL), tokb), pl.BlockSpec((1, CONV_HALO, D_MODEL), perb)],
        out_shape=[jax.ShapeDtypeStruct((b, t, D_MODEL), F32),
                   jax.ShapeDtypeStruct((b, CONV_HALO, D_MODEL), F32)],
        scratch_shapes=[pltpu.VMEM((tm + CONV_HALO, D_MODEL), F32), pltpu.VMEM((tm, D_MODEL), F32)],
        compiler_params=pltpu.CompilerParams(
            dimension_semantics=("arbitrary", "arbitrary"),
            vmem_limit_bytes=_vmem_limit(blk, scratch, temp_bytes=8 * _nbytes((tm, D_MODEL), F32))),
        name="conv_layer")(x, past32, *consts)


def _head_groups(nope, rope, third):
    return jnp.concatenate([nope, rope, third], axis=-1).reshape(nope.shape[0], HEAD_PAD)


def _half_swap(a):
    half = ROPE_DIM // 2
    return jnp.concatenate([a[..., half:], a[..., :half]], axis=-1)


def _lane_pad(a, lo, width=LANES):
    pad = [(0, 0)] * (a.ndim - 1) + [(lo, width - lo - a.shape[-1])]
    return jnp.pad(a, pad)


def _prep_ab_weights(norm_g, w_in, g_q_lat, w_uq, g_kv_lat, w_uk, w_uv, g_q_nope, g_q_rope, g_k_nope, g_k_rope,
                     w_out):
    o_kr = Q_LORA + KV_LORA
    kr_cols = w_in[:, o_kr:o_kr + ROPE_DIM]
    kr_group = jnp.concatenate([kr_cols, _half_swap(kr_cols), jnp.zeros((D_MODEL, LANES - 2 * ROPE_DIM), F32)], -1)
    w_in_p = jnp.concatenate([w_in[:, :o_kr], kr_group, w_in[:, o_kr + ROPE_DIM:]], axis=-1).astype(BF16)
    uq = w_uq.reshape(Q_LORA, MLA_HEADS, NOPE_DIM + ROPE_DIM)
    uq_r = uq[..., NOPE_DIM:]
    w_uq_p = _head_groups(uq[..., :NOPE_DIM], uq_r, _half_swap(uq_r)).astype(BF16)
    zeros_kv = jnp.zeros((KV_LORA, MLA_HEADS, LANES - NOPE_DIM), F32)
    w_uk_p = jnp.concatenate([w_uk, zeros_kv], axis=-1).reshape(KV_LORA, HEAD_PAD).astype(BF16)
    w_uv_p = jnp.concatenate([w_uv, zeros_kv], axis=-1).reshape(KV_LORA, HEAD_PAD).astype(BF16)
    r = np.arange(2 * LANES)
    same = (r[:, None] // LANES) == (r[None, :] // LANES)
    ri, ci = r[:, None] % LANES, r[None, :] % LANES
    e2 = np.where(same & (ri < NOPE_DIM) & (ci < NOPE_DIM), 1.0 / NOPE_DIM, 0.0) \
        + np.where(same & (ri >= NOPE_DIM) & (ri < NOPE_DIM + ROPE_DIM) & (ci >= NOPE_DIM), 1.0 / ROPE_DIM, 0.0)
    place = np.zeros((ROPE_DIM, LANES), np.float32)
    place[np.arange(ROPE_DIM), NOPE_DIM + np.arange(ROPE_DIM)] = 1.0
    row = lambda a: a.reshape(1, -1)
    return dict(
        norm_g=row(norm_g), w_in=w_in_p, g_q_lat=row(g_q_lat), w_uq=w_uq_p, g_kv=row(g_kv_lat),
        w_uk=w_uk_p, w_uv=w_uv_p, e2=jnp.asarray(e2, BF16), place=jnp.asarray(place, BF16),
        gq=row(jnp.concatenate([g_q_nope, g_q_rope, jnp.zeros((ROPE_DIM,), F32)])),
        gqs=row(_lane_pad(_half_swap(g_q_rope), NOPE_DIM)),
        gk=row(_lane_pad(g_k_rope, 0)), gks=row(_lane_pad(_half_swap(g_k_rope), 0)),
        gkn=row(_lane_pad(g_k_nope, 0)),
        w_out_a=w_out[:MLA_WIDTH].astype(BF16), w_out_s=w_out[MLA_WIDTH:].astype(BF16))


def _rope_tables(pos):
    half = ROPE_DIM // 2
    inv = ROPE_BASE ** (-jnp.arange(half, dtype=F32) / half)
    ang = pos.astype(F32)[:, None] * inv[None, :]
    cos = jnp.cos(ang)
    sin = jnp.sin(ang)
    cosf = jnp.concatenate([cos, cos], axis=-1)
    sinf = jnp.concatenate([-sin, sin], axis=-1)
    ones = jnp.ones((pos.shape[0], NOPE_DIM), F32)
    cq = _lane_pad(jnp.concatenate([ones, cosf], axis=-1), 0)
    sq = _lane_pad(sinf, NOPE_DIM)
    return cq, sq, _lane_pad(cosf, 0), _lane_pad(sinf, 0)


def _prep_s5_weights(lam_re, lam_im, log_dt, b_re, b_im, c_re, c_im, d_skip, w_glu, b_glu, *, max_tile):
    pos, neg, lam1, bbar = _s5_prep(lam_re, lam_im, log_dt, b_re, b_im, sub=S5_SUB)
    eye = jnp.eye(S5_HALF_GROUPS, dtype=F32)

    def b_block(bb):
        bb = bb.reshape(S5_GROUP, 2, S5_HALF_GROUPS, S5_STATE)
        return jnp.einsum('nhgp,gk->hgnkp', bb, eye).reshape(2, S5_HALF_GROUPS * S5_GROUP, S5_HALF_STATE)

    def c_block(cc):
        cc = cc.reshape(2, S5_HALF_GROUPS, S5_GROUP, S5_STATE)
        return jnp.einsum('hgnp,gk->hkpgn', cc, eye).reshape(2, S5_HALF_STATE, S5_HALF_GROUPS * S5_GROUP)

    bblk = jnp.concatenate([b_block(bbar[0]), b_block(bbar[1])], axis=-1).astype(BF16)
    cblk = jnp.concatenate([c_block(c_re), c_block(-c_im)], axis=1).astype(BF16)
    r = np.arange(max_tile)
    tri = ((r[:, None] // S5_SUB) == (r[None, :] // S5_SUB)) & (r[None, :] <= r[:, None])
    return dict(bblk=bblk, cblk=cblk, tri=jnp.asarray(tri, BF16), neg=neg, pos=pos, lam1=lam1,
                d=d_skip.reshape(1, -1), w_glu=w_glu.astype(BF16), b_glu=b_glu.reshape(1, -1))


def _pack_state(s_re, s_im):
    b = s_re.shape[0]
    st = jnp.stack([s_re.reshape(b, 2, S5_HALF_STATE), s_im.reshape(b, 2, S5_HALF_STATE)], axis=2)
    return st.reshape(b, 1, 4 * S5_HALF_STATE)


def _unpack_state(st):
    b = st.shape[0]
    st = st.reshape(b, 2, 2, S5_HALF_GROUPS, S5_STATE)
    return (st[:, :, 0].reshape(b, S5_GROUPS, S5_STATE), st[:, :, 1].reshape(b, S5_GROUPS, S5_STATE))


def _ab_layer(x, pos, past, wa, ws, *, tm_in, tm_s5, tm_out, tq, tk):
    b, t, _ = x.shape
    x2d = x.reshape(b * t, D_MODEL)
    lat, kr, qp, kp, vp, gm, u, gs = _ab_in(x2d, _rope_tables(pos), wa, seq_len=t, tm=tm_in)
    r3 = lambda a: a.reshape(b, t, a.shape[-1])
    if past is None:
        mla = _attn_prompt(r3(qp), r3(kp), r3(vp), r3(gm), tq=tq)
        s0 = jnp.zeros((b, 1, 4 * S5_HALF_STATE), F32)
    else:
        past_lat, past_kr, past_re, past_im = past
        mla = _attn_sample(r3(qp), past_lat, past_kr, r3(kp), r3(vp), r3(gm), wa, tk=tk)
        s0 = _pack_state(past_re, past_im)
    s5, sfin = _s5(r3(u), r3(gs), s0, ws, tm=tm_s5)
    y = _ab_out(mla.reshape(b * t, MLA_WIDTH), s5.reshape(b * t, S5_WIDTH), x2d, wa, tm=tm_out)
    fin_re, fin_im = _unpack_state(sfin)
    return y.reshape(b, t, D_MODEL), r3(lat), r3(kr), fin_re, fin_im


def _conv(x, past, wc, *, tm):
    b = x.shape[0]
    if past is None:
        past32 = jnp.zeros((b, CONV_HALO, D_MODEL), F32)
    else:
        past32 = jnp.pad(past, ((0, 0), (CONV_HALO - (CONV_WIDTH - 1), 0), (0, 0)))
    y, newc = _conv_layer(x, past32, wc, tm=tm)
    return y, newc[:, CONV_HALO - (CONV_WIDTH - 1):]


def kernel(x_prompt, x_sample, cache_mla_latent, cache_mla_krope, state_s5_re, state_s5_im, state_conv, norm_ab, w_in_ab, g_q_lat, w_uq, g_kv_lat, w_uk, w_uv, g_q_nope, g_q_rope, g_k_nope, g_k_rope, s5_lam_re, s5_lam_im, s5_log_dt, s5_b_re, s5_b_im, s5_c_re, s5_c_im, s5_d, s5_w_glu, s5_b_glu, w_out_ab, norm_c, w_in_c, conv_w, conv_b, ln_g, ln_b, w_out_c):
    t_p = x_prompt.shape[1]
    t_s = x_sample.shape[1]
    past_len = cache_mla_latent.shape[2]
    pos_p = jnp.arange(t_p, dtype=jnp.int32)
    pos_s = past_len + jnp.arange(t_s, dtype=jnp.int32)

    i = 0
    wa = _prep_ab_weights(norm_ab[i], w_in_ab[i], g_q_lat[i], w_uq[i], g_kv_lat[i], w_uk[i], w_uv[i],
                          g_q_nope[i], g_q_rope[i], g_k_nope[i], g_k_rope[i], w_out_ab[i])
    ws = _prep_s5_weights(s5_lam_re[i], s5_lam_im[i], s5_log_dt[i], s5_b_re[i], s5_b_im[i], s5_c_re[i],
                          s5_c_im[i], s5_d[i], s5_w_glu[i], s5_b_glu[i], max_tile=256)
    yp, lat_p, kr_p, re_p, im_p = _ab_layer(
        x_prompt, pos_p, None, wa, ws, tm_in=256, tm_s5=256, tm_out=512, tq=256, tk=512)
    ys, lat_s, kr_s, re_s, im_s = _ab_layer(
        x_sample, pos_s, (cache_mla_latent[i], cache_mla_krope[i], state_s5_re[i], state_s5_im[i]), wa, ws,
        tm_in=256, tm_s5=t_s, tm_out=512, tq=256, tk=512)

    row = lambda a: a.reshape(1, -1)
    cw = jnp.pad(conv_w[i][::-1], ((0, 1), (0, 0)))
    wc = dict(norm_g=row(norm_c[i]), w_in=w_in_c[i].astype(BF16), conv_w=cw, conv_b=row(conv_b[i]),
              ln_g=row(ln_g[i]), ln_b=row(ln_b[i]), w_out=w_out_c[i].astype(BF16))
    yp, conv_p = _conv(yp, None, wc, tm=256)
    ys, conv_s = _conv(ys, state_conv[i], wc, tm=t_s)

    st = lambda a: a[None]
    return (yp, ys, st(lat_p), st(kr_p), st(re_p), st(im_p), st(conv_p),
            st(lat_s), st(kr_s), st(re_s), st(im_s), st(conv_s))
```

```python
import functools
import math

import numpy as np
import jax
import jax.numpy as jnp
from jax import lax
from jax.experimental import pallas as pl
from jax.experimental.pallas import tpu as pltpu

F32 = jnp.float32
BF16 = jnp.bfloat16

D_MODEL = 1024
CHUNK = 64
MLA_HEADS = 8
Q_LORA = 384
KV_LORA = 256
NOPE_DIM = 64
ROPE_DIM = 32
V_DIM = 64
MLA_WIDTH = MLA_HEADS * V_DIM
ROPE_BASE = 10000.0
ATTN_SCALE = (NOPE_DIM + ROPE_DIM) ** -0.5
S5_WIDTH = 512
S5_GROUP = 16
S5_GROUPS = S5_WIDTH // S5_GROUP
S5_STATE = 64
CONV_WIDTH = 31
EPS = 1e-6
NEG_INF = -1e30
LOG2E = math.log2(math.e)

LANES = 128
SUBLANES = 8
HEAD_PAD = MLA_HEADS * LANES
S5_HALF_GROUPS = S5_GROUPS // 2
S5_HALF_STATE = S5_HALF_GROUPS * S5_STATE
S5_SUB = 32
CONV_HALO = 32
V7X_VMEM_BYTES = 64 * 1024 * 1024


def _vmem_limit(block_bytes, scratch_bytes=0, temp_bytes=0):
    est = 2 * block_bytes + scratch_bytes + temp_bytes
    return int(min(max(est, 16 * 1024 * 1024), V7X_VMEM_BYTES - 8 * 1024 * 1024))


def _nbytes(shape, dtype):
    return int(np.prod(shape)) * jnp.dtype(dtype).itemsize


def _rms(x, g):
    return x * lax.rsqrt(jnp.mean(x * x, axis=-1, keepdims=True) + EPS) * g


def _silu(x):
    return x * jax.nn.sigmoid(x)


def _dot(a, b):
    return jnp.dot(a, b, preferred_element_type=F32)


def _dot_nt(a, b):
    return lax.dot_general(a, b, (((1,), (1,)), ((), ())), preferred_element_type=F32)


def _ab_in_kernel(x_ref, ng_ref, win_ref, gql_ref, wuq_ref, gkv_ref, wuk_ref, wuv_ref, e2_ref,
                  cq_ref, sq_ref, ck_ref, sk_ref, gq_ref, gqs_ref, gk_ref, gks_ref, gkn_ref,
                  lat_ref, kr_ref, qp_ref, kp_ref, vt_ref, gm_ref, u_ref, gs_ref):
    x = x_ref[...]
    h = _rms(x, ng_ref[...]).astype(BF16)

    def proj(lo, hi):
        return _dot(h, win_ref[:, lo:hi])

    q_lat = proj(0, Q_LORA)
    c_kv = proj(Q_LORA, Q_LORA + KV_LORA)
    krg = proj(640, 768)
    gm_ref[...] = proj(768, 1280)
    u_ref[...] = proj(1280, 1792)
    gs_ref[...] = proj(1792, 2304)

    c_n = _rms(c_kv, gkv_ref[...])
    lat_ref[...] = c_n
    cb = c_n.astype(BF16)

    lane = lax.broadcasted_iota(jnp.int32, (1, LANES), 1)
    ms = jnp.sum(jnp.where(lane < ROPE_DIM, krg * krg, 0.0), axis=-1, keepdims=True) * (1.0 / ROPE_DIM)
    kr = lax.rsqrt(ms + EPS) * (krg * (ck_ref[...] * gk_ref[...])
                                + pltpu.roll(krg, LANES - ROPE_DIM, 1) * (sk_ref[...] * gks_ref[...]))
    kr_ref[...] = kr[:, :ROPE_DIM]
    kr_at64 = pltpu.roll(kr, NOPE_DIM, 1)

    qn = _rms(q_lat, gql_ref[...]).astype(BF16)
    qa_tab = cq_ref[...] * (gq_ref[...] * (ATTN_SCALE * LOG2E))
    qb_tab = sq_ref[...] * (gqs_ref[...] * (ATTN_SCALE * LOG2E))
    for pair in range(MLA_HEADS // 2):
        lo = 2 * LANES * pair
        qa = _dot(qn, wuq_ref[:, lo:lo + 2 * LANES])
        qs = qa * lax.rsqrt(_dot((qa * qa).astype(BF16), e2_ref[...]) + EPS)
        ka = _dot(cb, wuk_ref[:, lo:lo + 2 * LANES])
        ks = ka * lax.rsqrt(_dot((ka * ka).astype(BF16), e2_ref[...]) + EPS)
        for j in range(2):
            sl = slice(LANES * j, LANES * (j + 1))
            s = qs[:, sl]
            qp_ref[:, lo + LANES * j:lo + LANES * (j + 1)] = (
                s * qa_tab + pltpu.roll(s, LANES - ROPE_DIM, 1) * qb_tab).astype(BF16)
            kp_ref[:, lo + LANES * j:lo + LANES * (j + 1)] = (
                ks[:, sl] * gkn_ref[...] + kr_at64).astype(BF16)
    vt_ref[...] = _dot_nt(wuv_ref[...], cb).astype(BF16)


def _ab_in(x2d, pos_tabs, w, *, seq_len, tm):
    n_tok = x2d.shape[0]
    if tm > seq_len:
        pos_tabs = tuple(jnp.tile(p, (tm // seq_len, 1)) for p in pos_tabs)
    n_pos = max(seq_len // tm, 1)
    tok = lambda i: (i, 0)
    pos = lambda i: (i % n_pos, 0)
    const = lambda i: (0, 0)

    def full(a):
        return pl.BlockSpec(a.shape, const)

    cq, sq, ck, sk = pos_tabs
    ins = [x2d, w['norm_g'], w['w_in'], w['g_q_lat'], w['w_uq'], w['g_kv'], w['w_uk'], w['w_uvt'], w['e2'],
           cq, sq, ck, sk, w['gq'], w['gqs'], w['gk'], w['gks'], w['gkn']]
    in_specs = [pl.BlockSpec((tm, D_MODEL), tok)] + [full(a) for a in ins[1:9]] \
        + [pl.BlockSpec((tm, LANES), pos)] * 4 + [full(a) for a in ins[13:]]
    outs = [((n_tok, KV_LORA), F32), ((n_tok, ROPE_DIM), F32), ((n_tok, HEAD_PAD), BF16),
            ((n_tok, HEAD_PAD), BF16), ((MLA_WIDTH, n_tok), BF16), ((n_tok, MLA_WIDTH), F32),
            ((n_tok, S5_WIDTH), F32), ((n_tok, S5_WIDTH), F32)]
    vt_index = 4
    out_specs = [pl.BlockSpec((MLA_WIDTH, tm), lambda i: (0, i)) if n == vt_index
                 else pl.BlockSpec((tm, s[1]), tok) for n, (s, _) in enumerate(outs)]
    out_shape = [jax.ShapeDtypeStruct(s, d) for s, d in outs]
    blk = sum(_nbytes(a.shape, a.dtype) for a in ins[1:9]) + _nbytes((tm, D_MODEL), F32) \
        + 4 * _nbytes((tm, LANES), F32) + sum(_nbytes(s, d) for s, d in outs) * tm // n_tok
    return pl.pallas_call(
        _ab_in_kernel, grid=(n_tok // tm,), in_specs=in_specs, out_specs=out_specs, out_shape=out_shape,
        compiler_params=pltpu.CompilerParams(
            dimension_semantics=("arbitrary",),
            vmem_limit_bytes=_vmem_limit(blk, temp_bytes=8 * _nbytes((tm, D_MODEL), F32))),
        name="ab_in")(*ins)


def _ones_row(n):
    return (lax.broadcasted_iota(jnp.int32, (V_DIM, n), 0) == 0).astype(BF16)


def _attn_prompt_kernel(qi_ref, kj_ref, q_ref, k_ref, vt_ref, g_ref, o_ref, vaug_sc, s_sc, *, tq, seq_len):
    n_steps = qi_ref.shape[0]
    ones_row = _ones_row(seq_len)
    for hh in range(2):
        vaug_sc[hh, 0:V_DIM, :] = vt_ref[hh * V_DIM:(hh + 1) * V_DIM, :]
        vaug_sc[hh, V_DIM:2 * V_DIM, :] = ones_row
    key_chunk = lax.broadcasted_iota(jnp.int32, (tq, tq), 0) // CHUNK
    qry_chunk = lax.broadcasted_iota(jnp.int32, (tq, tq), 1) // CHUNK
    diag_mask = key_chunk <= qry_chunk
    lane = lax.broadcasted_iota(jnp.int32, (1, LANES), 1)

    def produce(n, slot):
        qi = qi_ref[n]
        kj = kj_ref[n]
        r0 = pl.multiple_of(qi * tq, tq)
        k0 = pl.multiple_of(kj * tq, tq)
        keep = jnp.logical_or(diag_mask, kj < qi)
        bms = []
        for hh in range(2):
            hs = slice(hh * LANES, (hh + 1) * LANES)
            s = _dot_nt(k_ref[0, pl.ds(k0, tq), hs], q_ref[0, pl.ds(r0, tq), hs])
            s = jnp.where(keep, s, NEG_INF)
            s_sc[slot, hh] = s
            bms.append(jnp.max(s, axis=0, keepdims=True))
        return tuple(bms)

    def consume(n, slot, bms, state):
        qi = qi_ref[n]
        kj = kj_ref[n]
        k0 = pl.multiple_of(kj * tq, tq)
        out = []
        for hh in range(2):
            m, acc = state[hh]
            m = jnp.where(kj == 0, -jnp.inf, m)
            m_new = jnp.maximum(m, bms[hh])
            p = jnp.exp2(s_sc[slot, hh] - m_new)
            alpha = jnp.exp2(m - m_new)
            acc = alpha * acc + _dot(vaug_sc[hh, :, pl.ds(k0, tq)], p.astype(BF16))
            out.append((m_new, acc))

        @pl.when(kj == qi)
        def _():
            r0 = pl.multiple_of(qi * tq, tq)
            outs = []
            for hh in range(2):
                acc_t = out[hh][1].T
                outs.append(acc_t * (1.0 / acc_t[:, V_DIM:V_DIM + 1]))
            o = jnp.where(lane < V_DIM, outs[0], pltpu.roll(outs[1], V_DIM, 1))
            o_ref[0, pl.ds(r0, tq), :] = (o * _silu(g_ref[0, pl.ds(r0, tq), :])).astype(BF16)

        return tuple(out)

    def double_step(t, carry):
        bms_a, state = carry
        bms_b = produce(2 * t + 1, 1)
        state = consume(2 * t, 0, bms_a, state)
        bms_a = produce(2 * t + 2, 0)
        state = consume(2 * t + 1, 1, bms_b, state)
        return bms_a, state

    state = tuple((jnp.full((1, tq), -jnp.inf, F32), jnp.zeros((2 * V_DIM, tq), F32)) for _ in range(2))
    n_double = (n_steps - 1) // 2
    bms_a, state = lax.fori_loop(0, n_double, double_step, (produce(0, 0), state))
    if n_steps - 1 == 2 * n_double:
        consume(n_steps - 1, 0, bms_a, state)
    else:
        bms_b = produce(n_steps - 1, 1)
        state = consume(n_steps - 2, 0, bms_a, state)
        consume(n_steps - 1, 1, bms_b, state)


def _attn_prompt(qp, kp, vt, gate, *, tq):
    b, t, _ = qp.shape
    nq = t // tq
    steps = [(qi, kj) for qi in range(nq) for kj in range(qi + 1)]
    qi_tab = jnp.asarray([s[0] for s in steps], jnp.int32)
    kj_tab = jnp.asarray([s[1] for s in steps], jnp.int32)
    pair = lambda i, j, qt, kt: (i, 0, j)
    blk = 2 * _nbytes((t, 2 * LANES), BF16) + _nbytes((LANES, t), BF16) + _nbytes((t, LANES), F32) \
        + _nbytes((t, LANES), BF16)
    scratch = _nbytes((2, LANES, t), BF16) + _nbytes((2, 2, tq, tq), F32)
    return pl.pallas_call(
        functools.partial(_attn_prompt_kernel, tq=tq, seq_len=t),
        grid_spec=pltpu.PrefetchScalarGridSpec(
            num_scalar_prefetch=2,
            grid=(b, MLA_HEADS // 2),
            in_specs=[pl.BlockSpec((1, t, 2 * LANES), pair), pl.BlockSpec((1, t, 2 * LANES), pair),
                      pl.BlockSpec((LANES, t), lambda i, j, qt, kt: (j, i)), pl.BlockSpec((1, t, LANES), pair)],
            out_specs=pl.BlockSpec((1, t, LANES), pair),
            scratch_shapes=[pltpu.VMEM((2, LANES, t), BF16), pltpu.VMEM((2, 2, tq, tq), F32)]),
        out_shape=jax.ShapeDtypeStruct((b, t, MLA_WIDTH), BF16),
        compiler_params=pltpu.CompilerParams(
            dimension_semantics=("arbitrary", "arbitrary"),
            vmem_limit_bytes=_vmem_limit(blk, scratch, temp_bytes=12 * _nbytes((tq, tq), F32))),
        name="attn_prompt")(qi_tab, kj_tab, qp, kp, vt, gate)


def _attn_sample_kernel(q_ref, lat_ref, kr_ref, kpn_ref, vtn_ref, g_ref, wuk_ref, wuvt_ref, e2_ref, gkn2_ref,
                        place_ref, o_ref, qbd_sc, m_sc, acc_sc, *, n_past_blocks):
    j = pl.program_id(1)
    n_pairs = MLA_HEADS // 2
    t = q_ref.shape[1]
    lane = lax.broadcasted_iota(jnp.int32, (1, LANES), 1)

    @pl.when(j == 0)
    def _():
        m_sc[...] = jnp.full(m_sc.shape, -jnp.inf, F32)
        acc_sc[...] = jnp.zeros(acc_sc.shape, F32)
        q = q_ref[0].astype(F32)
        q_t = jnp.concatenate([q, jnp.zeros((LANES - t, HEAD_PAD), F32)], axis=0).T
        for p in range(n_pairs):
            top = q_t[2 * LANES * p:2 * LANES * p + LANES, :]
            bot = pltpu.roll(q_t[2 * LANES * p + LANES:2 * LANES * (p + 1), :], t, 1)
            qbd_sc[p] = jnp.concatenate([top, bot], axis=0).astype(BF16)

    def update(p, kp_pair, vt_pair):
        n = kp_pair.shape[0]
        s = _dot(kp_pair, qbd_sc[p])
        m_old = m_sc[p]
        m_new = jnp.maximum(m_old, jnp.max(s, axis=0, keepdims=True))
        pr = jnp.exp2(s - m_new)
        alpha = jnp.exp2(m_old - m_new)
        ones_row = _ones_row(n)
        vaug = jnp.concatenate([vt_pair[0:V_DIM, :], ones_row, vt_pair[V_DIM:2 * V_DIM, :], ones_row], axis=0)
        acc_sc[p] = alpha * acc_sc[p] + _dot(vaug, pr.astype(BF16))
        m_sc[p] = m_new

    @pl.when(j < n_past_blocks)
    def _():
        latb = lat_ref[0].astype(BF16)
        kr128 = _dot(kr_ref[0].astype(BF16), place_ref[...])
        kr256 = jnp.concatenate([kr128, kr128], axis=1)
        vt_all = _dot_nt(wuvt_ref[...], latb).astype(BF16)
        kas = [_dot(latb, wuk_ref[:, 2 * LANES * p:2 * LANES * (p + 1)]) for p in range(n_pairs)]
        mss = [_dot((ka * ka).astype(BF16), e2_ref[...]) for ka in kas]
        kps = [(ka * lax.rsqrt(ms + EPS) * gkn2_ref[...] + kr256).astype(BF16) for ka, ms in zip(kas, mss)]
        for p in range(n_pairs):
            update(p, kps[p], vt_all[LANES * p:LANES * (p + 1), :])

    @pl.when(j == n_past_blocks)
    def _():
        for p in range(n_pairs):
            update(p, kpn_ref[0, :, 2 * LANES * p:2 * LANES * (p + 1)], vtn_ref[0, LANES * p:LANES * (p + 1), :])
        for p in range(n_pairs):
            acc_t = acc_sc[p].T
            a0 = acc_t[0:t, 0:LANES]
            a1 = acc_t[t:2 * t, LANES:2 * LANES]
            o0 = a0 * (1.0 / a0[:, V_DIM:V_DIM + 1])
            o1 = a1 * (1.0 / a1[:, V_DIM:V_DIM + 1])
            o = jnp.where(lane < V_DIM, o0, pltpu.roll(o1, V_DIM, 1))
            g = g_ref[0, :, p * LANES:(p + 1) * LANES]
            o_ref[0, :, p * LANES:(p + 1) * LANES] = (o * _silu(g)).astype(BF16)


def _attn_sample(qp, past_lat, past_kr, kp_new, vt_new, gate, w, *, tk):
    b, t, _ = qp.shape
    assert 2 * t == LANES, "two heads' queries share one 128-lane group"
    n_past_blocks = past_lat.shape[1] // tk
    n_pairs = MLA_HEADS // 2
    cur = lambda i, j: (i, 0, 0)
    past = lambda i, j: (i, jnp.minimum(j, n_past_blocks - 1), 0)
    const = lambda i, j: (0, 0)
    consts = [w['w_uk'], w['w_uvt'], w['e2'], w['gkn2'], w['place']]
    blk = _nbytes((tk, KV_LORA), F32) + _nbytes((tk, LANES), F32) + 2 * _nbytes((t, HEAD_PAD), BF16) \
        + _nbytes((MLA_WIDTH, LANES), BF16) + 2 * _nbytes((t, MLA_WIDTH), F32) \
        + sum(_nbytes(a.shape, a.dtype) for a in consts)
    scratch = _nbytes((n_pairs, 2 * LANES, LANES), BF16) + _nbytes((n_pairs, SUBLANES, LANES), F32) \
        + _nbytes((n_pairs, 2 * LANES, LANES), F32)
    return pl.pallas_call(
        functools.partial(_attn_sample_kernel, n_past_blocks=n_past_blocks),
        grid=(b, n_past_blocks + 1),
        in_specs=[pl.BlockSpec((1, t, HEAD_PAD), cur),
                  pl.BlockSpec((1, tk, KV_LORA), past),
                  pl.BlockSpec((1, tk, ROPE_DIM), past),
                  pl.BlockSpec((1, t, HEAD_PAD), cur),
                  pl.BlockSpec((1, MLA_WIDTH, t), cur),
                  pl.BlockSpec((1, t, MLA_WIDTH), cur)] + [pl.BlockSpec(a.shape, const) for a in consts],
        out_specs=pl.BlockSpec((1, t, MLA_WIDTH), cur),
        out_shape=jax.ShapeDtypeStruct((b, t, MLA_WIDTH), BF16),
        scratch_shapes=[pltpu.VMEM((n_pairs, 2 * LANES, LANES), BF16),
                        pltpu.VMEM((n_pairs, 1, LANES), F32),
                        pltpu.VMEM((n_pairs, 2 * LANES, LANES), F32)],
        compiler_params=pltpu.CompilerParams(
            dimension_semantics=("arbitrary", "arbitrary"),
            vmem_limit_bytes=_vmem_limit(blk, scratch, temp_bytes=12 * _nbytes((tk, 2 * LANES), F32))),
        name="attn_sample")(qp, past_lat, past_kr, kp_new, vt_new, gate, *consts)


def _s5_prep_kernel(lr_ref, li_ref, ldt_ref, br_ref, bi_ref, pos_ref, neg_ref, lam1_ref, bbar_ref, *, sub):
    lr = lr_ref[...]
    li = li_ref[...]
    dt = jnp.exp(ldt_ref[...])
    ar = lr * dt
    ai = li * dt
    k = lax.broadcasted_iota(jnp.int32, (sub, 1), 0).astype(F32)
    mag_p = jnp.exp(k * ar)
    mag_n = jnp.exp(-k * ar)
    ang = k * ai
    c = jnp.cos(ang)
    s = jnp.sin(ang)
    pos_ref[0] = mag_p * c
    pos_ref[1] = mag_p * s
    neg_ref[0] = mag_n * c
    neg_ref[1] = -mag_n * s
    mag1 = jnp.exp(ar)
    l1r = mag1 * jnp.cos(ai)
    l1i = mag1 * jnp.sin(ai)
    lam1_ref[0:1, :] = l1r
    lam1_ref[1:2, :] = l1i
    inv = 1.0 / (lr * lr + li * li)
    nr = l1r - 1.0
    cr = (nr * lr + l1i * li) * inv
    ci = (l1i * lr - nr * li) * inv
    br = br_ref[...]
    bi = bi_ref[...]
    bbar_ref[0] = cr * br - ci * bi
    bbar_ref[1] = cr * bi + ci * br


def _s5_prep(lam_re, lam_im, log_dt, b_re, b_im, *, sub):
    gp = S5_GROUPS * S5_STATE
    lr = lam_re.reshape(1, gp)
    li = lam_im.reshape(1, gp)
    ldt = jnp.broadcast_to(log_dt[:, None], (S5_GROUPS, S5_STATE)).reshape(1, gp)
    br = jnp.transpose(b_re, (2, 0, 1)).reshape(S5_GROUP, gp)
    bi = jnp.transpose(b_im, (2, 0, 1)).reshape(S5_GROUP, gp)
    return pl.pallas_call(
        functools.partial(_s5_prep_kernel, sub=sub),
        out_shape=[jax.ShapeDtypeStruct((2, sub, gp), F32), jax.ShapeDtypeStruct((2, sub, gp), F32),
                   jax.ShapeDtypeStruct((2, gp), F32), jax.ShapeDtypeStruct((2, S5_GROUP, gp), F32)],
        name="s5_prep")(lr, li, ldt, br, bi)


def _s5_kernel(u_ref, gs_ref, s0_ref, bblk_ref, cblk_ref, tri_ref, neg_ref, pos_ref, lam1_ref, d_ref,
               wglu_ref, bglu_ref, o_ref, sfin_ref, carry_sc, x_sc, s_sc, *, tm, sub):
    ti = pl.program_id(1)
    hs = S5_HALF_STATE

    @pl.when(ti == 0)
    def _():
        carry_sc[...] = s0_ref[0]

    u = u_ref[0]
    ub = u.astype(BF16)
    ys = []
    for hf in range(2):
        lanes = slice(hf * hs, (hf + 1) * hs)
        bu = _dot(ub[:, hf * 256:(hf + 1) * 256], bblk_ref[hf])
        nre = neg_ref[0, :, lanes]
        nim = neg_ref[1, :, lanes]
        for c in range(tm // sub):
            rs = slice(c * sub, (c + 1) * sub)
            bre = bu[rs, :hs]
            bim = bu[rs, hs:]
            x_sc[rs, :hs] = (nre * bre - nim * bim).astype(BF16)
            x_sc[rs, hs:] = (nre * bim + nim * bre).astype(BF16)
        cs = _dot(tri_ref[...], x_sc[...])
        cre = carry_sc[:, 2 * hf * hs:(2 * hf + 1) * hs]
        cim = carry_sc[:, (2 * hf + 1) * hs:(2 * hf + 2) * hs]
        l1r = lam1_ref[0:1, lanes]
        l1i = lam1_ref[1:2, lanes]
        pre = pos_ref[0, :, lanes]
        pim = pos_ref[1, :, lanes]
        for c in range(tm // sub):
            rs = slice(c * sub, (c + 1) * sub)
            tre = cs[rs, :hs] + (l1r * cre - l1i * cim)
            tim = cs[rs, hs:] + (l1r * cim + l1i * cre)
            sre = pre * tre - pim * tim
            sim = pre * tim + pim * tre
            s_sc[rs, :hs] = sre.astype(BF16)
            s_sc[rs, hs:] = sim.astype(BF16)
            cre = sre[sub - 1:sub, :]
            cim = sim[sub - 1:sub, :]
        carry_sc[:, 2 * hf * hs:(2 * hf + 1) * hs] = cre
        carry_sc[:, (2 * hf + 1) * hs:(2 * hf + 2) * hs] = cim
        ys.append(_dot(s_sc[...], cblk_ref[hf]))
    y = jnp.concatenate(ys, axis=-1) + d_ref[...] * u
    z = jax.nn.gelu(y)
    z = z * jax.nn.sigmoid(_dot(z.astype(BF16), wglu_ref[...]) + bglu_ref[...])
    o_ref[0] = (z * _silu(gs_ref[0])).astype(BF16)

    @pl.when(ti == pl.num_programs(1) - 1)
    def _():
        sfin_ref[0] = carry_sc[...]


def _s5(u, gate, s0, w, *, tm):
    b, t, _ = u.shape
    sub = S5_SUB
    tokb = lambda i, j: (i, j, 0)
    perb = lambda i, j: (i, 0, 0)
    c2 = lambda i, j: (0, 0)
    c3 = lambda i, j: (0, 0, 0)
    tri = w['tri'][:tm, :tm]
    consts = [w['bblk'], w['cblk'], tri, w['neg'], w['pos'], w['lam1'], w['d'], w['w_glu'], w['b_glu']]
    blk = 2 * _nbytes((tm, S5_WIDTH), F32) + _nbytes((tm, S5_WIDTH), BF16) + 2 * _nbytes((1, 4 * S5_HALF_STATE), F32) \
        + sum(_nbytes(a.shape, a.dtype) for a in consts)
    scratch = _nbytes((8, 4 * S5_HALF_STATE), F32) + 2 * _nbytes((tm, 2 * S5_HALF_STATE), BF16)
    return pl.pallas_call(
        functools.partial(_s5_kernel, tm=tm, sub=sub),
        grid=(b, t // tm),
        in_specs=[pl.BlockSpec((1, tm, S5_WIDTH), tokb), pl.BlockSpec((1, tm, S5_WIDTH), tokb),
                  pl.BlockSpec((1, 1, 4 * S5_HALF_STATE), perb)]
                 + [pl.BlockSpec(a.shape, c3 if a.ndim == 3 else c2) for a in consts],
        out_specs=[pl.BlockSpec((1, tm, S5_WIDTH), tokb), pl.BlockSpec((1, 1, 4 * S5_HALF_STATE), perb)],
        out_shape=[jax.ShapeDtypeStruct((b, t, S5_WIDTH), BF16),
                   jax.ShapeDtypeStruct((b, 1, 4 * S5_HALF_STATE), F32)],
        scratch_shapes=[pltpu.VMEM((1, 4 * S5_HALF_STATE), F32),
                        pltpu.VMEM((tm, 2 * S5_HALF_STATE), BF16),
                        pltpu.VMEM((tm, 2 * S5_HALF_STATE), BF16)],
        compiler_params=pltpu.CompilerParams(
            dimension_semantics=("arbitrary", "arbitrary"),
            vmem_limit_bytes=_vmem_limit(blk, scratch, temp_bytes=4 * _nbytes((tm, 2 * S5_HALF_STATE), F32))),
        name="s5_scan")(u, gate, s0, *consts)


def _ab_out_kernel(a_ref, s_ref, x_ref, wa_ref, ws_ref, o_ref):
    o_ref[...] = _dot(a_ref[...], wa_ref[...]) + _dot(s_ref[...], ws_ref[...]) + x_ref[...]


def _ab_out(mla, s5, x2d, w, *, tm):
    n_tok = x2d.shape[0]
    tok = lambda i: (i, 0)
    const = lambda i: (0, 0)
    blk = 2 * _nbytes((tm, MLA_WIDTH), BF16) + 2 * _nbytes((tm, D_MODEL), F32) \
        + 2 * _nbytes((MLA_WIDTH, D_MODEL), BF16)
    return pl.pallas_call(
        _ab_out_kernel, grid=(n_tok // tm,),
        in_specs=[pl.BlockSpec((tm, MLA_WIDTH), tok), pl.BlockSpec((tm, S5_WIDTH), tok),
                  pl.BlockSpec((tm, D_MODEL), tok),
                  pl.BlockSpec((MLA_WIDTH, D_MODEL), const), pl.BlockSpec((S5_WIDTH, D_MODEL), const)],
        out_specs=pl.BlockSpec((tm, D_MODEL), tok),
        out_shape=jax.ShapeDtypeStruct((n_tok, D_MODEL), F32),
        compiler_params=pltpu.CompilerParams(
            dimension_semantics=("arbitrary",),
            vmem_limit_bytes=_vmem_limit(blk, temp_bytes=2 * _nbytes((tm, D_MODEL), F32))),
        name="ab_out")(mla, s5, x2d, w['w_out_a'], w['w_out_s'])


def _conv_kernel(x_ref, past_ref, ng_ref, win_ref, cw_ref, cb_ref, lng_ref, lnb_ref, wout_ref,
                 y_ref, newc_ref, vext_sc, yc_sc, *, tm):
    ti = pl.program_id(1)

    @pl.when(ti == 0)
    def _():
        vext_sc[0:CONV_HALO, :] = past_ref[0]

    x = x_ref[0]
    h = _rms(x, ng_ref[...]).astype(BF16)
    a = _dot(h, win_ref[:, 0:D_MODEL])
    b = _dot(h, win_ref[:, D_MODEL:2 * D_MODEL])
    vext_sc[CONV_HALO:CONV_HALO + tm, :] = a * jax.nn.sigmoid(b)

    rows = 32
    width = 2 * LANES

    def row_block(i, carry):
        r0 = pl.multiple_of(i * rows, rows)
        for lb in range(D_MODEL // width):
            ls = slice(lb * width, (lb + 1) * width)
            acc = None
            for r in range(SUBLANES):
                z = None
                for a in range(CONV_HALO // SUBLANES):
                    lag = SUBLANES * a + r
                    start = r0 + (CONV_HALO - SUBLANES - SUBLANES * a)
                    term = cw_ref[lag:lag + 1, ls] * vext_sc[pl.ds(start, rows + SUBLANES), ls]
                    z = term if z is None else z + term
                part = z[SUBLANES - r:SUBLANES - r + rows, :]
                acc = part if acc is None else acc + part
            yc_sc[pl.ds(r0, rows), ls] = acc
        return carry

    lax.fori_loop(0, tm // rows, row_block, 0)

    yc = yc_sc[...] + cb_ref[...]
    mu = jnp.mean(yc, axis=-1, keepdims=True)
    xc = yc - mu
    var = jnp.mean(xc * xc, axis=-1, keepdims=True)
    yn = xc * lax.rsqrt(var + EPS) * lng_ref[...] + lnb_ref[...]
    gate = _dot(h, win_ref[:, 2 * D_MODEL:3 * D_MODEL])
    mixed = (_silu(yn) * _silu(gate)).astype(BF16)
    y_ref[0] = _dot(mixed, wout_ref[...]) + x

    tail = vext_sc[tm:tm + CONV_HALO, :]
    vext_sc[0:CONV_HALO, :] = tail
    newc_ref[0] = tail


def _conv_layer(x, past32, w, *, tm):
    b, t, _ = x.shape
    tokb = lambda i, j: (i, j, 0)
    perb = lambda i, j: (i, 0, 0)
    const = lambda i, j: (0, 0)
    consts = [w['norm_g'], w['w_in'], w['conv_w'], w['conv_b'], w['ln_g'], w['ln_b'], w['w_out']]
    blk = 2 * _nbytes((tm, D_MODEL), F32) + 2 * _nbytes((CONV_HALO, D_MODEL), F32) \
        + sum(_nbytes(a.shape, a.dtype) for a in consts)
    scratch = _nbytes((tm + CONV_HALO, D_MODEL), F32) + _nbytes((tm, D_MODEL), F32)
    return pl.pallas_call(
        functools.partial(_conv_kernel, tm=tm),
        grid=(b, t // tm),
        in_specs=[pl.BlockSpec((1, tm, D_MODEL), tokb), pl.BlockSpec((1, CONV_HALO, D_MODEL), perb)]
                 + [pl.BlockSpec(a.shape, const) for a in consts],
        out_specs=[pl.BlockSpec((1, tm, D_MODEL), tokb), pl.BlockSpec((1, CONV_HALO, D_MODEL), perb)],
        out_shape=[jax.ShapeDtypeStruct((b, t, D_MODEL), F32),
                   jax.ShapeDtypeStruct((b, CONV_HALO, D_MODEL), F32)],
        scratch_shapes=[pltpu.VMEM((tm + CONV_HALO, D_MODEL), F32), pltpu.VMEM((tm, D_MODEL), F32)],
        compiler_params=pltpu.CompilerParams(
            dimension_semantics=("arbitrary", "arbitrary"),
            vmem_limit_bytes=_vmem_limit(blk, scratch, temp_bytes=8 * _nbytes((tm, D_MODEL), F32))),
        name="conv_layer")(x, past32, *consts)


def _head_groups(nope, rope, third):
    return jnp.concatenate([nope, rope, third], axis=-1).reshape(nope.shape[0], HEAD_PAD)


def _half_swap(a):
    half = ROPE_DIM // 2
    return jnp.concatenate([a[..., half:], a[..., :half]], axis=-1)


def _lane_pad(a, lo, width=LANES):
    pad = [(0, 0)] * (a.ndim - 1) + [(lo, width - lo - a.shape[-1])]
    return jnp.pad(a, pad)


def _prep_ab_weights(norm_g, w_in, g_q_lat, w_uq, g_kv_lat, w_uk, w_uv, g_q_nope, g_q_rope, g_k_nope, g_k_rope,
                     w_out):
    o_kr = Q_LORA + KV_LORA
    kr_cols = w_in[:, o_kr:o_kr + ROPE_DIM]
    kr_group = jnp.concatenate([kr_cols, _half_swap(kr_cols), jnp.zeros((D_MODEL, LANES - 2 * ROPE_DIM), F32)], -1)
    w_in_p = jnp.concatenate([w_in[:, :o_kr], kr_group, w_in[:, o_kr + ROPE_DIM:]], axis=-1).astype(BF16)
    uq = w_uq.reshape(Q_LORA, MLA_HEADS, NOPE_DIM + ROPE_DIM)
    uq_r = uq[..., NOPE_DIM:]
    w_uq_p = _head_groups(uq[..., :NOPE_DIM], uq_r, _half_swap(uq_r)).astype(BF16)
    zeros_kv = jnp.zeros((KV_LORA, MLA_HEADS, LANES - NOPE_DIM), F32)
    w_uk_p = jnp.concatenate([w_uk, zeros_kv], axis=-1).reshape(KV_LORA, HEAD_PAD).astype(BF16)
    w_uv_t = w_uv.reshape(KV_LORA, MLA_WIDTH).T.astype(BF16)
    r = np.arange(2 * LANES)
    same = (r[:, None] // LANES) == (r[None, :] // LANES)
    ri, ci = r[:, None] % LANES, r[None, :] % LANES
    e2 = np.where(same & (ri < NOPE_DIM) & (ci < NOPE_DIM), 1.0 / NOPE_DIM, 0.0) \
        + np.where(same & (ri >= NOPE_DIM) & (ri < NOPE_DIM + ROPE_DIM) & (ci >= NOPE_DIM), 1.0 / ROPE_DIM, 0.0)
    place = np.zeros((ROPE_DIM, LANES), np.float32)
    place[np.arange(ROPE_DIM), NOPE_DIM + np.arange(ROPE_DIM)] = 1.0
    row = lambda a: a.reshape(1, -1)
    return dict(
        norm_g=row(norm_g), w_in=w_in_p, g_q_lat=row(g_q_lat), w_uq=w_uq_p, g_kv=row(g_kv_lat),
        w_uk=w_uk_p, w_uvt=w_uv_t, e2=jnp.asarray(e2, BF16), place=jnp.asarray(place, BF16),
        gq=row(jnp.concatenate([g_q_nope, g_q_rope, jnp.zeros((ROPE_DIM,), F32)])),
        gqs=row(_lane_pad(_half_swap(g_q_rope), NOPE_DIM)),
        gk=row(_lane_pad(g_k_rope, 0)), gks=row(_lane_pad(_half_swap(g_k_rope), 0)),
        gkn=row(_lane_pad(g_k_nope, 0)), gkn2=row(jnp.tile(_lane_pad(g_k_nope, 0), 2)),
        w_out_a=w_out[:MLA_WIDTH].astype(BF16), w_out_s=w_out[MLA_WIDTH:].astype(BF16))


def _rope_tables(pos):
    half = ROPE_DIM // 2
    inv = ROPE_BASE ** (-jnp.arange(half, dtype=F32) / half)
    ang = pos.astype(F32)[:, None] * inv[None, :]
    cos = jnp.cos(ang)
    sin = jnp.sin(ang)
    cosf = jnp.concatenate([cos, cos], axis=-1)
    sinf = jnp.concatenate([-sin, sin], axis=-1)
    ones = jnp.ones((pos.shape[0], NOPE_DIM), F32)
    cq = _lane_pad(jnp.concatenate([ones, cosf], axis=-1), 0)
    sq = _lane_pad(sinf, NOPE_DIM)
    return cq, sq, _lane_pad(cosf, 0), _lane_pad(sinf, 0)


def _prep_s5_weights(lam_re, lam_im, log_dt, b_re, b_im, c_re, c_im, d_skip, w_glu, b_glu, *, max_tile):
    pos, neg, lam1, bbar = _s5_prep(lam_re, lam_im, log_dt, b_re, b_im, sub=S5_SUB)
    eye = jnp.eye(S5_HALF_GROUPS, dtype=F32)

    def b_block(bb):
        bb = bb.reshape(S5_GROUP, 2, S5_HALF_GROUPS, S5_STATE)
        return jnp.einsum('nhgp,gk->hgnkp', bb, eye).reshape(2, S5_HALF_GROUPS * S5_GROUP, S5_HALF_STATE)

    def c_block(cc):
        cc = cc.reshape(2, S5_HALF_GROUPS, S5_GROUP, S5_STATE)
        return jnp.einsum('hgnp,gk->hkpgn', cc, eye).reshape(2, S5_HALF_STATE, S5_HALF_GROUPS * S5_GROUP)

    bblk = jnp.concatenate([b_block(bbar[0]), b_block(bbar[1])], axis=-1).astype(BF16)
    cblk = jnp.concatenate([c_block(c_re), c_block(-c_im)], axis=1).astype(BF16)
    r = np.arange(max_tile)
    tri = ((r[:, None] // S5_SUB) == (r[None, :] // S5_SUB)) & (r[None, :] <= r[:, None])
    return dict(bblk=bblk, cblk=cblk, tri=jnp.asarray(tri, BF16), neg=neg, pos=pos, lam1=lam1,
                d=d_skip.reshape(1, -1), w_glu=w_glu.astype(BF16), b_glu=b_glu.reshape(1, -1))


def _pack_state(s_re, s_im):
    b = s_re.shape[0]
    st = jnp.stack([s_re.reshape(b, 2, S5_HALF_STATE), s_im.reshape(b, 2, S5_HALF_STATE)], axis=2)
    return st.reshape(b, 1, 4 * S5_HALF_STATE)


def _unpack_state(st):
    b = st.shape[0]
    st = st.reshape(b, 2, 2, S5_HALF_GROUPS, S5_STATE)
    return (st[:, :, 0].reshape(b, S5_GROUPS, S5_STATE), st[:, :, 1].reshape(b, S5_GROUPS, S5_STATE))


def _ab_layer(x, pos, past, wa, ws, *, tm_in, tm_s5, tm_out, tq, tk):
    b, t, _ = x.shape
    x2d = x.reshape(b * t, D_MODEL)
    lat, kr, qp, kp, vt, gm, u, gs = _ab_in(x2d, _rope_tables(pos), wa, seq_len=t, tm=tm_in)
    r3 = lambda a: a.reshape(b, t, a.shape[-1])
    if past is None:
        mla = _attn_prompt(r3(qp), r3(kp), vt, r3(gm), tq=tq)
        s0 = jnp.zeros((b, 1, 4 * S5_HALF_STATE), F32)
    else:
        past_lat, past_kr, past_re, past_im = past
        vt_b = jnp.transpose(vt.reshape(MLA_WIDTH, b, t), (1, 0, 2))
        mla = _attn_sample(r3(qp), past_lat, past_kr, r3(kp), vt_b, r3(gm), wa, tk=tk)
        s0 = _pack_state(past_re, past_im)
    s5, sfin = _s5(r3(u), r3(gs), s0, ws, tm=tm_s5)
    y = _ab_out(mla.reshape(b * t, MLA_WIDTH), s5.reshape(b * t, S5_WIDTH), x2d, wa, tm=tm_out)
    fin_re, fin_im = _unpack_state(sfin)
    return y.reshape(b, t, D_MODEL), r3(lat), r3(kr), fin_re, fin_im


def _conv(x, past, wc, *, tm):
    b = x.shape[0]
    if past is None:
        past32 = jnp.zeros((b, CONV_HALO, D_MODEL), F32)
    else:
        past32 = jnp.pad(past, ((0, 0), (CONV_HALO - (CONV_WIDTH - 1), 0), (0, 0)))
    y, newc = _conv_layer(x, past32, wc, tm=tm)
    return y, newc[:, CONV_HALO - (CONV_WIDTH - 1):]


def kernel(x_prompt, x_sample, cache_mla_latent, cache_mla_krope, state_s5_re, state_s5_im, state_conv, norm_ab, w_in_ab, g_q_lat, w_uq, g_kv_lat, w_uk, w_uv, g_q_nope, g_q_rope, g_k_nope, g_k_rope, s5_lam_re, s5_lam_im, s5_log_dt, s5_b_re, s5_b_im, s5_c_re, s5_c_im, s5_d, s5_w_glu, s5_b_glu, w_out_ab, norm_c, w_in_c, conv_w, conv_b, ln_g, ln_b, w_out_c):
    t_p = x_prompt.shape[1]
    t_s = x_sample.shape[1]
    past_len = cache_mla_latent.shape[2]
    pos_p = jnp.arange(t_p, dtype=jnp.int32)
    pos_s = past_len + jnp.arange(t_s, dtype=jnp.int32)

    i = 0
    wa = _prep_ab_weights(norm_ab[i], w_in_ab[i], g_q_lat[i], w_uq[i], g_kv_lat[i], w_uk[i], w_uv[i],
                          g_q_nope[i], g_q_rope[i], g_k_nope[i], g_k_rope[i], w_out_ab[i])
    ws = _prep_s5_weights(s5_lam_re[i], s5_lam_im[i], s5_log_dt[i], s5_b_re[i], s5_b_im[i], s5_c_re[i],
                          s5_c_im[i], s5_d[i], s5_w_glu[i], s5_b_glu[i], max_tile=256)
    yp, lat_p, kr_p, re_p, im_p = _ab_layer(
        x_prompt, pos_p, None, wa, ws, tm_in=256, tm_s5=256, tm_out=512, tq=512, tk=512)
    ys, lat_s, kr_s, re_s, im_s = _ab_layer(
        x_sample, pos_s, (cache_mla_latent[i], cache_mla_krope[i], state_s5_re[i], state_s5_im[i]), wa, ws,
        tm_in=256, tm_s5=t_s, tm_out=512, tq=512, tk=1024)

    row = lambda a: a.reshape(1, -1)
    cw = jnp.pad(conv_w[i][::-1], ((0, 1), (0, 0)))
    wc = dict(norm_g=row(norm_c[i]), w_in=w_in_c[i].astype(BF16), conv_w=cw, conv_b=row(conv_b[i]),
              ln_g=row(ln_g[i]), ln_b=row(ln_b[i]), w_out=w_out_c[i].astype(BF16))
    yp, conv_p = _conv(yp, None, wc, tm=256)
    ys, conv_s = _conv(ys, state_conv[i], wc, tm=t_s)

    st = lambda a: a[None]
    return (yp, ys, st(lat_p), st(kr_p), st(re_p), st(im_p), st(conv_p),
            st(lat_s), st(kr_s), st(re_s), st(im_s), st(conv_s))
```

```python
import functools
import math

import numpy as np
import jax
import jax.numpy as jnp
from jax import lax
from jax.experimental import pallas as pl
from jax.experimental.pallas import tpu as pltpu

F32 = jnp.float32
BF16 = jnp.bfloat16

D_MODEL = 1024
CHUNK = 64
MLA_HEADS = 8
Q_LORA = 384
KV_LORA = 256
NOPE_DIM = 64
ROPE_DIM = 32
V_DIM = 64
MLA_WIDTH = MLA_HEADS * V_DIM
ROPE_BASE = 10000.0
ATTN_SCALE = (NOPE_DIM + ROPE_DIM) ** -0.5
S5_WIDTH = 512
S5_GROUP = 16
S5_GROUPS = S5_WIDTH // S5_GROUP
S5_STATE = 64
CONV_WIDTH = 31
EPS = 1e-6
NEG_INF = -1e30
LOG2E = math.log2(math.e)

LANES = 128
SUBLANES = 8
HEAD_PAD = MLA_HEADS * LANES
S5_HALF_GROUPS = S5_GROUPS // 2
S5_HALF_STATE = S5_HALF_GROUPS * S5_STATE
S5_SUB = 32
CONV_HALO = 32
V7X_VMEM_BYTES = 64 * 1024 * 1024


def _vmem_limit(block_bytes, scratch_bytes=0, temp_bytes=0):
    est = 2 * block_bytes + scratch_bytes + temp_bytes
    return int(min(max(est, 16 * 1024 * 1024), V7X_VMEM_BYTES - 8 * 1024 * 1024))


def _nbytes(shape, dtype):
    return int(np.prod(shape)) * jnp.dtype(dtype).itemsize


def _rms(x, g):
    return x * lax.rsqrt(jnp.mean(x * x, axis=-1, keepdims=True) + EPS) * g


def _silu(x):
    return x * jax.nn.sigmoid(x)


def _dot(a, b):
    return jnp.dot(a, b, preferred_element_type=F32)


def _dot_nt(a, b):
    return lax.dot_general(a, b, (((1,), (1,)), ((), ())), preferred_element_type=F32)


def _ab_in_kernel(x_ref, ng_ref, win_ref, gql_ref, wuq_ref, gkv_ref, wuk_ref, wuv_ref, e2_ref,
                  cq_ref, sq_ref, ck_ref, sk_ref, gq_ref, gqs_ref, gk_ref, gks_ref, gkn_ref,
                  lat_ref, kr_ref, qp_ref, kp_ref, vt_ref, gm_ref, u_ref, gs_ref):
    x = x_ref[...]
    h = _rms(x, ng_ref[...]).astype(BF16)

    def proj(lo, hi):
        return _dot(h, win_ref[:, lo:hi])

    q_lat = proj(0, Q_LORA)
    c_kv = proj(Q_LORA, Q_LORA + KV_LORA)
    krg = proj(640, 768)
    gm_ref[...] = proj(768, 1280)
    u_ref[...] = proj(1280, 1792)
    gs_ref[...] = proj(1792, 2304)

    c_n = _rms(c_kv, gkv_ref[...])
    lat_ref[...] = c_n
    cb = c_n.astype(BF16)

    lane = lax.broadcasted_iota(jnp.int32, (1, LANES), 1)
    ms = jnp.sum(jnp.where(lane < ROPE_DIM, krg * krg, 0.0), axis=-1, keepdims=True) * (1.0 / ROPE_DIM)
    kr = lax.rsqrt(ms + EPS) * (krg * (ck_ref[...] * gk_ref[...])
                                + pltpu.roll(krg, LANES - ROPE_DIM, 1) * (sk_ref[...] * gks_ref[...]))
    kr_ref[...] = kr[:, :ROPE_DIM]
    kr_at64 = pltpu.roll(kr, NOPE_DIM, 1)

    qn = _rms(q_lat, gql_ref[...]).astype(BF16)
    qa_tab = cq_ref[...] * (gq_ref[...] * (ATTN_SCALE * LOG2E))
    qb_tab = sq_ref[...] * (gqs_ref[...] * (ATTN_SCALE * LOG2E))
    n_pairs = MLA_HEADS // 2
    pair_cols = [slice(2 * LANES * p, 2 * LANES * (p + 1)) for p in range(n_pairs)]
    qas = [_dot(qn, wuq_ref[:, cols]) for cols in pair_cols]
    kas = [_dot(cb, wuk_ref[:, cols]) for cols in pair_cols]
    vt_ref[...] = _dot_nt(wuv_ref[...], cb).astype(BF16)
    q_ms = [_dot((qa * qa).astype(BF16), e2_ref[...]) for qa in qas]
    k_ms = [_dot((ka * ka).astype(BF16), e2_ref[...]) for ka in kas]
    for p in range(n_pairs):
        lo = 2 * LANES * p
        qs = qas[p] * lax.rsqrt(q_ms[p] + EPS)
        ks = kas[p] * lax.rsqrt(k_ms[p] + EPS)
        for j in range(2):
            sl = slice(LANES * j, LANES * (j + 1))
            s = qs[:, sl]
            qp_ref[:, lo + LANES * j:lo + LANES * (j + 1)] = (
                s * qa_tab + pltpu.roll(s, LANES - ROPE_DIM, 1) * qb_tab).astype(BF16)
            kp_ref[:, lo + LANES * j:lo + LANES * (j + 1)] = (
                ks[:, sl] * gkn_ref[...] + kr_at64).astype(BF16)


def _ab_in(x2d, pos_tabs, w, *, seq_len, tm):
    n_tok = x2d.shape[0]
    if tm > seq_len:
        pos_tabs = tuple(jnp.tile(p, (tm // seq_len, 1)) for p in pos_tabs)
    n_pos = max(seq_len // tm, 1)
    tok = lambda i: (i, 0)
    pos = lambda i: (i % n_pos, 0)
    const = lambda i: (0, 0)

    def full(a):
        return pl.BlockSpec(a.shape, const)

    cq, sq, ck, sk = pos_tabs
    ins = [x2d, w['norm_g'], w['w_in'], w['g_q_lat'], w['w_uq'], w['g_kv'], w['w_uk'], w['w_uvt'], w['e2'],
           cq, sq, ck, sk, w['gq'], w['gqs'], w['gk'], w['gks'], w['gkn']]
    in_specs = [pl.BlockSpec((tm, D_MODEL), tok)] + [full(a) for a in ins[1:9]] \
        + [pl.BlockSpec((tm, LANES), pos)] * 4 + [full(a) for a in ins[13:]]
    outs = [((n_tok, KV_LORA), F32), ((n_tok, ROPE_DIM), F32), ((n_tok, HEAD_PAD), BF16),
            ((n_tok, HEAD_PAD), BF16), ((MLA_WIDTH, n_tok), BF16), ((n_tok, MLA_WIDTH), F32),
            ((n_tok, S5_WIDTH), F32), ((n_tok, S5_WIDTH), F32)]
    vt_index = 4
    out_specs = [pl.BlockSpec((MLA_WIDTH, tm), lambda i: (0, i)) if n == vt_index
                 else pl.BlockSpec((tm, s[1]), tok) for n, (s, _) in enumerate(outs)]
    out_shape = [jax.ShapeDtypeStruct(s, d) for s, d in outs]
    blk = sum(_nbytes(a.shape, a.dtype) for a in ins[1:9]) + _nbytes((tm, D_MODEL), F32) \
        + 4 * _nbytes((tm, LANES), F32) + sum(_nbytes(s, d) for s, d in outs) * tm // n_tok
    return pl.pallas_call(
        _ab_in_kernel, grid=(n_tok // tm,), in_specs=in_specs, out_specs=out_specs, out_shape=out_shape,
        compiler_params=pltpu.CompilerParams(
            dimension_semantics=("arbitrary",),
            vmem_limit_bytes=_vmem_limit(blk, temp_bytes=8 * _nbytes((tm, D_MODEL), F32))),
        name="ab_in")(*ins)


def _ones_row(n):
    return (lax.broadcasted_iota(jnp.int32, (V_DIM, n), 0) == 0).astype(BF16)


def _attn_prompt_kernel(qi_ref, kj_ref, q_ref, k_ref, vt_ref, g_ref, o_ref, vaug_sc, s_sc, acc_sc, *, tq,
                        seq_len):
    n_steps = qi_ref.shape[0]
    ones_row = _ones_row(seq_len)
    for hh in range(2):
        vaug_sc[hh, 0:V_DIM, :] = vt_ref[hh * V_DIM:(hh + 1) * V_DIM, :]
        vaug_sc[hh, V_DIM:2 * V_DIM, :] = ones_row
    key_chunk = lax.broadcasted_iota(jnp.int32, (tq, tq), 0) // CHUNK
    qry_chunk = lax.broadcasted_iota(jnp.int32, (tq, tq), 1) // CHUNK
    diag_mask = key_chunk <= qry_chunk
    lane = lax.broadcasted_iota(jnp.int32, (1, LANES), 1)

    def produce(n, slot):
        qi = qi_ref[n]
        kj = kj_ref[n]
        r0 = pl.multiple_of(qi * tq, tq)
        k0 = pl.multiple_of(kj * tq, tq)
        keep = jnp.logical_or(diag_mask, kj < qi)
        bms = []
        for hh in range(2):
            hs = slice(hh * LANES, (hh + 1) * LANES)
            s = _dot_nt(k_ref[0, pl.ds(k0, tq), hs], q_ref[0, pl.ds(r0, tq), hs])
            s = jnp.where(keep, s, NEG_INF)
            s_sc[slot, hh] = s
            bms.append(jnp.max(s, axis=0, keepdims=True))
        return tuple(bms)

    def consume(n, slot, bms, state):
        qi = qi_ref[n]
        kj = kj_ref[n]
        k0 = pl.multiple_of(kj * tq, tq)
        out = []
        for hh in range(2):
            m, acc = state[hh]
            m = jnp.where(kj == 0, -jnp.inf, m)
            m_new = jnp.maximum(m, bms[hh])
            p = jnp.exp2(s_sc[slot, hh] - m_new)
            alpha = jnp.exp2(m - m_new)
            acc = alpha * acc + _dot(vaug_sc[hh, :, pl.ds(k0, tq)], p.astype(BF16))
            acc_sc[qi, hh] = acc
            out.append((m_new, acc))
        return tuple(out)

    def double_step(t, carry):
        bms_a, state = carry
        bms_b = produce(2 * t + 1, 1)
        state = consume(2 * t, 0, bms_a, state)
        bms_a = produce(2 * t + 2, 0)
        state = consume(2 * t + 1, 1, bms_b, state)
        return bms_a, state

    state = tuple((jnp.full((1, tq), -jnp.inf, F32), jnp.zeros((2 * V_DIM, tq), F32)) for _ in range(2))
    n_double = (n_steps - 1) // 2
    bms_a, state = lax.fori_loop(0, n_double, double_step, (produce(0, 0), state))
    if n_steps - 1 == 2 * n_double:
        consume(n_steps - 1, 0, bms_a, state)
    else:
        bms_b = produce(n_steps - 1, 1)
        state = consume(n_steps - 2, 0, bms_a, state)
        consume(n_steps - 1, 1, bms_b, state)

    def finalize(qi, carry):
        r0 = pl.multiple_of(qi * tq, tq)
        outs = []
        for hh in range(2):
            acc_t = acc_sc[qi, hh].T
            outs.append(acc_t * (1.0 / acc_t[:, V_DIM:V_DIM + 1]))
        o = jnp.where(lane < V_DIM, outs[0], pltpu.roll(outs[1], V_DIM, 1))
        o_ref[0, pl.ds(r0, tq), :] = (o * _silu(g_ref[0, pl.ds(r0, tq), :])).astype(BF16)
        return carry

    lax.fori_loop(0, seq_len // tq, finalize, 0)


def _attn_prompt(qp, kp, vt, gate, *, tq):
    b, t, _ = qp.shape
    nq = t // tq
    steps = [(qi, kj) for qi in range(nq) for kj in range(qi + 1)]
    qi_tab = jnp.asarray([s[0] for s in steps], jnp.int32)
    kj_tab = jnp.asarray([s[1] for s in steps], jnp.int32)
    pair = lambda i, j, qt, kt: (i, 0, j)
    blk = 2 * _nbytes((t, 2 * LANES), BF16) + _nbytes((LANES, t), BF16) + _nbytes((t, LANES), F32) \
        + _nbytes((t, LANES), BF16)
    scratch = _nbytes((2, LANES, t), BF16) + _nbytes((2, 2, tq, tq), F32) + _nbytes((nq, 2, LANES, tq), F32)
    return pl.pallas_call(
        functools.partial(_attn_prompt_kernel, tq=tq, seq_len=t),
        grid_spec=pltpu.PrefetchScalarGridSpec(
            num_scalar_prefetch=2,
            grid=(b, MLA_HEADS // 2),
            in_specs=[pl.BlockSpec((1, t, 2 * LANES), pair), pl.BlockSpec((1, t, 2 * LANES), pair),
                      pl.BlockSpec((LANES, t), lambda i, j, qt, kt: (j, i)), pl.BlockSpec((1, t, LANES), pair)],
            out_specs=pl.BlockSpec((1, t, LANES), pair),
            scratch_shapes=[pltpu.VMEM((2, LANES, t), BF16), pltpu.VMEM((2, 2, tq, tq), F32),
                            pltpu.VMEM((nq, 2, LANES, tq), F32)]),
        out_shape=jax.ShapeDtypeStruct((b, t, MLA_WIDTH), BF16),
        compiler_params=pltpu.CompilerParams(
            dimension_semantics=("arbitrary", "arbitrary"),
            vmem_limit_bytes=_vmem_limit(blk, scratch, temp_bytes=12 * _nbytes((tq, tq), F32))),
        name="attn_prompt")(qi_tab, kj_tab, qp, kp, vt, gate)


def _attn_sample_kernel(q_ref, lat_ref, kr_ref, kpn_ref, vtn_ref, g_ref, wuk_ref, wuvt_ref, e2_ref, gkn2_ref,
                        place_ref, o_ref, qbd_sc, m_sc, acc_sc, *, n_past_blocks):
    j = pl.program_id(1)
    n_pairs = MLA_HEADS // 2
    t = q_ref.shape[1]
    lane = lax.broadcasted_iota(jnp.int32, (1, LANES), 1)

    @pl.when(j == 0)
    def _():
        m_sc[...] = jnp.full(m_sc.shape, -jnp.inf, F32)
        acc_sc[...] = jnp.zeros(acc_sc.shape, F32)
        q = q_ref[0].astype(F32)
        q_t = jnp.concatenate([q, jnp.zeros((LANES - t, HEAD_PAD), F32)], axis=0).T
        for p in range(n_pairs):
            top = q_t[2 * LANES * p:2 * LANES * p + LANES, :]
            bot = pltpu.roll(q_t[2 * LANES * p + LANES:2 * LANES * (p + 1), :], t, 1)
            qbd_sc[p] = jnp.concatenate([top, bot], axis=0).astype(BF16)

    def update(p, kp_pair, vt_pair):
        n = kp_pair.shape[0]
        s = _dot(kp_pair, qbd_sc[p])
        m_old = m_sc[p]
        m_new = jnp.maximum(m_old, jnp.max(s, axis=0, keepdims=True))
        pr = jnp.exp2(s - m_new)
        alpha = jnp.exp2(m_old - m_new)
        ones_row = _ones_row(n)
        vaug = jnp.concatenate([vt_pair[0:V_DIM, :], ones_row, vt_pair[V_DIM:2 * V_DIM, :], ones_row], axis=0)
        acc_sc[p] = alpha * acc_sc[p] + _dot(vaug, pr.astype(BF16))
        m_sc[p] = m_new

    @pl.when(j < n_past_blocks)
    def _():
        latb = lat_ref[0].astype(BF16)
        kr128 = _dot(kr_ref[0].astype(BF16), place_ref[...])
        kr256 = jnp.concatenate([kr128, kr128], axis=1)
        vt_all = _dot_nt(wuvt_ref[...], latb).astype(BF16)
        kas = [_dot(latb, wuk_ref[:, 2 * LANES * p:2 * LANES * (p + 1)]) for p in range(n_pairs)]
        mss = [_dot((ka * ka).astype(BF16), e2_ref[...]) for ka in kas]
        kps = [(ka * lax.rsqrt(ms + EPS) * gkn2_ref[...] + kr256).astype(BF16) for ka, ms in zip(kas, mss)]
        for p in range(n_pairs):
            update(p, kps[p], vt_all[LANES * p:LANES * (p + 1), :])

    @pl.when(j == n_past_blocks)
    def _():
        for p in range(n_pairs):
            update(p, kpn_ref[0, :, 2 * LANES * p:2 * LANES * (p + 1)], vtn_ref[0, LANES * p:LANES * (p + 1), :])
        for p in range(n_pairs):
            acc_t = acc_sc[p].T
            a0 = acc_t[0:t, 0:LANES]
            a1 = acc_t[t:2 * t, LANES:2 * LANES]
            o0 = a0 * (1.0 / a0[:, V_DIM:V_DIM + 1])
            o1 = a1 * (1.0 / a1[:, V_DIM:V_DIM + 1])
            o = jnp.where(lane < V_DIM, o0, pltpu.roll(o1, V_DIM, 1))
            g = g_ref[0, :, p * LANES:(p + 1) * LANES]
            o_ref[0, :, p * LANES:(p + 1) * LANES] = (o * _silu(g)).astype(BF16)


def _attn_sample(qp, past_lat, past_kr, kp_new, vt_new, gate, w, *, tk):
    b, t, _ = qp.shape
    assert 2 * t == LANES, "two heads' queries share one 128-lane group"
    n_past_blocks = past_lat.shape[1] // tk
    n_pairs = MLA_HEADS // 2
    cur = lambda i, j: (i, 0, 0)
    past = lambda i, j: (i, jnp.minimum(j, n_past_blocks - 1), 0)
    const = lambda i, j: (0, 0)
    consts = [w['w_uk'], w['w_uvt'], w['e2'], w['gkn2'], w['place']]
    blk = _nbytes((tk, KV_LORA), F32) + _nbytes((tk, LANES), F32) + 2 * _nbytes((t, HEAD_PAD), BF16) \
        + _nbytes((MLA_WIDTH, LANES), BF16) + 2 * _nbytes((t, MLA_WIDTH), F32) \
        + sum(_nbytes(a.shape, a.dtype) for a in consts)
    scratch = _nbytes((n_pairs, 2 * LANES, LANES), BF16) + _nbytes((n_pairs, SUBLANES, LANES), F32) \
        + _nbytes((n_pairs, 2 * LANES, LANES), F32)
    return pl.pallas_call(
        functools.partial(_attn_sample_kernel, n_past_blocks=n_past_blocks),
        grid=(b, n_past_blocks + 1),
        in_specs=[pl.BlockSpec((1, t, HEAD_PAD), cur),
                  pl.BlockSpec((1, tk, KV_LORA), past),
                  pl.BlockSpec((1, tk, ROPE_DIM), past),
                  pl.BlockSpec((1, t, HEAD_PAD), cur),
                  pl.BlockSpec((1, MLA_WIDTH, t), cur),
                  pl.BlockSpec((1, t, MLA_WIDTH), cur)] + [pl.BlockSpec(a.shape, const) for a in consts],
        out_specs=pl.BlockSpec((1, t, MLA_WIDTH), cur),
        out_shape=jax.ShapeDtypeStruct((b, t, MLA_WIDTH), BF16),
        scratch_shapes=[pltpu.VMEM((n_pairs, 2 * LANES, LANES), BF16),
                        pltpu.VMEM((n_pairs, 1, LANES), F32),
                        pltpu.VMEM((n_pairs, 2 * LANES, LANES), F32)],
        compiler_params=pltpu.CompilerParams(
            dimension_semantics=("arbitrary", "arbitrary"),
            vmem_limit_bytes=_vmem_limit(blk, scratch, temp_bytes=12 * _nbytes((tk, 2 * LANES), F32))),
        name="attn_sample")(qp, past_lat, past_kr, kp_new, vt_new, gate, *consts)


def _s5_prep_kernel(lr_ref, li_ref, ldt_ref, br_ref, bi_ref, pos_ref, neg_ref, lam1_ref, bbar_ref, *, sub):
    lr = lr_ref[...]
    li = li_ref[...]
    dt = jnp.exp(ldt_ref[...])
    ar = lr * dt
    ai = li * dt
    k = lax.broadcasted_iota(jnp.int32, (sub, 1), 0).astype(F32)
    mag_p = jnp.exp(k * ar)
    mag_n = jnp.exp(-k * ar)
    ang = k * ai
    c = jnp.cos(ang)
    s = jnp.sin(ang)
    pos_ref[0] = mag_p * c
    pos_ref[1] = mag_p * s
    neg_ref[0] = mag_n * c
    neg_ref[1] = -mag_n * s
    mag1 = jnp.exp(ar)
    l1r = mag1 * jnp.cos(ai)
    l1i = mag1 * jnp.sin(ai)
    lam1_ref[0:1, :] = l1r
    lam1_ref[1:2, :] = l1i
    inv = 1.0 / (lr * lr + li * li)
    nr = l1r - 1.0
    cr = (nr * lr + l1i * li) * inv
    ci = (l1i * lr - nr * li) * inv
    br = br_ref[...]
    bi = bi_ref[...]
    bbar_ref[0] = cr * br - ci * bi
    bbar_ref[1] = cr * bi + ci * br


def _s5_prep(lam_re, lam_im, log_dt, b_re, b_im, *, sub):
    gp = S5_GROUPS * S5_STATE
    lr = lam_re.reshape(1, gp)
    li = lam_im.reshape(1, gp)
    ldt = jnp.broadcast_to(log_dt[:, None], (S5_GROUPS, S5_STATE)).reshape(1, gp)
    br = jnp.transpose(b_re, (2, 0, 1)).reshape(S5_GROUP, gp)
    bi = jnp.transpose(b_im, (2, 0, 1)).reshape(S5_GROUP, gp)
    return pl.pallas_call(
        functools.partial(_s5_prep_kernel, sub=sub),
        out_shape=[jax.ShapeDtypeStruct((2, sub, gp), F32), jax.ShapeDtypeStruct((2, sub, gp), F32),
                   jax.ShapeDtypeStruct((2, gp), F32), jax.ShapeDtypeStruct((2, S5_GROUP, gp), F32)],
        name="s5_prep")(lr, li, ldt, br, bi)


def _s5_kernel(u_ref, gs_ref, s0_ref, bblk_ref, cblk_ref, tri_ref, neg_ref, pos_ref, lam1_ref, d_ref,
               wglu_ref, bglu_ref, o_ref, sfin_ref, carry_sc, *, tm, sub):
    ti = pl.program_id(1)
    tile = 2 * LANES
    tiles_per_half = 2 * S5_HALF_STATE // tile

    @pl.when(ti == 0)
    def _():
        carry_sc[...] = s0_ref[0]

    u = u_ref[0]
    ub = u.astype(BF16)
    n_tiles = 2 * tiles_per_half

    def lanes_of(t):
        lo = t * tile
        return slice(lo, lo + LANES), slice(lo + LANES, lo + tile)

    def stage_a(t):
        hf, q = divmod(t, tiles_per_half)
        re, im = lanes_of(t)
        bu = _dot(ub[:, hf * 256:(hf + 1) * 256], bblk_ref[hf, :, q * tile:(q + 1) * tile])
        nre, nim = neg_ref[:, re], neg_ref[:, im]
        xs = []
        for c in range(tm // sub):
            rs = slice(c * sub, (c + 1) * sub)
            bre, bim = bu[rs, :LANES], bu[rs, LANES:]
            xs.append(jnp.concatenate([nre * bre - nim * bim, nre * bim + nim * bre], axis=-1).astype(BF16))
        return jnp.concatenate(xs, axis=0)

    def stage_b(t, x):
        re, im = lanes_of(t)
        cs = _dot(tri_ref[...], x)
        cre, cim = carry_sc[:, re], carry_sc[:, im]
        l1r, l1i = lam1_ref[:, re], lam1_ref[:, im]
        pre, pim = pos_ref[:, re], pos_ref[:, im]
        ss = []
        for c in range(tm // sub):
            rs = slice(c * sub, (c + 1) * sub)
            tre = cs[rs, :LANES] + (l1r * cre - l1i * cim)
            tim = cs[rs, LANES:] + (l1r * cim + l1i * cre)
            sre = pre * tre - pim * tim
            sim = pre * tim + pim * tre
            ss.append(jnp.concatenate([sre, sim], axis=-1).astype(BF16))
            cre = sre[sub - 1:sub, :]
            cim = sim[sub - 1:sub, :]
        carry_sc[:, re] = cre
        carry_sc[:, im] = cim
        return jnp.concatenate(ss, axis=0)

    def stage_c(t, s):
        hf, q = divmod(t, tiles_per_half)
        return _dot(s, cblk_ref[hf, q * tile:(q + 1) * tile, :])

    xs_, ss_ = {}, {}
    accs = [None, None]
    skew = 1
    for step in range(n_tiles + 2 * skew):
        if step < n_tiles:
            xs_[step] = stage_a(step)
        if skew <= step < n_tiles + skew:
            ss_[step - skew] = stage_b(step - skew, xs_.pop(step - skew))
        if step >= 2 * skew:
            t = step - 2 * skew
            part = stage_c(t, ss_.pop(t))
            hf = t // tiles_per_half
            accs[hf] = part if accs[hf] is None else accs[hf] + part
    y = jnp.concatenate(accs, axis=-1) + d_ref[...] * u
    z = jax.nn.gelu(y)
    z = z * jax.nn.sigmoid(_dot(z.astype(BF16), wglu_ref[...]) + bglu_ref[...])
    o_ref[0] = (z * _silu(gs_ref[0])).astype(BF16)

    @pl.when(ti == pl.num_programs(1) - 1)
    def _():
        sfin_ref[0] = carry_sc[...]


def _s5(u, gate, s0, w, *, tm):
    b, t, _ = u.shape
    sub = S5_SUB
    tokb = lambda i, j: (i, j, 0)
    perb = lambda i, j: (i, 0, 0)
    c2 = lambda i, j: (0, 0)
    c3 = lambda i, j: (0, 0, 0)
    tri = w['tri'][:tm, :tm]
    consts = [w['bblk'], w['cblk'], tri, w['neg'], w['pos'], w['lam1'], w['d'], w['w_glu'], w['b_glu']]
    blk = 2 * _nbytes((tm, S5_WIDTH), F32) + _nbytes((tm, S5_WIDTH), BF16) + 2 * _nbytes((1, 4 * S5_HALF_STATE), F32) \
        + sum(_nbytes(a.shape, a.dtype) for a in consts)
    scratch = _nbytes((SUBLANES, 4 * S5_HALF_STATE), F32)
    return pl.pallas_call(
        functools.partial(_s5_kernel, tm=tm, sub=sub),
        grid=(b, t // tm),
        in_specs=[pl.BlockSpec((1, tm, S5_WIDTH), tokb), pl.BlockSpec((1, tm, S5_WIDTH), tokb),
                  pl.BlockSpec((1, 1, 4 * S5_HALF_STATE), perb)]
                 + [pl.BlockSpec(a.shape, c3 if a.ndim == 3 else c2) for a in consts],
        out_specs=[pl.BlockSpec((1, tm, S5_WIDTH), tokb), pl.BlockSpec((1, 1, 4 * S5_HALF_STATE), perb)],
        out_shape=[jax.ShapeDtypeStruct((b, t, S5_WIDTH), BF16),
                   jax.ShapeDtypeStruct((b, 1, 4 * S5_HALF_STATE), F32)],
        scratch_shapes=[pltpu.VMEM((1, 4 * S5_HALF_STATE), F32)],
        compiler_params=pltpu.CompilerParams(
            dimension_semantics=("arbitrary", "arbitrary"),
            vmem_limit_bytes=_vmem_limit(blk, scratch, temp_bytes=16 * _nbytes((tm, 2 * LANES), F32))),
        name="s5_scan")(u, gate, s0, *consts)


def _ab_out_kernel(a_ref, s_ref, x_ref, wa_ref, ws_ref, o_ref):
    o_ref[...] = _dot(a_ref[...], wa_ref[...]) + _dot(s_ref[...], ws_ref[...]) + x_ref[...]


def _ab_out(mla, s5, x2d, w, *, tm):
    n_tok = x2d.shape[0]
    tok = lambda i: (i, 0)
    const = lambda i: (0, 0)
    blk = 2 * _nbytes((tm, MLA_WIDTH), BF16) + 2 * _nbytes((tm, D_MODEL), F32) \
        + 2 * _nbytes((MLA_WIDTH, D_MODEL), BF16)
    return pl.pallas_call(
        _ab_out_kernel, grid=(n_tok // tm,),
        in_specs=[pl.BlockSpec((tm, MLA_WIDTH), tok), pl.BlockSpec((tm, S5_WIDTH), tok),
                  pl.BlockSpec((tm, D_MODEL), tok),
                  pl.BlockSpec((MLA_WIDTH, D_MODEL), const), pl.BlockSpec((S5_WIDTH, D_MODEL), const)],
        out_specs=pl.BlockSpec((tm, D_MODEL), tok),
        out_shape=jax.ShapeDtypeStruct((n_tok, D_MODEL), F32),
        compiler_params=pltpu.CompilerParams(
            dimension_semantics=("arbitrary",),
            vmem_limit_bytes=_vmem_limit(blk, temp_bytes=2 * _nbytes((tm, D_MODEL), F32))),
        name="ab_out")(mla, s5, x2d, w['w_out_a'], w['w_out_s'])


def _conv_kernel(x_ref, xn_ref, past_ref, ng_ref, win_ref, cw_ref, cb_ref, lng_ref, lnb_ref, wout_ref,
                 y_ref, newc_ref, vext_sc, stage_sc, hstage_sc, yc_sc, *, tm):
    ti = pl.program_id(1)

    def stage_tile(x):
        h = _rms(x, ng_ref[...]).astype(BF16)
        hstage_sc[...] = h
        stage_sc[...] = _dot(h, win_ref[:, 0:D_MODEL]) * jax.nn.sigmoid(_dot(h, win_ref[:, D_MODEL:2 * D_MODEL]))

    @pl.when(ti == 0)
    def _():
        vext_sc[0:CONV_HALO, :] = past_ref[0]
        stage_tile(x_ref[0])

    h = hstage_sc[...]
    vext_sc[CONV_HALO:CONV_HALO + tm, :] = stage_sc[...]
    stage_tile(xn_ref[0])

    rows = 64
    width = 2 * LANES
    for rb in range(tm // rows):
        r0 = rb * rows
        for lb in range(D_MODEL // width):
            ls = slice(lb * width, (lb + 1) * width)
            acc = None
            for r in range(SUBLANES):
                z = None
                for a in range(CONV_HALO // SUBLANES):
                    lag = SUBLANES * a + r
                    start = r0 + (CONV_HALO - SUBLANES - SUBLANES * a)
                    term = cw_ref[lag:lag + 1, ls] * vext_sc[start:start + rows + SUBLANES, ls]
                    z = term if z is None else z + term
                part = (z if r == 0 else pltpu.roll(z, r, 0))[SUBLANES:SUBLANES + rows, :]
                acc = part if acc is None else acc + part
            yc_sc[r0:r0 + rows, ls] = acc

    x = x_ref[0]
    yc = yc_sc[...] + cb_ref[...]
    mu = jnp.mean(yc, axis=-1, keepdims=True)
    xc = yc - mu
    var = jnp.mean(xc * xc, axis=-1, keepdims=True)
    yn = xc * lax.rsqrt(var + EPS) * lng_ref[...] + lnb_ref[...]
    gate = _dot(h, win_ref[:, 2 * D_MODEL:3 * D_MODEL])
    mixed = (_silu(yn) * _silu(gate)).astype(BF16)
    y_ref[0] = _dot(mixed, wout_ref[...]) + x

    tail = vext_sc[tm:tm + CONV_HALO, :]
    vext_sc[0:CONV_HALO, :] = tail
    newc_ref[0] = tail


def _conv_layer(x, past32, w, *, tm):
    b, t, _ = x.shape
    nt = t // tm
    tokb = lambda i, j: (i, j, 0)
    nxtb = lambda i, j: (i, jnp.minimum(j + 1, nt - 1), 0)
    perb = lambda i, j: (i, 0, 0)
    const = lambda i, j: (0, 0)
    consts = [w['norm_g'], w['w_in'], w['conv_w'], w['conv_b'], w['ln_g'], w['ln_b'], w['w_out']]
    blk = 3 * _nbytes((tm, D_MODEL), F32) + 2 * _nbytes((CONV_HALO, D_MODEL), F32) \
        + sum(_nbytes(a.shape, a.dtype) for a in consts)
    scratch = _nbytes((tm + CONV_HALO, D_MODEL), F32) + 2 * _nbytes((tm, D_MODEL), F32)
    return pl.pallas_call(
        functools.partial(_conv_kernel, tm=tm),
        grid=(b, nt),
        in_specs=[pl.BlockSpec((1, tm, D_MODEL), tokb), pl.BlockSpec((1, tm, D_MODEL), nxtb),
                  pl.BlockSpec((1, CONV_HALO, D_MODEL), perb)]
                 + [pl.BlockSpec(a.shape, const) for a in consts],
        out_specs=[pl.BlockSpec((1, tm, D_MODEL), tokb), pl.BlockSpec((1, CONV_HALO, D_MODEL), perb)],
        out_shape=[jax.ShapeDtypeStruct((b, t, D_MODEL), F32),
                   jax.ShapeDtypeStruct((b, CONV_HALO, D_MODEL), F32)],
        scratch_shapes=[pltpu.VMEM((tm + CONV_HALO, D_MODEL), F32), pltpu.VMEM((tm, D_MODEL), F32),
                        pltpu.VMEM((tm, D_MODEL), BF16), pltpu.VMEM((tm, D_MODEL), F32)],
        compiler_params=pltpu.CompilerParams(
            dimension_semantics=("arbitrary", "arbitrary"),
            vmem_limit_bytes=_vmem_limit(blk, scratch, temp_bytes=8 * _nbytes((tm, D_MODEL), F32))),
        name="conv_layer")(x, x, past32, *consts)


def _head_groups(nope, rope, third):
    return jnp.concatenate([nope, rope, third], axis=-1).reshape(nope.shape[0], HEAD_PAD)


def _half_swap(a):
    half = ROPE_DIM // 2
    return jnp.concatenate([a[..., half:], a[..., :half]], axis=-1)


def _lane_pad(a, lo, width=LANES):
    pad = [(0, 0)] * (a.ndim - 1) + [(lo, width - lo - a.shape[-1])]
    return jnp.pad(a, pad)


def _prep_ab_weights(norm_g, w_in, g_q_lat, w_uq, g_kv_lat, w_uk, w_uv, g_q_nope, g_q_rope, g_k_nope, g_k_rope,
                     w_out):
    o_kr = Q_LORA + KV_LORA
    kr_cols = w_in[:, o_kr:o_kr + ROPE_DIM]
    kr_group = jnp.concatenate([kr_cols, _half_swap(kr_cols), jnp.zeros((D_MODEL, LANES - 2 * ROPE_DIM), F32)], -1)
    w_in_p = jnp.concatenate([w_in[:, :o_kr], kr_group, w_in[:, o_kr + ROPE_DIM:]], axis=-1).astype(BF16)
    uq = w_uq.reshape(Q_LORA, MLA_HEADS, NOPE_DIM + ROPE_DIM)
    uq_r = uq[..., NOPE_DIM:]
    w_uq_p = _head_groups(uq[..., :NOPE_DIM], uq_r, _half_swap(uq_r)).astype(BF16)
    zeros_kv = jnp.zeros((KV_LORA, MLA_HEADS, LANES - NOPE_DIM), F32)
    w_uk_p = jnp.concatenate([w_uk, zeros_kv], axis=-1).reshape(KV_LORA, HEAD_PAD).astype(BF16)
    w_uv_t = w_uv.reshape(KV_LORA, MLA_WIDTH).T.astype(BF16)
    r = np.arange(2 * LANES)
    same = (r[:, None] // LANES) == (r[None, :] // LANES)
    ri, ci = r[:, None] % LANES, r[None, :] % LANES
    e2 = np.where(same & (ri < NOPE_DIM) & (ci < NOPE_DIM), 1.0 / NOPE_DIM, 0.0) \
        + np.where(same & (ri >= NOPE_DIM) & (ri < NOPE_DIM + ROPE_DIM) & (ci >= NOPE_DIM), 1.0 / ROPE_DIM, 0.0)
    place = np.zeros((ROPE_DIM, LANES), np.float32)
    place[np.arange(ROPE_DIM), NOPE_DIM + np.arange(ROPE_DIM)] = 1.0
    row = lambda a: a.reshape(1, -1)
    return dict(
        norm_g=row(norm_g), w_in=w_in_p, g_q_lat=row(g_q_lat), w_uq=w_uq_p, g_kv=row(g_kv_lat),
        w_uk=w_uk_p, w_uvt=w_uv_t, e2=jnp.asarray(e2, BF16), place=jnp.asarray(place, BF16),
        gq=row(jnp.concatenate([g_q_nope, g_q_rope, jnp.zeros((ROPE_DIM,), F32)])),
        gqs=row(_lane_pad(_half_swap(g_q_rope), NOPE_DIM)),
        gk=row(_lane_pad(g_k_rope, 0)), gks=row(_lane_pad(_half_swap(g_k_rope), 0)),
        gkn=row(_lane_pad(g_k_nope, 0)), gkn2=row(jnp.tile(_lane_pad(g_k_nope, 0), 2)),
        w_out_a=w_out[:MLA_WIDTH].astype(BF16), w_out_s=w_out[MLA_WIDTH:].astype(BF16))


def _rope_tables(pos):
    half = ROPE_DIM // 2
    inv = ROPE_BASE ** (-jnp.arange(half, dtype=F32) / half)
    ang = pos.astype(F32)[:, None] * inv[None, :]
    cos = jnp.cos(ang)
    sin = jnp.sin(ang)
    cosf = jnp.concatenate([cos, cos], axis=-1)
    sinf = jnp.concatenate([-sin, sin], axis=-1)
    ones = jnp.ones((pos.shape[0], NOPE_DIM), F32)
    cq = _lane_pad(jnp.concatenate([ones, cosf], axis=-1), 0)
    sq = _lane_pad(sinf, NOPE_DIM)
    return cq, sq, _lane_pad(cosf, 0), _lane_pad(sinf, 0)


def _prep_s5_weights(lam_re, lam_im, log_dt, b_re, b_im, c_re, c_im, d_skip, w_glu, b_glu, *, max_tile):
    pos, neg, lam1, bbar = _s5_prep(lam_re, lam_im, log_dt, b_re, b_im, sub=S5_SUB)
    eye = jnp.eye(S5_HALF_GROUPS, dtype=F32)

    def b_block(bb):
        bb = bb.reshape(S5_GROUP, 2, S5_HALF_GROUPS, S5_STATE)
        return jnp.einsum('nhgp,gk->hgnkp', bb, eye).reshape(2, S5_HALF_GROUPS * S5_GROUP, S5_HALF_STATE)

    def c_block(cc):
        cc = cc.reshape(2, S5_HALF_GROUPS, S5_GROUP, S5_STATE)
        return jnp.einsum('hgnp,gk->hkpgn', cc, eye).reshape(2, S5_HALF_STATE, S5_HALF_GROUPS * S5_GROUP)

    n_t = S5_HALF_STATE // LANES
    b_re_blk, b_im_blk = b_block(bbar[0]), b_block(bbar[1])
    bblk = jnp.stack([b_re_blk.reshape(2, -1, n_t, LANES), b_im_blk.reshape(2, -1, n_t, LANES)], axis=3)
    bblk = bblk.reshape(2, S5_HALF_GROUPS * S5_GROUP, 2 * S5_HALF_STATE).astype(BF16)
    c_re_blk, c_im_blk = c_block(c_re), c_block(-c_im)
    cblk = jnp.stack([c_re_blk.reshape(2, n_t, LANES, -1), c_im_blk.reshape(2, n_t, LANES, -1)], axis=2)
    cblk = cblk.reshape(2, 2 * S5_HALF_STATE, S5_HALF_GROUPS * S5_GROUP).astype(BF16)

    def tiled(tab):
        rows = tab.shape[1]
        return jnp.transpose(tab.reshape(2, rows, 2 * n_t, LANES), (1, 2, 0, 3)).reshape(rows, 4 * S5_HALF_STATE)

    r = np.arange(max_tile)
    tri = ((r[:, None] // S5_SUB) == (r[None, :] // S5_SUB)) & (r[None, :] <= r[:, None])
    return dict(bblk=bblk, cblk=cblk, tri=jnp.asarray(tri, BF16), neg=tiled(neg), pos=tiled(pos),
                lam1=tiled(lam1[:, None, :]),
                d=d_skip.reshape(1, -1), w_glu=w_glu.astype(BF16), b_glu=b_glu.reshape(1, -1))


def _pack_state(s_re, s_im):
    b = s_re.shape[0]
    st = jnp.stack([s_re.reshape(b, -1, LANES), s_im.reshape(b, -1, LANES)], axis=2)
    return st.reshape(b, 1, 4 * S5_HALF_STATE)


def _unpack_state(st):
    b = st.shape[0]
    st = st.reshape(b, -1, 2, LANES)
    return (st[:, :, 0].reshape(b, S5_GROUPS, S5_STATE), st[:, :, 1].reshape(b, S5_GROUPS, S5_STATE))


def _ab_layer(x, pos, past, wa, ws, *, tm_in, tm_s5, tm_out, tq, tk):
    b, t, _ = x.shape
    x2d = x.reshape(b * t, D_MODEL)
    lat, kr, qp, kp, vt, gm, u, gs = _ab_in(x2d, _rope_tables(pos), wa, seq_len=t, tm=tm_in)
    r3 = lambda a: a.reshape(b, t, a.shape[-1])
    if past is None:
        mla = _attn_prompt(r3(qp), r3(kp), vt, r3(gm), tq=tq)
        s0 = jnp.zeros((b, 1, 4 * S5_HALF_STATE), F32)
    else:
        past_lat, past_kr, past_re, past_im = past
        vt_b = jnp.transpose(vt.reshape(MLA_WIDTH, b, t), (1, 0, 2))
        mla = _attn_sample(r3(qp), past_lat, past_kr, r3(kp), vt_b, r3(gm), wa, tk=tk)
        s0 = _pack_state(past_re, past_im)
    s5, sfin = _s5(r3(u), r3(gs), s0, ws, tm=tm_s5)
    y = _ab_out(mla.reshape(b * t, MLA_WIDTH), s5.reshape(b * t, S5_WIDTH), x2d, wa, tm=tm_out)
    fin_re, fin_im = _unpack_state(sfin)
    return y.reshape(b, t, D_MODEL), r3(lat), r3(kr), fin_re, fin_im


def _conv(x, past, wc, *, tm):
    b = x.shape[0]
    if past is None:
        past32 = jnp.zeros((b, CONV_HALO, D_MODEL), F32)
    else:
        past32 = jnp.pad(past, ((0, 0), (CONV_HALO - (CONV_WIDTH - 1), 0), (0, 0)))
    y, newc = _conv_layer(x, past32, wc, tm=tm)
    return y, newc[:, CONV_HALO - (CONV_WIDTH - 1):]


def kernel(x_prompt, x_sample, cache_mla_latent, cache_mla_krope, state_s5_re, state_s5_im, state_conv, norm_ab, w_in_ab, g_q_lat, w_uq, g_kv_lat, w_uk, w_uv, g_q_nope, g_q_rope, g_k_nope, g_k_rope, s5_lam_re, s5_lam_im, s5_log_dt, s5_b_re, s5_b_im, s5_c_re, s5_c_im, s5_d, s5_w_glu, s5_b_glu, w_out_ab, norm_c, w_in_c, conv_w, conv_b, ln_g, ln_b, w_out_c):
    t_p = x_prompt.shape[1]
    t_s = x_sample.shape[1]
    past_len = cache_mla_latent.shape[2]
    pos_p = jnp.arange(t_p, dtype=jnp.int32)
    pos_s = past_len + jnp.arange(t_s, dtype=jnp.int32)

    i = 0
    wa = _prep_ab_weights(norm_ab[i], w_in_ab[i], g_q_lat[i], w_uq[i], g_kv_lat[i], w_uk[i], w_uv[i],
                          g_q_nope[i], g_q_rope[i], g_k_nope[i], g_k_rope[i], w_out_ab[i])
    ws = _prep_s5_weights(s5_lam_re[i], s5_lam_im[i], s5_log_dt[i], s5_b_re[i], s5_b_im[i], s5_c_re[i],
                          s5_c_im[i], s5_d[i], s5_w_glu[i], s5_b_glu[i], max_tile=256)
    yp, lat_p, kr_p, re_p, im_p = _ab_layer(
        x_prompt, pos_p, None, wa, ws, tm_in=256, tm_s5=256, tm_out=512, tq=512, tk=512)
    ys, lat_s, kr_s, re_s, im_s = _ab_layer(
        x_sample, pos_s, (cache_mla_latent[i], cache_mla_krope[i], state_s5_re[i], state_s5_im[i]), wa, ws,
        tm_in=256, tm_s5=t_s, tm_out=512, tq=512, tk=1024)

    row = lambda a: a.reshape(1, -1)
    cw = jnp.pad(conv_w[i][::-1], ((0, 1), (0, 0)))
    wc = dict(norm_g=row(norm_c[i]), w_in=w_in_c[i].astype(BF16), conv_w=cw, conv_b=row(conv_b[i]),
              ln_g=row(ln_g[i]), ln_b=row(ln_b[i]), w_out=w_out_c[i].astype(BF16))
    yp, conv_p = _conv(yp, None, wc, tm=256)
    ys, conv_s = _conv(ys, state_conv[i], wc, tm=t_s)

    st = lambda a: a[None]
    return (yp, ys, st(lat_p), st(kr_p), st(re_p), st(im_p), st(conv_p),
            st(lat_s), st(kr_s), st(re_s), st(im_s), st(conv_s))
```

```python
import functools
import math

import numpy as np
import jax
import jax.numpy as jnp
from jax import lax
from jax.experimental import pallas as pl
from jax.experimental.pallas import tpu as pltpu

F32 = jnp.float32
BF16 = jnp.bfloat16

D_MODEL = 1024
CHUNK = 64
MLA_HEADS = 8
Q_LORA = 384
KV_LORA = 256
NOPE_DIM = 64
ROPE_DIM = 32
V_DIM = 64
MLA_WIDTH = MLA_HEADS * V_DIM
ROPE_BASE = 10000.0
ATTN_SCALE = (NOPE_DIM + ROPE_DIM) ** -0.5
S5_WIDTH = 512
S5_GROUP = 16
S5_GROUPS = S5_WIDTH // S5_GROUP
S5_STATE = 64
CONV_WIDTH = 31
EPS = 1e-6
NEG_INF = -1e30
LOG2E = math.log2(math.e)
MASK_LANE0 = NOPE_DIM + ROPE_DIM

LANES = 128
SUBLANES = 8
HEAD_PAD = MLA_HEADS * LANES
S5_HALF_GROUPS = S5_GROUPS // 2
S5_HALF_STATE = S5_HALF_GROUPS * S5_STATE
S5_SUB = 32
CONV_HALO = 32
V7X_VMEM_BYTES = 64 * 1024 * 1024


def _vmem_limit(block_bytes, scratch_bytes=0, temp_bytes=0):
    est = 2 * block_bytes + scratch_bytes + temp_bytes
    return int(min(max(est, 16 * 1024 * 1024), V7X_VMEM_BYTES - 8 * 1024 * 1024))


def _nbytes(shape, dtype):
    return int(np.prod(shape)) * jnp.dtype(dtype).itemsize


def _rms(x, g):
    return x * lax.rsqrt(jnp.mean(x * x, axis=-1, keepdims=True) + EPS) * g


def _silu(x):
    return x * jax.nn.sigmoid(x)


def _dot(a, b):
    return jnp.dot(a, b, preferred_element_type=F32)


def _dot_nt(a, b):
    return lax.dot_general(a, b, (((1,), (1,)), ((), ())), preferred_element_type=F32)


def _ab_in_kernel(x_ref, ng_ref, win_ref, gql_ref, wuq_ref, gkv_ref, wuk_ref, wuv_ref, e2_ref,
                  cq_ref, sq_ref, ck_ref, sk_ref, qm_ref, km_ref, gq_ref, gqs_ref, gk_ref, gks_ref, gkn_ref,
                  lat_ref, kr_ref, qp_ref, kp_ref, vt_ref, gm_ref, u_ref, gs_ref):
    x = x_ref[...]
    h = _rms(x, ng_ref[...]).astype(BF16)

    def proj(lo, hi):
        return _dot(h, win_ref[:, lo:hi])

    q_lat = proj(0, Q_LORA)
    c_kv = proj(Q_LORA, Q_LORA + KV_LORA)
    krg = proj(640, 768)
    gm_ref[...] = proj(768, 1280)
    u_ref[...] = proj(1280, 1792)
    gs_ref[...] = proj(1792, 2304)

    c_n = _rms(c_kv, gkv_ref[...])
    lat_ref[...] = c_n
    cb = c_n.astype(BF16)

    lane = lax.broadcasted_iota(jnp.int32, (1, LANES), 1)
    ms = jnp.sum(jnp.where(lane < ROPE_DIM, krg * krg, 0.0), axis=-1, keepdims=True) * (1.0 / ROPE_DIM)
    kr = lax.rsqrt(ms + EPS) * (krg * (ck_ref[...] * gk_ref[...])
                                + pltpu.roll(krg, LANES - ROPE_DIM, 1) * (sk_ref[...] * gks_ref[...]))
    kr_ref[...] = kr[:, :ROPE_DIM]
    kr_mask = pltpu.roll(kr, NOPE_DIM, 1) + km_ref[...]

    qn = _rms(q_lat, gql_ref[...]).astype(BF16)
    qa_tab = cq_ref[...] * (gq_ref[...] * (ATTN_SCALE * LOG2E))
    qb_tab = sq_ref[...] * (gqs_ref[...] * (ATTN_SCALE * LOG2E))
    n_pairs = MLA_HEADS // 2
    pair_cols = [slice(2 * LANES * p, 2 * LANES * (p + 1)) for p in range(n_pairs)]
    qas = [_dot(qn, wuq_ref[:, cols]) for cols in pair_cols]
    kas = [_dot(cb, wuk_ref[:, cols]) for cols in pair_cols]
    vt_ref[...] = _dot_nt(wuv_ref[...], cb).astype(BF16)
    q_ms = [_dot((qa * qa).astype(BF16), e2_ref[...]) for qa in qas]
    k_ms = [_dot((ka * ka).astype(BF16), e2_ref[...]) for ka in kas]
    for p in range(n_pairs):
        lo = 2 * LANES * p
        qs = qas[p] * lax.rsqrt(q_ms[p] + EPS)
        ks = kas[p] * lax.rsqrt(k_ms[p] + EPS)
        for j in range(2):
            sl = slice(LANES * j, LANES * (j + 1))
            s = qs[:, sl]
            qp_ref[:, lo + LANES * j:lo + LANES * (j + 1)] = (
                s * qa_tab + pltpu.roll(s, LANES - ROPE_DIM, 1) * qb_tab + qm_ref[...]).astype(BF16)
            kp_ref[:, lo + LANES * j:lo + LANES * (j + 1)] = (
                ks[:, sl] * gkn_ref[...] + kr_mask).astype(BF16)


def _ab_in(x2d, pos_tabs, w, *, seq_len, tm):
    n_tok = x2d.shape[0]
    if tm > seq_len:
        pos_tabs = tuple(jnp.tile(p, (tm // seq_len, 1)) for p in pos_tabs)
    n_pos = max(seq_len // tm, 1)
    tok = lambda i: (i, 0)
    pos = lambda i: (i % n_pos, 0)
    const = lambda i: (0, 0)

    def full(a):
        return pl.BlockSpec(a.shape, const)

    ins = [x2d, w['norm_g'], w['w_in'], w['g_q_lat'], w['w_uq'], w['g_kv'], w['w_uk'], w['w_uvt'], w['e2'],
           *pos_tabs, w['gq'], w['gqs'], w['gk'], w['gks'], w['gkn']]
    n_tabs = len(pos_tabs)
    in_specs = [pl.BlockSpec((tm, D_MODEL), tok)] + [full(a) for a in ins[1:9]] \
        + [pl.BlockSpec((tm, LANES), pos)] * n_tabs + [full(a) for a in ins[9 + n_tabs:]]
    outs = [((n_tok, KV_LORA), F32), ((n_tok, ROPE_DIM), F32), ((n_tok, HEAD_PAD), BF16),
            ((n_tok, HEAD_PAD), BF16), ((MLA_WIDTH, n_tok), BF16), ((n_tok, MLA_WIDTH), F32),
            ((n_tok, S5_WIDTH), F32), ((n_tok, S5_WIDTH), F32)]
    vt_index = 4
    out_specs = [pl.BlockSpec((MLA_WIDTH, tm), lambda i: (0, i)) if n == vt_index
                 else pl.BlockSpec((tm, s[1]), tok) for n, (s, _) in enumerate(outs)]
    out_shape = [jax.ShapeDtypeStruct(s, d) for s, d in outs]
    blk = sum(_nbytes(a.shape, a.dtype) for a in ins[1:9]) + _nbytes((tm, D_MODEL), F32) \
        + n_tabs * _nbytes((tm, LANES), F32) + sum(_nbytes(s, d) for s, d in outs) * tm // n_tok
    return pl.pallas_call(
        _ab_in_kernel, grid=(n_tok // tm,), in_specs=in_specs, out_specs=out_specs, out_shape=out_shape,
        compiler_params=pltpu.CompilerParams(
            dimension_semantics=("arbitrary",),
            vmem_limit_bytes=_vmem_limit(blk, temp_bytes=8 * _nbytes((tm, D_MODEL), F32))),
        name="ab_in")(*ins)


def _ones_row(n):
    return (lax.broadcasted_iota(jnp.int32, (V_DIM, n), 0) == 0).astype(BF16)


def _attn_prompt_kernel(qi_ref, kj_ref, q_ref, k_ref, vt_ref, g_ref, o_ref, vaug_sc, s_sc, acc_sc, *, tq,
                        seq_len):
    n_steps = qi_ref.shape[0]
    ones_row = _ones_row(seq_len)
    for hh in range(2):
        vaug_sc[hh, 0:V_DIM, :] = vt_ref[hh * V_DIM:(hh + 1) * V_DIM, :]
        vaug_sc[hh, V_DIM:2 * V_DIM, :] = ones_row
    lane = lax.broadcasted_iota(jnp.int32, (1, LANES), 1)
    mask_lane_off = jnp.logical_or(lane < MASK_LANE0, lane >= MASK_LANE0 + tq // CHUNK)

    def produce(n, slot):
        qi = qi_ref[n]
        kj = kj_ref[n]
        r0 = pl.multiple_of(qi * tq, tq)
        k0 = pl.multiple_of(kj * tq, tq)
        q_keep = jnp.logical_or(mask_lane_off, kj == qi)
        bms = []
        for hh in range(2):
            hs = slice(hh * LANES, (hh + 1) * LANES)
            q = q_ref[0, pl.ds(r0, tq), hs]
            q = jnp.where(q_keep, q, jnp.zeros_like(q))
            s = _dot_nt(k_ref[0, pl.ds(k0, tq), hs], q)
            s_sc[slot, hh] = s
            bms.append(jnp.max(s, axis=0, keepdims=True))
        return tuple(bms)

    def consume(n, slot, bms, state):
        qi = qi_ref[n]
        kj = kj_ref[n]
        k0 = pl.multiple_of(kj * tq, tq)
        out = []
        for hh in range(2):
            m, acc = state[hh]
            m = jnp.where(kj == 0, -jnp.inf, m)
            m_new = jnp.maximum(m, bms[hh])
            p = jnp.exp2(s_sc[slot, hh] - m_new)
            alpha = jnp.exp2(m - m_new)
            acc = alpha * acc + _dot(vaug_sc[hh, :, pl.ds(k0, tq)], p.astype(BF16))
            acc_sc[qi, hh] = acc
            out.append((m_new, acc))
        return tuple(out)

    unroll = 4

    def multi_step(t, carry):
        bms, state = carry
        for u in range(unroll):
            n = unroll * t + u
            next_bms = produce(n + 1, (u + 1) % 2)
            state = consume(n, u % 2, bms, state)
            bms = next_bms
        return bms, state

    state = tuple((jnp.full((1, tq), -jnp.inf, F32), jnp.zeros((2 * V_DIM, tq), F32)) for _ in range(2))
    n_loops = (n_steps - 1) // unroll
    bms, state = lax.fori_loop(0, n_loops, multi_step, (produce(0, 0), state))
    for n in range(unroll * n_loops, n_steps):
        next_bms = produce(n + 1, (n + 1) % 2) if n + 1 < n_steps else None
        state = consume(n, n % 2, bms, state)
        bms = next_bms

    def finalize(qi, carry):
        r0 = pl.multiple_of(qi * tq, tq)
        vals = []
        for hh in range(2):
            acc = acc_sc[qi, hh]
            vals.append(acc[0:V_DIM, :] * (1.0 / acc[V_DIM:V_DIM + 1, :]))
        o = jnp.concatenate(vals, axis=0).T
        o_ref[0, pl.ds(r0, tq), :] = (o * _silu(g_ref[0, pl.ds(r0, tq), :])).astype(BF16)
        return carry

    lax.fori_loop(0, seq_len // tq, finalize, 0)


def _attn_prompt(qp, kp, vt, gate, *, tq):
    b, t, _ = qp.shape
    nq = t // tq
    steps = [(qi, kj) for qi in range(nq) for kj in range(qi + 1)]
    qi_tab = jnp.asarray([s[0] for s in steps], jnp.int32)
    kj_tab = jnp.asarray([s[1] for s in steps], jnp.int32)
    pair = lambda i, j, qt, kt: (i, 0, j)
    blk = 2 * _nbytes((t, 2 * LANES), BF16) + _nbytes((LANES, t), BF16) + _nbytes((t, LANES), F32) \
        + _nbytes((t, LANES), BF16)
    scratch = _nbytes((2, LANES, t), BF16) + _nbytes((2, 2, tq, tq), F32) + _nbytes((nq, 2, LANES, tq), F32)
    return pl.pallas_call(
        functools.partial(_attn_prompt_kernel, tq=tq, seq_len=t),
        grid_spec=pltpu.PrefetchScalarGridSpec(
            num_scalar_prefetch=2,
            grid=(b, MLA_HEADS // 2),
            in_specs=[pl.BlockSpec((1, t, 2 * LANES), pair), pl.BlockSpec((1, t, 2 * LANES), pair),
                      pl.BlockSpec((LANES, t), lambda i, j, qt, kt: (j, i)), pl.BlockSpec((1, t, LANES), pair)],
            out_specs=pl.BlockSpec((1, t, LANES), pair),
            scratch_shapes=[pltpu.VMEM((2, LANES, t), BF16), pltpu.VMEM((2, 2, tq, tq), F32),
                            pltpu.VMEM((nq, 2, LANES, tq), F32)]),
        out_shape=jax.ShapeDtypeStruct((b, t, MLA_WIDTH), BF16),
        compiler_params=pltpu.CompilerParams(
            dimension_semantics=("arbitrary", "arbitrary"),
            vmem_limit_bytes=_vmem_limit(blk, scratch, temp_bytes=12 * _nbytes((tq, tq), F32))),
        name="attn_prompt")(qi_tab, kj_tab, qp, kp, vt, gate)


def _attn_sample_kernel(q_ref, lat_ref, kr_ref, kpn_ref, vtn_ref, g_ref, wuk_ref, wuvt_ref, e2_ref, gkn2_ref,
                        place_ref, o_ref, qbd_sc, m_sc, acc_sc, *, n_past_blocks):
    j = pl.program_id(1)
    n_pairs = MLA_HEADS // 2
    t = q_ref.shape[1]
    lane = lax.broadcasted_iota(jnp.int32, (1, LANES), 1)

    @pl.when(j == 0)
    def _():
        m_sc[...] = jnp.full(m_sc.shape, -jnp.inf, F32)
        acc_sc[...] = jnp.zeros(acc_sc.shape, F32)
        q = q_ref[0].astype(F32)
        q_t = jnp.concatenate([q, jnp.zeros((LANES - t, HEAD_PAD), F32)], axis=0).T
        for p in range(n_pairs):
            top = q_t[2 * LANES * p:2 * LANES * p + LANES, :]
            bot = pltpu.roll(q_t[2 * LANES * p + LANES:2 * LANES * (p + 1), :], t, 1)
            qbd_sc[p] = jnp.concatenate([top, bot], axis=0).astype(BF16)

    def update(p, kp_pair, vt_pair):
        n = kp_pair.shape[0]
        s = _dot(kp_pair, qbd_sc[p])
        m_old = m_sc[p]
        m_new = jnp.maximum(m_old, jnp.max(s, axis=0, keepdims=True))
        pr = jnp.exp2(s - m_new)
        alpha = jnp.exp2(m_old - m_new)
        ones_row = _ones_row(n)
        vaug = jnp.concatenate([vt_pair[0:V_DIM, :], ones_row, vt_pair[V_DIM:2 * V_DIM, :], ones_row], axis=0)
        acc_sc[p] = alpha * acc_sc[p] + _dot(vaug, pr.astype(BF16))
        m_sc[p] = m_new

    @pl.when(j < n_past_blocks)
    def _():
        latb = lat_ref[0].astype(BF16)
        kr128 = _dot(kr_ref[0].astype(BF16), place_ref[...])
        kr256 = jnp.concatenate([kr128, kr128], axis=1)
        vt_all = _dot_nt(wuvt_ref[...], latb).astype(BF16)
        kas = [_dot(latb, wuk_ref[:, 2 * LANES * p:2 * LANES * (p + 1)]) for p in range(n_pairs)]
        mss = [_dot((ka * ka).astype(BF16), e2_ref[...]) for ka in kas]
        kps = [(ka * lax.rsqrt(ms + EPS) * gkn2_ref[...] + kr256).astype(BF16) for ka, ms in zip(kas, mss)]
        for p in range(n_pairs):
            update(p, kps[p], vt_all[LANES * p:LANES * (p + 1), :])

    @pl.when(j == n_past_blocks)
    def _():
        for p in range(n_pairs):
            update(p, kpn_ref[0, :, 2 * LANES * p:2 * LANES * (p + 1)], vtn_ref[0, LANES * p:LANES * (p + 1), :])
        for p in range(n_pairs):
            acc_t = acc_sc[p].T
            a0 = acc_t[0:t, 0:LANES]
            a1 = acc_t[t:2 * t, LANES:2 * LANES]
            o0 = a0 * (1.0 / a0[:, V_DIM:V_DIM + 1])
            o1 = a1 * (1.0 / a1[:, V_DIM:V_DIM + 1])
            o = jnp.where(lane < V_DIM, o0, pltpu.roll(o1, V_DIM, 1))
            g = g_ref[0, :, p * LANES:(p + 1) * LANES]
            o_ref[0, :, p * LANES:(p + 1) * LANES] = (o * _silu(g)).astype(BF16)


def _attn_sample(qp, past_lat, past_kr, kp_new, vt_new, gate, w, *, tk):
    b, t, _ = qp.shape
    assert 2 * t == LANES, "two heads' queries share one 128-lane group"
    n_past_blocks = past_lat.shape[1] // tk
    n_pairs = MLA_HEADS // 2
    cur = lambda i, j: (i, 0, 0)
    past = lambda i, j: (i, jnp.minimum(j, n_past_blocks - 1), 0)
    const = lambda i, j: (0, 0)
    consts = [w['w_uk'], w['w_uvt'], w['e2'], w['gkn2'], w['place']]
    blk = _nbytes((tk, KV_LORA), F32) + _nbytes((tk, LANES), F32) + 2 * _nbytes((t, HEAD_PAD), BF16) \
        + _nbytes((MLA_WIDTH, LANES), BF16) + 2 * _nbytes((t, MLA_WIDTH), F32) \
        + sum(_nbytes(a.shape, a.dtype) for a in consts)
    scratch = _nbytes((n_pairs, 2 * LANES, LANES), BF16) + _nbytes((n_pairs, SUBLANES, LANES), F32) \
        + _nbytes((n_pairs, 2 * LANES, LANES), F32)
    return pl.pallas_call(
        functools.partial(_attn_sample_kernel, n_past_blocks=n_past_blocks),
        grid=(b, n_past_blocks + 1),
        in_specs=[pl.BlockSpec((1, t, HEAD_PAD), cur),
                  pl.BlockSpec((1, tk, KV_LORA), past),
                  pl.BlockSpec((1, tk, ROPE_DIM), past),
                  pl.BlockSpec((1, t, HEAD_PAD), cur),
                  pl.BlockSpec((1, MLA_WIDTH, t), cur),
                  pl.BlockSpec((1, t, MLA_WIDTH), cur)] + [pl.BlockSpec(a.shape, const) for a in consts],
        out_specs=pl.BlockSpec((1, t, MLA_WIDTH), cur),
        out_shape=jax.ShapeDtypeStruct((b, t, MLA_WIDTH), BF16),
        scratch_shapes=[pltpu.VMEM((n_pairs, 2 * LANES, LANES), BF16),
                        pltpu.VMEM((n_pairs, 1, LANES), F32),
                        pltpu.VMEM((n_pairs, 2 * LANES, LANES), F32)],
        compiler_params=pltpu.CompilerParams(
            dimension_semantics=("arbitrary", "arbitrary"),
            vmem_limit_bytes=_vmem_limit(blk, scratch, temp_bytes=12 * _nbytes((tk, 2 * LANES), F32))),
        name="attn_sample")(qp, past_lat, past_kr, kp_new, vt_new, gate, *consts)


def _s5_prep_kernel(lr_ref, li_ref, ldt_ref, br_ref, bi_ref, pos_ref, neg_ref, lam1_ref, bbar_ref, *, sub):
    lr = lr_ref[...]
    li = li_ref[...]
    dt = jnp.exp(ldt_ref[...])
    ar = lr * dt
    ai = li * dt
    k = lax.broadcasted_iota(jnp.int32, (sub, 1), 0).astype(F32)
    mag_p = jnp.exp(k * ar)
    mag_n = jnp.exp(-k * ar)
    ang = k * ai
    c = jnp.cos(ang)
    s = jnp.sin(ang)
    pos_ref[0] = mag_p * c
    pos_ref[1] = mag_p * s
    neg_ref[0] = mag_n * c
    neg_ref[1] = -mag_n * s
    mag1 = jnp.exp(ar)
    l1r = mag1 * jnp.cos(ai)
    l1i = mag1 * jnp.sin(ai)
    lam1_ref[0:1, :] = l1r
    lam1_ref[1:2, :] = l1i
    inv = 1.0 / (lr * lr + li * li)
    nr = l1r - 1.0
    cr = (nr * lr + l1i * li) * inv
    ci = (l1i * lr - nr * li) * inv
    br = br_ref[...]
    bi = bi_ref[...]
    bbar_ref[0] = cr * br - ci * bi
    bbar_ref[1] = cr * bi + ci * br


def _s5_prep(lam_re, lam_im, log_dt, b_re, b_im, *, sub):
    gp = S5_GROUPS * S5_STATE
    lr = lam_re.reshape(1, gp)
    li = lam_im.reshape(1, gp)
    ldt = jnp.broadcast_to(log_dt[:, None], (S5_GROUPS, S5_STATE)).reshape(1, gp)
    br = jnp.transpose(b_re, (2, 0, 1)).reshape(S5_GROUP, gp)
    bi = jnp.transpose(b_im, (2, 0, 1)).reshape(S5_GROUP, gp)
    return pl.pallas_call(
        functools.partial(_s5_prep_kernel, sub=sub),
        out_shape=[jax.ShapeDtypeStruct((2, sub, gp), F32), jax.ShapeDtypeStruct((2, sub, gp), F32),
                   jax.ShapeDtypeStruct((2, gp), F32), jax.ShapeDtypeStruct((2, S5_GROUP, gp), F32)],
        name="s5_prep")(lr, li, ldt, br, bi)


def _s5_kernel(u_ref, gs_ref, s0_ref, bblk_ref, cblk_ref, tri_ref, neg_ref, pos_ref, lam1_ref, d_ref,
               wglu_ref, bglu_ref, o_ref, sfin_ref, carry_sc, *, tm, sub):
    ti = pl.program_id(1)
    tile = 2 * LANES
    tiles_per_half = 2 * S5_HALF_STATE // tile

    @pl.when(ti == 0)
    def _():
        carry_sc[...] = s0_ref[0]

    u = u_ref[0]
    ub = u.astype(BF16)
    n_tiles = 2 * tiles_per_half

    def lanes_of(t):
        lo = t * tile
        return slice(lo, lo + LANES), slice(lo + LANES, lo + tile)

    def stage_a(t):
        hf, q = divmod(t, tiles_per_half)
        re, im = lanes_of(t)
        bu = _dot(ub[:, hf * 256:(hf + 1) * 256], bblk_ref[hf, :, q * tile:(q + 1) * tile])
        nre, nim = neg_ref[:, re], neg_ref[:, im]
        xs = []
        for c in range(tm // sub):
            rs = slice(c * sub, (c + 1) * sub)
            bre, bim = bu[rs, :LANES], bu[rs, LANES:]
            xs.append(jnp.concatenate([nre * bre - nim * bim, nre * bim + nim * bre], axis=-1).astype(BF16))
        return jnp.concatenate(xs, axis=0)

    def stage_b(t, x):
        re, im = lanes_of(t)
        cs = _dot(tri_ref[...], x)
        cre, cim = carry_sc[:, re], carry_sc[:, im]
        l1r, l1i = lam1_ref[:, re], lam1_ref[:, im]
        pre, pim = pos_ref[:, re], pos_ref[:, im]
        ss = []
        for c in range(tm // sub):
            rs = slice(c * sub, (c + 1) * sub)
            tre = cs[rs, :LANES] + (l1r * cre - l1i * cim)
            tim = cs[rs, LANES:] + (l1r * cim + l1i * cre)
            sre = pre * tre - pim * tim
            sim = pre * tim + pim * tre
            ss.append(jnp.concatenate([sre, sim], axis=-1).astype(BF16))
            cre = sre[sub - 1:sub, :]
            cim = sim[sub - 1:sub, :]
        carry_sc[:, re] = cre
        carry_sc[:, im] = cim
        return jnp.concatenate(ss, axis=0)

    def stage_c(t, s):
        hf, q = divmod(t, tiles_per_half)
        return _dot(s, cblk_ref[hf, q * tile:(q + 1) * tile, :])

    xs_, ss_ = {}, {}
    accs = [None, None]
    skew = 1
    for step in range(n_tiles + 2 * skew):
        if step < n_tiles:
            xs_[step] = stage_a(step)
        if skew <= step < n_tiles + skew:
            ss_[step - skew] = stage_b(step - skew, xs_.pop(step - skew))
        if step >= 2 * skew:
            t = step - 2 * skew
            part = stage_c(t, ss_.pop(t))
            hf = t // tiles_per_half
            accs[hf] = part if accs[hf] is None else accs[hf] + part
    y = jnp.concatenate(accs, axis=-1) + d_ref[...] * u
    z = jax.nn.gelu(y)
    z = z * jax.nn.sigmoid(_dot(z.astype(BF16), wglu_ref[...]) + bglu_ref[...])
    o_ref[0] = (z * _silu(gs_ref[0])).astype(BF16)

    @pl.when(ti == pl.num_programs(1) - 1)
    def _():
        sfin_ref[0] = carry_sc[...]


def _s5(u, gate, s0, w, *, tm):
    b, t, _ = u.shape
    sub = S5_SUB
    tokb = lambda i, j: (i, j, 0)
    perb = lambda i, j: (i, 0, 0)
    c2 = lambda i, j: (0, 0)
    c3 = lambda i, j: (0, 0, 0)
    tri = w['tri'][:tm, :tm]
    consts = [w['bblk'], w['cblk'], tri, w['neg'], w['pos'], w['lam1'], w['d'], w['w_glu'], w['b_glu']]
    blk = 2 * _nbytes((tm, S5_WIDTH), F32) + _nbytes((tm, S5_WIDTH), BF16) + 2 * _nbytes((1, 4 * S5_HALF_STATE), F32) \
        + sum(_nbytes(a.shape, a.dtype) for a in consts)
    scratch = _nbytes((SUBLANES, 4 * S5_HALF_STATE), F32)
    return pl.pallas_call(
        functools.partial(_s5_kernel, tm=tm, sub=sub),
        grid=(b, t // tm),
        in_specs=[pl.BlockSpec((1, tm, S5_WIDTH), tokb), pl.BlockSpec((1, tm, S5_WIDTH), tokb),
                  pl.BlockSpec((1, 1, 4 * S5_HALF_STATE), perb)]
                 + [pl.BlockSpec(a.shape, c3 if a.ndim == 3 else c2) for a in consts],
        out_specs=[pl.BlockSpec((1, tm, S5_WIDTH), tokb), pl.BlockSpec((1, 1, 4 * S5_HALF_STATE), perb)],
        out_shape=[jax.ShapeDtypeStruct((b, t, S5_WIDTH), BF16),
                   jax.ShapeDtypeStruct((b, 1, 4 * S5_HALF_STATE), F32)],
        scratch_shapes=[pltpu.VMEM((1, 4 * S5_HALF_STATE), F32)],
        compiler_params=pltpu.CompilerParams(
            dimension_semantics=("arbitrary", "arbitrary"),
            vmem_limit_bytes=_vmem_limit(blk, scratch, temp_bytes=16 * _nbytes((tm, 2 * LANES), F32))),
        name="s5_scan")(u, gate, s0, *consts)


def _conv_kernel(x_ref, xn_ref, ma_ref, man_ref, ms_ref, msn_ref, past_ref, woa_ref, wos_ref, ng_ref, win_ref,
                 cw_ref, cb_ref, lng_ref, lnb_ref, wout_ref,
                 y_ref, newc_ref, vext_sc, stage_sc, hstage_sc, xstage_sc, yc_sc, *, tm, pipelined):
    ti = pl.program_id(1)

    def stage_tile(x0, mla, s5m):
        x1 = _dot(mla, woa_ref[...]) + _dot(s5m, wos_ref[...]) + x0
        xstage_sc[...] = x1
        h = _rms(x1, ng_ref[...]).astype(BF16)
        hstage_sc[...] = h
        stage_sc[...] = _dot(h, win_ref[:, 0:D_MODEL]) * jax.nn.sigmoid(_dot(h, win_ref[:, D_MODEL:2 * D_MODEL]))

    @pl.when(ti == 0)
    def _():
        vext_sc[0:CONV_HALO, :] = past_ref[0]
        if pipelined:
            stage_tile(x_ref[0], ma_ref[0], ms_ref[0])

    if not pipelined:
        stage_tile(x_ref[0], ma_ref[0], ms_ref[0])
    x = xstage_sc[...]
    h = hstage_sc[...]
    vext_sc[CONV_HALO:CONV_HALO + tm, :] = stage_sc[...]
    if pipelined:
        stage_tile(xn_ref[0], man_ref[0], msn_ref[0])

    rows = min(tm, 128)
    width = 2 * LANES
    n_groups = rows // SUBLANES
    sub_row = lax.broadcasted_iota(jnp.int32, (SUBLANES, width), 0)
    for rb in range(tm // rows):
        r0 = rb * rows
        for lb in range(D_MODEL // width):
            ls = slice(lb * width, (lb + 1) * width)
            accs = [None] * n_groups
            for r in range(SUBLANES):
                z = None
                for a in range(CONV_HALO // SUBLANES):
                    lag = SUBLANES * a + r
                    start = r0 + (CONV_HALO - SUBLANES - SUBLANES * a)
                    term = cw_ref[lag:lag + 1, ls] * vext_sc[start:start + rows + SUBLANES, ls]
                    z = term if z is None else z + term
                groups = [z[SUBLANES * g:SUBLANES * (g + 1), :] for g in range(n_groups + 1)]
                if r > 0:
                    rot = [pltpu.roll(g, r, 0) for g in groups]
                    groups = [None] + [jnp.where(sub_row < r, rot[g - 1], rot[g]) for g in range(1, n_groups + 1)]
                for g in range(n_groups):
                    part = groups[g + 1]
                    accs[g] = part if accs[g] is None else accs[g] + part
            yc_sc[r0:r0 + rows, ls] = jnp.concatenate(accs, axis=0)

    yc = yc_sc[...] + cb_ref[...]
    mu = jnp.mean(yc, axis=-1, keepdims=True)
    xc = yc - mu
    var = jnp.mean(xc * xc, axis=-1, keepdims=True)
    yn = xc * lax.rsqrt(var + EPS) * lng_ref[...] + lnb_ref[...]
    gate = _dot(h, win_ref[:, 2 * D_MODEL:3 * D_MODEL])
    mixed = (_silu(yn) * _silu(gate)).astype(BF16)
    y_ref[0] = _dot(mixed, wout_ref[...]) + x

    tail = vext_sc[tm:tm + CONV_HALO, :]
    vext_sc[0:CONV_HALO, :] = tail
    newc_ref[0] = tail


def _conv_layer(x, mla, s5m, past32, w, *, tm):
    b, t, _ = x.shape
    nt = t // tm
    tokb = lambda i, j: (i, j, 0)
    nxtb = lambda i, j: (i, jnp.minimum(j + 1, nt - 1), 0)
    perb = lambda i, j: (i, 0, 0)
    const = lambda i, j: (0, 0)
    consts = [w['w_out_a'], w['w_out_s'], w['norm_g'], w['w_in'], w['conv_w'], w['conv_b'], w['ln_g'], w['ln_b'],
              w['w_out']]
    blk = 3 * _nbytes((tm, D_MODEL), F32) + 4 * _nbytes((tm, MLA_WIDTH), BF16) \
        + 2 * _nbytes((CONV_HALO, D_MODEL), F32) + sum(_nbytes(a.shape, a.dtype) for a in consts)
    scratch = _nbytes((tm + CONV_HALO, D_MODEL), F32) + 3 * _nbytes((tm, D_MODEL), F32) \
        + _nbytes((tm, D_MODEL), BF16)
    wide = lambda index: pl.BlockSpec((1, tm, D_MODEL), index)
    half = lambda index: pl.BlockSpec((1, tm, MLA_WIDTH), index)
    return pl.pallas_call(
        functools.partial(_conv_kernel, tm=tm, pipelined=nt > 1),
        grid=(b, nt),
        in_specs=[wide(tokb), wide(nxtb), half(tokb), half(nxtb), half(tokb), half(nxtb),
                  pl.BlockSpec((1, CONV_HALO, D_MODEL), perb)]
                 + [pl.BlockSpec(a.shape, const) for a in consts],
        out_specs=[pl.BlockSpec((1, tm, D_MODEL), tokb), pl.BlockSpec((1, CONV_HALO, D_MODEL), perb)],
        out_shape=[jax.ShapeDtypeStruct((b, t, D_MODEL), F32),
                   jax.ShapeDtypeStruct((b, CONV_HALO, D_MODEL), F32)],
        scratch_shapes=[pltpu.VMEM((tm + CONV_HALO, D_MODEL), F32), pltpu.VMEM((tm, D_MODEL), F32),
                        pltpu.VMEM((tm, D_MODEL), BF16), pltpu.VMEM((tm, D_MODEL), F32),
                        pltpu.VMEM((tm, D_MODEL), F32)],
        compiler_params=pltpu.CompilerParams(
            dimension_semantics=("arbitrary", "arbitrary"),
            vmem_limit_bytes=_vmem_limit(blk, scratch, temp_bytes=8 * _nbytes((tm, D_MODEL), F32))),
        name="conv_layer")(x, x, mla, mla, s5m, s5m, past32, *consts)


def _head_groups(nope, rope, third):
    return jnp.concatenate([nope, rope, third], axis=-1).reshape(nope.shape[0], HEAD_PAD)


def _half_swap(a):
    half = ROPE_DIM // 2
    return jnp.concatenate([a[..., half:], a[..., :half]], axis=-1)


def _lane_pad(a, lo, width=LANES):
    pad = [(0, 0)] * (a.ndim - 1) + [(lo, width - lo - a.shape[-1])]
    return jnp.pad(a, pad)


def _prep_ab_weights(norm_g, w_in, g_q_lat, w_uq, g_kv_lat, w_uk, w_uv, g_q_nope, g_q_rope, g_k_nope, g_k_rope,
                     w_out):
    o_kr = Q_LORA + KV_LORA
    kr_cols = w_in[:, o_kr:o_kr + ROPE_DIM]
    kr_group = jnp.concatenate([kr_cols, _half_swap(kr_cols), jnp.zeros((D_MODEL, LANES - 2 * ROPE_DIM), F32)], -1)
    w_in_p = jnp.concatenate([w_in[:, :o_kr], kr_group, w_in[:, o_kr + ROPE_DIM:]], axis=-1).astype(BF16)
    uq = w_uq.reshape(Q_LORA, MLA_HEADS, NOPE_DIM + ROPE_DIM)
    uq_r = uq[..., NOPE_DIM:]
    w_uq_p = _head_groups(uq[..., :NOPE_DIM], uq_r, _half_swap(uq_r)).astype(BF16)
    zeros_kv = jnp.zeros((KV_LORA, MLA_HEADS, LANES - NOPE_DIM), F32)
    w_uk_p = jnp.concatenate([w_uk, zeros_kv], axis=-1).reshape(KV_LORA, HEAD_PAD).astype(BF16)
    w_uv_t = w_uv.reshape(KV_LORA, MLA_WIDTH).T.astype(BF16)
    r = np.arange(2 * LANES)
    same = (r[:, None] // LANES) == (r[None, :] // LANES)
    ri, ci = r[:, None] % LANES, r[None, :] % LANES
    e2 = np.where(same & (ri < NOPE_DIM) & (ci < NOPE_DIM), 1.0 / NOPE_DIM, 0.0) \
        + np.where(same & (ri >= NOPE_DIM) & (ri < NOPE_DIM + ROPE_DIM) & (ci >= NOPE_DIM), 1.0 / ROPE_DIM, 0.0)
    place = np.zeros((ROPE_DIM, LANES), np.float32)
    place[np.arange(ROPE_DIM), NOPE_DIM + np.arange(ROPE_DIM)] = 1.0
    row = lambda a: a.reshape(1, -1)
    return dict(
        norm_g=row(norm_g), w_in=w_in_p, g_q_lat=row(g_q_lat), w_uq=w_uq_p, g_kv=row(g_kv_lat),
        w_uk=w_uk_p, w_uvt=w_uv_t, e2=jnp.asarray(e2, BF16), place=jnp.asarray(place, BF16),
        gq=row(jnp.concatenate([g_q_nope, g_q_rope, jnp.zeros((ROPE_DIM,), F32)])),
        gqs=row(_lane_pad(_half_swap(g_q_rope), NOPE_DIM)),
        gk=row(_lane_pad(g_k_rope, 0)), gks=row(_lane_pad(_half_swap(g_k_rope), 0)),
        gkn=row(_lane_pad(g_k_nope, 0)), gkn2=row(jnp.tile(_lane_pad(g_k_nope, 0), 2)),
        w_out_a=w_out[:MLA_WIDTH].astype(BF16), w_out_s=w_out[MLA_WIDTH:].astype(BF16))


def _rope_tables(pos, attn_block):
    half = ROPE_DIM // 2
    inv = ROPE_BASE ** (-jnp.arange(half, dtype=F32) / half)
    ang = pos.astype(F32)[:, None] * inv[None, :]
    cos = jnp.cos(ang)
    sin = jnp.sin(ang)
    cosf = jnp.concatenate([cos, cos], axis=-1)
    sinf = jnp.concatenate([-sin, sin], axis=-1)
    ones = jnp.ones((pos.shape[0], NOPE_DIM), F32)
    cq = _lane_pad(jnp.concatenate([ones, cosf], axis=-1), 0)
    sq = _lane_pad(sinf, NOPE_DIM)
    n_t = pos.shape[0]
    if attn_block is None:
        qm = km = jnp.zeros((n_t, LANES), F32)
    else:
        n_chunks = attn_block // CHUNK
        assert MASK_LANE0 + n_chunks <= LANES
        own = (jnp.arange(n_t, dtype=jnp.int32) % attn_block) // CHUNK
        c = jnp.arange(n_chunks, dtype=jnp.int32)
        km = _lane_pad((c[None, :] == own[:, None]).astype(F32), MASK_LANE0)
        qm = _lane_pad(jnp.where(c[None, :] <= own[:, None], 0.0, NEG_INF).astype(F32), MASK_LANE0)
    return cq, sq, _lane_pad(cosf, 0), _lane_pad(sinf, 0), qm, km


def _prep_s5_weights(lam_re, lam_im, log_dt, b_re, b_im, c_re, c_im, d_skip, w_glu, b_glu, *, max_tile):
    pos, neg, lam1, bbar = _s5_prep(lam_re, lam_im, log_dt, b_re, b_im, sub=S5_SUB)
    eye = jnp.eye(S5_HALF_GROUPS, dtype=F32)

    def b_block(bb):
        bb = bb.reshape(S5_GROUP, 2, S5_HALF_GROUPS, S5_STATE)
        return jnp.einsum('nhgp,gk->hgnkp', bb, eye).reshape(2, S5_HALF_GROUPS * S5_GROUP, S5_HALF_STATE)

    def c_block(cc):
        cc = cc.reshape(2, S5_HALF_GROUPS, S5_GROUP, S5_STATE)
        return jnp.einsum('hgnp,gk->hkpgn', cc, eye).reshape(2, S5_HALF_STATE, S5_HALF_GROUPS * S5_GROUP)

    n_t = S5_HALF_STATE // LANES
    b_re_blk, b_im_blk = b_block(bbar[0]), b_block(bbar[1])
    bblk = jnp.stack([b_re_blk.reshape(2, -1, n_t, LANES), b_im_blk.reshape(2, -1, n_t, LANES)], axis=3)
    bblk = bblk.reshape(2, S5_HALF_GROUPS * S5_GROUP, 2 * S5_HALF_STATE).astype(BF16)
    c_re_blk, c_im_blk = c_block(c_re), c_block(-c_im)
    cblk = jnp.stack([c_re_blk.reshape(2, n_t, LANES, -1), c_im_blk.reshape(2, n_t, LANES, -1)], axis=2)
    cblk = cblk.reshape(2, 2 * S5_HALF_STATE, S5_HALF_GROUPS * S5_GROUP).astype(BF16)

    def tiled(tab):
        rows = tab.shape[1]
        return jnp.transpose(tab.reshape(2, rows, 2 * n_t, LANES), (1, 2, 0, 3)).reshape(rows, 4 * S5_HALF_STATE)

    r = np.arange(max_tile)
    tri = ((r[:, None] // S5_SUB) == (r[None, :] // S5_SUB)) & (r[None, :] <= r[:, None])
    return dict(bblk=bblk, cblk=cblk, tri=jnp.asarray(tri, BF16), neg=tiled(neg), pos=tiled(pos),
                lam1=tiled(lam1[:, None, :]),
                d=d_skip.reshape(1, -1), w_glu=w_glu.astype(BF16), b_glu=b_glu.reshape(1, -1))


def _pack_state(s_re, s_im):
    b = s_re.shape[0]
    st = jnp.stack([s_re.reshape(b, -1, LANES), s_im.reshape(b, -1, LANES)], axis=2)
    return st.reshape(b, 1, 4 * S5_HALF_STATE)


def _unpack_state(st):
    b = st.shape[0]
    st = st.reshape(b, -1, 2, LANES)
    return (st[:, :, 0].reshape(b, S5_GROUPS, S5_STATE), st[:, :, 1].reshape(b, S5_GROUPS, S5_STATE))


def _ab_branches(x, pos, past, wa, ws, *, tm_in, tm_s5, tq, tk):
    b, t, _ = x.shape
    x2d = x.reshape(b * t, D_MODEL)
    tabs = _rope_tables(pos, tq if past is None else None)
    lat, kr, qp, kp, vt, gm, u, gs = _ab_in(x2d, tabs, wa, seq_len=t, tm=tm_in)
    r3 = lambda a: a.reshape(b, t, a.shape[-1])
    if past is None:
        mla = _attn_prompt(r3(qp), r3(kp), vt, r3(gm), tq=tq)
        s0 = jnp.zeros((b, 1, 4 * S5_HALF_STATE), F32)
    else:
        past_lat, past_kr, past_re, past_im = past
        vt_b = jnp.transpose(vt.reshape(MLA_WIDTH, b, t), (1, 0, 2))
        mla = _attn_sample(r3(qp), past_lat, past_kr, r3(kp), vt_b, r3(gm), wa, tk=tk)
        s0 = _pack_state(past_re, past_im)
    s5, sfin = _s5(r3(u), r3(gs), s0, ws, tm=tm_s5)
    fin_re, fin_im = _unpack_state(sfin)
    return mla, s5, r3(lat), r3(kr), fin_re, fin_im


def _conv(x, mla, s5m, past, wc, *, tm):
    b = x.shape[0]
    if past is None:
        past32 = jnp.zeros((b, CONV_HALO, D_MODEL), F32)
    else:
        past32 = jnp.pad(past, ((0, 0), (CONV_HALO - (CONV_WIDTH - 1), 0), (0, 0)))
    y, newc = _conv_layer(x, mla, s5m, past32, wc, tm=tm)
    return y, newc[:, CONV_HALO - (CONV_WIDTH - 1):]


def kernel(x_prompt, x_sample, cache_mla_latent, cache_mla_krope, state_s5_re, state_s5_im, state_conv, norm_ab, w_in_ab, g_q_lat, w_uq, g_kv_lat, w_uk, w_uv, g_q_nope, g_q_rope, g_k_nope, g_k_rope, s5_lam_re, s5_lam_im, s5_log_dt, s5_b_re, s5_b_im, s5_c_re, s5_c_im, s5_d, s5_w_glu, s5_b_glu, w_out_ab, norm_c, w_in_c, conv_w, conv_b, ln_g, ln_b, w_out_c):
    t_p = x_prompt.shape[1]
    t_s = x_sample.shape[1]
    past_len = cache_mla_latent.shape[2]
    pos_p = jnp.arange(t_p, dtype=jnp.int32)
    pos_s = past_len + jnp.arange(t_s, dtype=jnp.int32)

    i = 0
    wa = _prep_ab_weights(norm_ab[i], w_in_ab[i], g_q_lat[i], w_uq[i], g_kv_lat[i], w_uk[i], w_uv[i],
                          g_q_nope[i], g_q_rope[i], g_k_nope[i], g_k_rope[i], w_out_ab[i])
    ws = _prep_s5_weights(s5_lam_re[i], s5_lam_im[i], s5_log_dt[i], s5_b_re[i], s5_b_im[i], s5_c_re[i],
                          s5_c_im[i], s5_d[i], s5_w_glu[i], s5_b_glu[i], max_tile=256)
    mla_p, s5_p, lat_p, kr_p, re_p, im_p = _ab_branches(
        x_prompt, pos_p, None, wa, ws, tm_in=256, tm_s5=256, tq=512, tk=512)
    mla_s, s5_s, lat_s, kr_s, re_s, im_s = _ab_branches(
        x_sample, pos_s, (cache_mla_latent[i], cache_mla_krope[i], state_s5_re[i], state_s5_im[i]), wa, ws,
        tm_in=256, tm_s5=t_s, tq=512, tk=1024)

    row = lambda a: a.reshape(1, -1)
    cw = jnp.pad(conv_w[i][::-1], ((0, 1), (0, 0)))
    wc = dict(w_out_a=wa['w_out_a'], w_out_s=wa['w_out_s'],
              norm_g=row(norm_c[i]), w_in=w_in_c[i].astype(BF16), conv_w=cw, conv_b=row(conv_b[i]),
              ln_g=row(ln_g[i]), ln_b=row(ln_b[i]), w_out=w_out_c[i].astype(BF16))
    yp, conv_p = _conv(x_prompt, mla_p, s5_p, None, wc, tm=256)
    ys, conv_s = _conv(x_sample, mla_s, s5_s, state_conv[i], wc, tm=t_s)

    st = lambda a: a[None]
    return (yp, ys, st(lat_p), st(kr_p), st(re_p), st(im_p), st(conv_p),
            st(lat_s), st(kr_s), st(re_s), st(im_s), st(conv_s))
```

```python
import functools
import math

import numpy as np
import jax
import jax.numpy as jnp
from jax import lax
from jax.experimental import pallas as pl
from jax.experimental.pallas import tpu as pltpu

F32 = jnp.float32
BF16 = jnp.bfloat16

D_MODEL = 1024
CHUNK = 64
MLA_HEADS = 8
Q_LORA = 384
KV_LORA = 256
NOPE_DIM = 64
ROPE_DIM = 32
V_DIM = 64
MLA_WIDTH = MLA_HEADS * V_DIM
ROPE_BASE = 10000.0
ATTN_SCALE = (NOPE_DIM + ROPE_DIM) ** -0.5
S5_WIDTH = 512
S5_GROUP = 16
S5_GROUPS = S5_WIDTH // S5_GROUP
S5_STATE = 64
CONV_WIDTH = 31
EPS = 1e-6
NEG_INF = -1e30
LOG2E = math.log2(math.e)
MASK_LANE0 = NOPE_DIM + ROPE_DIM

LANES = 128
SUBLANES = 8
HEAD_PAD = MLA_HEADS * LANES
S5_HALF_GROUPS = S5_GROUPS // 2
S5_HALF_STATE = S5_HALF_GROUPS * S5_STATE
S5_SUB = 32
CONV_HALO = 32
CONV_PAD = 16
V7X_VMEM_BYTES = 64 * 1024 * 1024


def _vmem_limit(block_bytes, scratch_bytes=0, temp_bytes=0):
    est = 2 * block_bytes + scratch_bytes + temp_bytes
    return int(min(max(est, 16 * 1024 * 1024), V7X_VMEM_BYTES - 8 * 1024 * 1024))


def _nbytes(shape, dtype):
    return int(np.prod(shape)) * jnp.dtype(dtype).itemsize


def _rms(x, g):
    return x * lax.rsqrt(jnp.mean(x * x, axis=-1, keepdims=True) + EPS) * g


def _silu(x):
    return x * jax.nn.sigmoid(x)


def _dot(a, b):
    return jnp.dot(a, b, preferred_element_type=F32)


def _dot_nt(a, b):
    return lax.dot_general(a, b, (((1,), (1,)), ((), ())), preferred_element_type=F32)


def _ab_in_kernel(x_ref, ng_ref, win_ref, gql_ref, wuq_ref, gkv_ref, wuk_ref, wuv_ref, e2_ref,
                  cq_ref, sq_ref, ck_ref, sk_ref, qm_ref, km_ref, gq_ref, gqs_ref, gk_ref, gks_ref, gkn_ref,
                  lat_ref, kr_ref, qp_ref, kp_ref, vt_ref, gm_ref, u_ref, gs_ref):
    x = x_ref[...]
    h = _rms(x, ng_ref[...]).astype(BF16)

    def proj(lo, hi):
        return _dot(h, win_ref[:, lo:hi])

    q_lat = proj(0, Q_LORA)
    c_kv = proj(Q_LORA, Q_LORA + KV_LORA)
    krg = proj(640, 768)
    gm_ref[...] = proj(768, 1280)
    u_ref[...] = proj(1280, 1792)
    gs_ref[...] = proj(1792, 2304)

    c_n = _rms(c_kv, gkv_ref[...])
    lat_ref[...] = c_n
    cb = c_n.astype(BF16)

    lane = lax.broadcasted_iota(jnp.int32, (1, LANES), 1)
    ms = jnp.sum(jnp.where(lane < ROPE_DIM, krg * krg, 0.0), axis=-1, keepdims=True) * (1.0 / ROPE_DIM)
    kr = lax.rsqrt(ms + EPS) * (krg * (ck_ref[...] * gk_ref[...])
                                + pltpu.roll(krg, LANES - ROPE_DIM, 1) * (sk_ref[...] * gks_ref[...]))
    kr_ref[...] = kr[:, :ROPE_DIM]
    kr_mask = pltpu.roll(kr, NOPE_DIM, 1) + km_ref[...]

    qn = _rms(q_lat, gql_ref[...]).astype(BF16)
    qa_tab = cq_ref[...] * (gq_ref[...] * (ATTN_SCALE * LOG2E))
    qb_tab = sq_ref[...] * (gqs_ref[...] * (ATTN_SCALE * LOG2E))
    n_pairs = MLA_HEADS // 2
    pair_cols = [slice(2 * LANES * p, 2 * LANES * (p + 1)) for p in range(n_pairs)]
    qas = [_dot(qn, wuq_ref[:, cols]) for cols in pair_cols]
    kas = [_dot(cb, wuk_ref[:, cols]) for cols in pair_cols]
    vt_ref[...] = _dot_nt(wuv_ref[...], cb).astype(BF16)
    q_ms = [_dot((qa * qa).astype(BF16), e2_ref[...]) for qa in qas]
    k_ms = [_dot((ka * ka).astype(BF16), e2_ref[...]) for ka in kas]
    for p in range(n_pairs):
        lo = 2 * LANES * p
        qs = qas[p] * lax.rsqrt(q_ms[p] + EPS)
        ks = kas[p] * lax.rsqrt(k_ms[p] + EPS)
        for j in range(2):
            sl = slice(LANES * j, LANES * (j + 1))
            s = qs[:, sl]
            qp_ref[:, lo + LANES * j:lo + LANES * (j + 1)] = (
                s * qa_tab + pltpu.roll(s, LANES - ROPE_DIM, 1) * qb_tab + qm_ref[...]).astype(BF16)
            kp_ref[:, lo + LANES * j:lo + LANES * (j + 1)] = (
                ks[:, sl] * gkn_ref[...] + kr_mask).astype(BF16)


def _ab_in(x2d, pos_tabs, w, *, seq_len, tm):
    n_tok = x2d.shape[0]
    if tm > seq_len:
        pos_tabs = tuple(jnp.tile(p, (tm // seq_len, 1)) for p in pos_tabs)
    n_pos = max(seq_len // tm, 1)
    tok = lambda i: (i, 0)
    pos = lambda i: (i % n_pos, 0)
    const = lambda i: (0, 0)

    def full(a):
        return pl.BlockSpec(a.shape, const)

    ins = [x2d, w['norm_g'], w['w_in'], w['g_q_lat'], w['w_uq'], w['g_kv'], w['w_uk'], w['w_uvt'], w['e2'],
           *pos_tabs, w['gq'], w['gqs'], w['gk'], w['gks'], w['gkn']]
    n_tabs = len(pos_tabs)
    in_specs = [pl.BlockSpec((tm, D_MODEL), tok)] + [full(a) for a in ins[1:9]] \
        + [pl.BlockSpec((tm, LANES), pos)] * n_tabs + [full(a) for a in ins[9 + n_tabs:]]
    outs = [((n_tok, KV_LORA), F32), ((n_tok, ROPE_DIM), F32), ((n_tok, HEAD_PAD), BF16),
            ((n_tok, HEAD_PAD), BF16), ((MLA_WIDTH, n_tok), BF16), ((n_tok, MLA_WIDTH), F32),
            ((n_tok, S5_WIDTH), F32), ((n_tok, S5_WIDTH), F32)]
    vt_index = 4
    out_specs = [pl.BlockSpec((MLA_WIDTH, tm), lambda i: (0, i)) if n == vt_index
                 else pl.BlockSpec((tm, s[1]), tok) for n, (s, _) in enumerate(outs)]
    out_shape = [jax.ShapeDtypeStruct(s, d) for s, d in outs]
    blk = sum(_nbytes(a.shape, a.dtype) for a in ins[1:9]) + _nbytes((tm, D_MODEL), F32) \
        + n_tabs * _nbytes((tm, LANES), F32) + sum(_nbytes(s, d) for s, d in outs) * tm // n_tok
    return pl.pallas_call(
        _ab_in_kernel, grid=(n_tok // tm,), in_specs=in_specs, out_specs=out_specs, out_shape=out_shape,
        compiler_params=pltpu.CompilerParams(
            dimension_semantics=("arbitrary",),
            vmem_limit_bytes=_vmem_limit(blk, temp_bytes=8 * _nbytes((tm, D_MODEL), F32))),
        name="ab_in")(*ins)


def _ones_row(n):
    return (lax.broadcasted_iota(jnp.int32, (V_DIM, n), 0) == 0).astype(BF16)


def _attn_prompt_kernel(qi_ref, kj_ref, q_ref, k_ref, vt_ref, g_ref, o_ref, vaug_sc, s_sc, acc_sc, *, tq,
                        seq_len):
    n_steps = qi_ref.shape[0]
    ones_row = _ones_row(seq_len)
    for hh in range(2):
        vaug_sc[hh, 0:V_DIM, :] = vt_ref[hh * V_DIM:(hh + 1) * V_DIM, :]
        vaug_sc[hh, V_DIM:2 * V_DIM, :] = ones_row
    lane = lax.broadcasted_iota(jnp.int32, (1, LANES), 1)
    mask_lane_off = jnp.logical_or(lane < MASK_LANE0, lane >= MASK_LANE0 + tq // CHUNK)

    def produce(n, slot):
        qi = qi_ref[n]
        kj = kj_ref[n]
        r0 = pl.multiple_of(qi * tq, tq)
        k0 = pl.multiple_of(kj * tq, tq)
        q_keep = jnp.logical_or(mask_lane_off, kj == qi)
        bms = []
        for hh in range(2):
            hs = slice(hh * LANES, (hh + 1) * LANES)
            q = q_ref[0, pl.ds(r0, tq), hs]
            q = jnp.where(q_keep, q, jnp.zeros_like(q))
            s = _dot_nt(k_ref[0, pl.ds(k0, tq), hs], q)
            s_sc[slot, hh] = s
            bms.append(jnp.max(s, axis=0, keepdims=True))
        return tuple(bms)

    def consume(n, slot, bms, state):
        qi = qi_ref[n]
        kj = kj_ref[n]
        k0 = pl.multiple_of(kj * tq, tq)
        out = []
        for hh in range(2):
            m, acc = state[hh]
            m = jnp.where(kj == 0, -jnp.inf, m)
            m_new = jnp.maximum(m, bms[hh])
            p = jnp.exp2(s_sc[slot, hh] - m_new)
            alpha = jnp.exp2(m - m_new)
            acc = alpha * acc + _dot(vaug_sc[hh, :, pl.ds(k0, tq)], p.astype(BF16))
            acc_sc[qi, hh] = acc
            out.append((m_new, acc))
        return tuple(out)

    unroll = 4

    def multi_step(t, carry):
        bms, state = carry
        for u in range(unroll):
            n = unroll * t + u
            next_bms = produce(n + 1, (u + 1) % 2)
            state = consume(n, u % 2, bms, state)
            bms = next_bms
        return bms, state

    state = tuple((jnp.full((1, tq), -jnp.inf, F32), jnp.zeros((2 * V_DIM, tq), F32)) for _ in range(2))
    n_loops = (n_steps - 1) // unroll
    bms, state = lax.fori_loop(0, n_loops, multi_step, (produce(0, 0), state))
    for n in range(unroll * n_loops, n_steps):
        next_bms = produce(n + 1, (n + 1) % 2) if n + 1 < n_steps else None
        state = consume(n, n % 2, bms, state)
        bms = next_bms

    def finalize(qi, carry):
        r0 = pl.multiple_of(qi * tq, tq)
        vals = []
        for hh in range(2):
            acc = acc_sc[qi, hh]
            vals.append(acc[0:V_DIM, :] * (1.0 / acc[V_DIM:V_DIM + 1, :]))
        o = jnp.concatenate(vals, axis=0).T
        o_ref[0, pl.ds(r0, tq), :] = (o * _silu(g_ref[0, pl.ds(r0, tq), :])).astype(BF16)
        return carry

    lax.fori_loop(0, seq_len // tq, finalize, 0)


def _attn_prompt(qp, kp, vt, gate, *, tq):
    b, t, _ = qp.shape
    nq = t // tq
    steps = [(qi, kj) for qi in range(nq) for kj in range(qi + 1)]
    qi_tab = jnp.asarray([s[0] for s in steps], jnp.int32)
    kj_tab = jnp.asarray([s[1] for s in steps], jnp.int32)
    pair = lambda i, j, qt, kt: (i, 0, j)
    blk = 2 * _nbytes((t, 2 * LANES), BF16) + _nbytes((LANES, t), BF16) + _nbytes((t, LANES), F32) \
        + _nbytes((t, LANES), BF16)
    scratch = _nbytes((2, LANES, t), BF16) + _nbytes((2, 2, tq, tq), F32) + _nbytes((nq, 2, LANES, tq), F32)
    return pl.pallas_call(
        functools.partial(_attn_prompt_kernel, tq=tq, seq_len=t),
        grid_spec=pltpu.PrefetchScalarGridSpec(
            num_scalar_prefetch=2,
            grid=(b, MLA_HEADS // 2),
            in_specs=[pl.BlockSpec((1, t, 2 * LANES), pair), pl.BlockSpec((1, t, 2 * LANES), pair),
                      pl.BlockSpec((LANES, t), lambda i, j, qt, kt: (j, i)), pl.BlockSpec((1, t, LANES), pair)],
            out_specs=pl.BlockSpec((1, t, LANES), pair),
            scratch_shapes=[pltpu.VMEM((2, LANES, t), BF16), pltpu.VMEM((2, 2, tq, tq), F32),
                            pltpu.VMEM((nq, 2, LANES, tq), F32)]),
        out_shape=jax.ShapeDtypeStruct((b, t, MLA_WIDTH), BF16),
        compiler_params=pltpu.CompilerParams(
            dimension_semantics=("arbitrary", "arbitrary"),
            vmem_limit_bytes=_vmem_limit(blk, scratch, temp_bytes=12 * _nbytes((tq, tq), F32))),
        name="attn_prompt")(qi_tab, kj_tab, qp, kp, vt, gate)


def _attn_sample_kernel(q_ref, lat_ref, kr_ref, kpn_ref, vtn_ref, g_ref, wuk_ref, wuvt_ref, e2_ref, gkn2_ref,
                        place_ref, o_ref, qbd_sc, m_sc, acc_sc, *, n_past_blocks):
    j = pl.program_id(1)
    n_pairs = MLA_HEADS // 2
    t = q_ref.shape[1]
    lane = lax.broadcasted_iota(jnp.int32, (1, LANES), 1)

    @pl.when(j == 0)
    def _():
        m_sc[...] = jnp.full(m_sc.shape, -jnp.inf, F32)
        acc_sc[...] = jnp.zeros(acc_sc.shape, F32)
        q = q_ref[0].astype(F32)
        q_t = jnp.concatenate([q, jnp.zeros((LANES - t, HEAD_PAD), F32)], axis=0).T
        for p in range(n_pairs):
            top = q_t[2 * LANES * p:2 * LANES * p + LANES, :]
            bot = pltpu.roll(q_t[2 * LANES * p + LANES:2 * LANES * (p + 1), :], t, 1)
            qbd_sc[p] = jnp.concatenate([top, bot], axis=0).astype(BF16)

    def update(p, kp_pair, vt_pair):
        n = kp_pair.shape[0]
        s = _dot(kp_pair, qbd_sc[p])
        m_old = m_sc[p]
        m_new = jnp.maximum(m_old, jnp.max(s, axis=0, keepdims=True))
        pr = jnp.exp2(s - m_new)
        alpha = jnp.exp2(m_old - m_new)
        ones_row = _ones_row(n)
        vaug = jnp.concatenate([vt_pair[0:V_DIM, :], ones_row, vt_pair[V_DIM:2 * V_DIM, :], ones_row], axis=0)
        acc_sc[p] = alpha * acc_sc[p] + _dot(vaug, pr.astype(BF16))
        m_sc[p] = m_new

    @pl.when(j < n_past_blocks)
    def _():
        latb = lat_ref[0].astype(BF16)
        kr128 = _dot(kr_ref[0].astype(BF16), place_ref[...])
        kr256 = jnp.concatenate([kr128, kr128], axis=1)
        vt_all = _dot_nt(wuvt_ref[...], latb).astype(BF16)
        kas = [_dot(latb, wuk_ref[:, 2 * LANES * p:2 * LANES * (p + 1)]) for p in range(n_pairs)]
        mss = [_dot((ka * ka).astype(BF16), e2_ref[...]) for ka in kas]
        kps = [(ka * lax.rsqrt(ms + EPS) * gkn2_ref[...] + kr256).astype(BF16) for ka, ms in zip(kas, mss)]
        for p in range(n_pairs):
            update(p, kps[p], vt_all[LANES * p:LANES * (p + 1), :])

    @pl.when(j == n_past_blocks)
    def _():
        for p in range(n_pairs):
            update(p, kpn_ref[0, :, 2 * LANES * p:2 * LANES * (p + 1)], vtn_ref[0, LANES * p:LANES * (p + 1), :])
        for p in range(n_pairs):
            acc_t = acc_sc[p].T
            a0 = acc_t[0:t, 0:LANES]
            a1 = acc_t[t:2 * t, LANES:2 * LANES]
            o0 = a0 * (1.0 / a0[:, V_DIM:V_DIM + 1])
            o1 = a1 * (1.0 / a1[:, V_DIM:V_DIM + 1])
            o = jnp.where(lane < V_DIM, o0, pltpu.roll(o1, V_DIM, 1))
            g = g_ref[0, :, p * LANES:(p + 1) * LANES]
            o_ref[0, :, p * LANES:(p + 1) * LANES] = (o * _silu(g)).astype(BF16)


def _attn_sample(qp, past_lat, past_kr, kp_new, vt_new, gate, w, *, tk):
    b, t, _ = qp.shape
    assert 2 * t == LANES, "two heads' queries share one 128-lane group"
    n_past_blocks = past_lat.shape[1] // tk
    n_pairs = MLA_HEADS // 2
    cur = lambda i, j: (i, 0, 0)
    past = lambda i, j: (i, jnp.minimum(j, n_past_blocks - 1), 0)
    const = lambda i, j: (0, 0)
    consts = [w['w_uk'], w['w_uvt'], w['e2'], w['gkn2'], w['place']]
    blk = _nbytes((tk, KV_LORA), F32) + _nbytes((tk, LANES), F32) + 2 * _nbytes((t, HEAD_PAD), BF16) \
        + _nbytes((MLA_WIDTH, LANES), BF16) + 2 * _nbytes((t, MLA_WIDTH), F32) \
        + sum(_nbytes(a.shape, a.dtype) for a in consts)
    scratch = _nbytes((n_pairs, 2 * LANES, LANES), BF16) + _nbytes((n_pairs, SUBLANES, LANES), F32) \
        + _nbytes((n_pairs, 2 * LANES, LANES), F32)
    return pl.pallas_call(
        functools.partial(_attn_sample_kernel, n_past_blocks=n_past_blocks),
        grid=(b, n_past_blocks + 1),
        in_specs=[pl.BlockSpec((1, t, HEAD_PAD), cur),
                  pl.BlockSpec((1, tk, KV_LORA), past),
                  pl.BlockSpec((1, tk, ROPE_DIM), past),
                  pl.BlockSpec((1, t, HEAD_PAD), cur),
                  pl.BlockSpec((1, MLA_WIDTH, t), cur),
                  pl.BlockSpec((1, t, MLA_WIDTH), cur)] + [pl.BlockSpec(a.shape, const) for a in consts],
        out_specs=pl.BlockSpec((1, t, MLA_WIDTH), cur),
        out_shape=jax.ShapeDtypeStruct((b, t, MLA_WIDTH), BF16),
        scratch_shapes=[pltpu.VMEM((n_pairs, 2 * LANES, LANES), BF16),
                        pltpu.VMEM((n_pairs, 1, LANES), F32),
                        pltpu.VMEM((n_pairs, 2 * LANES, LANES), F32)],
        compiler_params=pltpu.CompilerParams(
            dimension_semantics=("arbitrary", "arbitrary"),
            vmem_limit_bytes=_vmem_limit(blk, scratch, temp_bytes=12 * _nbytes((tk, 2 * LANES), F32))),
        name="attn_sample")(qp, past_lat, past_kr, kp_new, vt_new, gate, *consts)


def _s5_prep_kernel(lr_ref, li_ref, ldt_ref, br_ref, bi_ref, pos_ref, neg_ref, lam1_ref, bbar_ref, *, sub):
    lr = lr_ref[...]
    li = li_ref[...]
    dt = jnp.exp(ldt_ref[...])
    ar = lr * dt
    ai = li * dt
    k = lax.broadcasted_iota(jnp.int32, (sub, 1), 0).astype(F32)
    mag_p = jnp.exp(k * ar)
    mag_n = jnp.exp(-k * ar)
    ang = k * ai
    c = jnp.cos(ang)
    s = jnp.sin(ang)
    pos_ref[0] = mag_p * c
    pos_ref[1] = mag_p * s
    neg_ref[0] = mag_n * c
    neg_ref[1] = -mag_n * s
    mag1 = jnp.exp(ar)
    l1r = mag1 * jnp.cos(ai)
    l1i = mag1 * jnp.sin(ai)
    lam1_ref[0:1, :] = l1r
    lam1_ref[1:2, :] = l1i
    inv = 1.0 / (lr * lr + li * li)
    nr = l1r - 1.0
    cr = (nr * lr + l1i * li) * inv
    ci = (l1i * lr - nr * li) * inv
    br = br_ref[...]
    bi = bi_ref[...]
    bbar_ref[0] = cr * br - ci * bi
    bbar_ref[1] = cr * bi + ci * br


def _s5_prep(lam_re, lam_im, log_dt, b_re, b_im, *, sub):
    gp = S5_GROUPS * S5_STATE
    lr = lam_re.reshape(1, gp)
    li = lam_im.reshape(1, gp)
    ldt = jnp.broadcast_to(log_dt[:, None], (S5_GROUPS, S5_STATE)).reshape(1, gp)
    br = jnp.transpose(b_re, (2, 0, 1)).reshape(S5_GROUP, gp)
    bi = jnp.transpose(b_im, (2, 0, 1)).reshape(S5_GROUP, gp)
    return pl.pallas_call(
        functools.partial(_s5_prep_kernel, sub=sub),
        out_shape=[jax.ShapeDtypeStruct((2, sub, gp), F32), jax.ShapeDtypeStruct((2, sub, gp), F32),
                   jax.ShapeDtypeStruct((2, gp), F32), jax.ShapeDtypeStruct((2, S5_GROUP, gp), F32)],
        name="s5_prep")(lr, li, ldt, br, bi)


def _s5_kernel(u_ref, gs_ref, s0_ref, bblk_ref, cblk_ref, tri_ref, neg_ref, pos_ref, lam1_ref, d_ref,
               wglu_ref, bglu_ref, o_ref, sfin_ref, carry_sc, *, tm, sub):
    ti = pl.program_id(1)
    tile = 2 * LANES
    tiles_per_half = 2 * S5_HALF_STATE // tile

    @pl.when(ti == 0)
    def _():
        carry_sc[...] = s0_ref[0]

    u = u_ref[0]
    ub = u.astype(BF16)
    n_tiles = 2 * tiles_per_half

    def lanes_of(t):
        lo = t * tile
        return slice(lo, lo + LANES), slice(lo + LANES, lo + tile)

    def stage_a(t):
        hf, q = divmod(t, tiles_per_half)
        re, im = lanes_of(t)
        bu = _dot(ub[:, hf * 256:(hf + 1) * 256], bblk_ref[hf, :, q * tile:(q + 1) * tile])
        nre, nim = neg_ref[:, re], neg_ref[:, im]
        xs = []
        for c in range(tm // sub):
            rs = slice(c * sub, (c + 1) * sub)
            bre, bim = bu[rs, :LANES], bu[rs, LANES:]
            xs.append(jnp.concatenate([nre * bre - nim * bim, nre * bim + nim * bre], axis=-1).astype(BF16))
        return jnp.concatenate(xs, axis=0)

    def stage_b(t, x):
        re, im = lanes_of(t)
        cs = _dot(tri_ref[...], x)
        cre, cim = carry_sc[:, re], carry_sc[:, im]
        l1r, l1i = lam1_ref[:, re], lam1_ref[:, im]
        pre, pim = pos_ref[:, re], pos_ref[:, im]
        ss = []
        for c in range(tm // sub):
            rs = slice(c * sub, (c + 1) * sub)
            tre = cs[rs, :LANES] + (l1r * cre - l1i * cim)
            tim = cs[rs, LANES:] + (l1r * cim + l1i * cre)
            sre = pre * tre - pim * tim
            sim = pre * tim + pim * tre
            ss.append(jnp.concatenate([sre, sim], axis=-1).astype(BF16))
            cre = sre[sub - 1:sub, :]
            cim = sim[sub - 1:sub, :]
        carry_sc[:, re] = cre
        carry_sc[:, im] = cim
        return jnp.concatenate(ss, axis=0)

    def stage_c(t, s):
        hf, q = divmod(t, tiles_per_half)
        return _dot(s, cblk_ref[hf, q * tile:(q + 1) * tile, :])

    xs_, ss_ = {}, {}
    accs = [None, None]
    skew = 1
    for step in range(n_tiles + 2 * skew):
        if step < n_tiles:
            xs_[step] = stage_a(step)
        if skew <= step < n_tiles + skew:
            ss_[step - skew] = stage_b(step - skew, xs_.pop(step - skew))
        if step >= 2 * skew:
            t = step - 2 * skew
            part = stage_c(t, ss_.pop(t))
            hf = t // tiles_per_half
            accs[hf] = part if accs[hf] is None else accs[hf] + part
    y = jnp.concatenate(accs, axis=-1) + d_ref[...] * u
    z = jax.nn.gelu(y)
    z = z * jax.nn.sigmoid(_dot(z.astype(BF16), wglu_ref[...]) + bglu_ref[...])
    o_ref[0] = (z * _silu(gs_ref[0])).astype(BF16)

    @pl.when(ti == pl.num_programs(1) - 1)
    def _():
        sfin_ref[0] = carry_sc[...]


def _s5(u, gate, s0, w, *, tm):
    b, t, _ = u.shape
    sub = S5_SUB
    tokb = lambda i, j: (i, j, 0)
    perb = lambda i, j: (i, 0, 0)
    c2 = lambda i, j: (0, 0)
    c3 = lambda i, j: (0, 0, 0)
    tri = w['tri'][:tm, :tm]
    consts = [w['bblk'], w['cblk'], tri, w['neg'], w['pos'], w['lam1'], w['d'], w['w_glu'], w['b_glu']]
    blk = 2 * _nbytes((tm, S5_WIDTH), F32) + _nbytes((tm, S5_WIDTH), BF16) + 2 * _nbytes((1, 4 * S5_HALF_STATE), F32) \
        + sum(_nbytes(a.shape, a.dtype) for a in consts)
    scratch = _nbytes((SUBLANES, 4 * S5_HALF_STATE), F32)
    return pl.pallas_call(
        functools.partial(_s5_kernel, tm=tm, sub=sub),
        grid=(b, t // tm),
        in_specs=[pl.BlockSpec((1, tm, S5_WIDTH), tokb), pl.BlockSpec((1, tm, S5_WIDTH), tokb),
                  pl.BlockSpec((1, 1, 4 * S5_HALF_STATE), perb)]
                 + [pl.BlockSpec(a.shape, c3 if a.ndim == 3 else c2) for a in consts],
        out_specs=[pl.BlockSpec((1, tm, S5_WIDTH), tokb), pl.BlockSpec((1, 1, 4 * S5_HALF_STATE), perb)],
        out_shape=[jax.ShapeDtypeStruct((b, t, S5_WIDTH), BF16),
                   jax.ShapeDtypeStruct((b, 1, 4 * S5_HALF_STATE), F32)],
        scratch_shapes=[pltpu.VMEM((1, 4 * S5_HALF_STATE), F32)],
        compiler_params=pltpu.CompilerParams(
            dimension_semantics=("arbitrary", "arbitrary"),
            vmem_limit_bytes=_vmem_limit(blk, scratch, temp_bytes=16 * _nbytes((tm, 2 * LANES), F32))),
        name="s5_scan")(u, gate, s0, *consts)


def _conv_kernel(x_ref, xn_ref, ma_ref, man_ref, ms_ref, msn_ref, past_ref, woa_ref, wos_ref, ng_ref, win_ref,
                 cw_ref, cb_ref, lng_ref, lnb_ref, wout_ref,
                 shift_ref, y_ref, newc_ref, vext_sc, vbe_sc, vbo_sc, stage_sc, hstage_sc, xstage_sc, yc_sc, *,
                 tm, pipelined):
    ti = pl.program_id(1)
    base = CONV_PAD + CONV_HALO

    def stage_tile(x0, mla, s5m):
        x1 = _dot(mla, woa_ref[...]) + _dot(s5m, wos_ref[...]) + x0
        xstage_sc[...] = x1
        h = _rms(x1, ng_ref[...]).astype(BF16)
        hstage_sc[...] = h
        stage_sc[...] = _dot(h, win_ref[:, 0:D_MODEL]) * jax.nn.sigmoid(_dot(h, win_ref[:, D_MODEL:2 * D_MODEL]))

    @pl.when(ti == 0)
    def _():
        vext_sc[0:CONV_PAD, :] = jnp.zeros((CONV_PAD, D_MODEL), F32)
        vext_sc[CONV_PAD:base, :] = past_ref[0]
        if pipelined:
            stage_tile(x_ref[0], ma_ref[0], ms_ref[0])

    if not pipelined:
        stage_tile(x_ref[0], ma_ref[0], ms_ref[0])
    x = xstage_sc[...]
    h = hstage_sc[...]
    vext_sc[base:base + tm, :] = stage_sc[...]
    if pipelined:
        stage_tile(xn_ref[0], man_ref[0], msn_ref[0])

    vbe_sc[...] = vext_sc[...].astype(BF16)
    vbo_sc[...] = vext_sc[SUBLANES:SUBLANES + tm + CONV_HALO, :].astype(BF16)
    rows = min(tm, 128)
    width = 2 * LANES
    win = rows + CONV_PAD
    for rb in range(tm // rows):
        r0 = rb * rows
        for lb in range(D_MODEL // width):
            ls = slice(lb * width, (lb + 1) * width)
            zs = []
            for r in range(SUBLANES):
                z = None
                for a in range(CONV_HALO // SUBLANES):
                    start = r0 + base - CONV_PAD - SUBLANES * a
                    if a % 2 == 0:
                        src = vbe_sc[start:start + win, ls]
                    else:
                        src = vbo_sc[start - SUBLANES:start - SUBLANES + win, ls]
                    term = src.reshape(win // CONV_PAD, CONV_PAD, width) * cw_ref[SUBLANES * a + r, :, ls][None]
                    z = term if z is None else z + term
                zs.append(z)
            zcat = jnp.concatenate(zs, axis=0).reshape(SUBLANES * win, width)
            yc_sc[r0:r0 + rows, ls] = _dot(shift_ref[...], zcat)

    yc = yc_sc[...] + cb_ref[...]
    mu = jnp.mean(yc, axis=-1, keepdims=True)
    xc = yc - mu
    var = jnp.mean(xc * xc, axis=-1, keepdims=True)
    yn = xc * lax.rsqrt(var + EPS) * lng_ref[...] + lnb_ref[...]
    gate = _dot(h, win_ref[:, 2 * D_MODEL:3 * D_MODEL])
    mixed = (_silu(yn) * _silu(gate)).astype(BF16)
    y_ref[0] = _dot(mixed, wout_ref[...]) + x

    tail = vext_sc[CONV_PAD + tm:base + tm, :]
    vext_sc[CONV_PAD:base, :] = tail
    newc_ref[0] = tail


def _conv_layer(x, mla, s5m, past32, w, *, tm):
    b, t, _ = x.shape
    nt = t // tm
    tokb = lambda i, j: (i, j, 0)
    nxtb = lambda i, j: (i, jnp.minimum(j + 1, nt - 1), 0)
    perb = lambda i, j: (i, 0, 0)
    const = lambda i, j: (0, 0)
    rows = min(tm, 128)
    win = rows + CONV_PAD
    shift = np.zeros((rows, SUBLANES * win), np.float32)
    for r in range(SUBLANES):
        shift[np.arange(rows), r * win + np.arange(rows) + CONV_PAD - r] = 1.0
    consts = [w['w_out_a'], w['w_out_s'], w['norm_g'], w['w_in'], w['conv_w'], w['conv_b'], w['ln_g'], w['ln_b'],
              w['w_out'], jnp.asarray(shift, BF16)]
    blk = 3 * _nbytes((tm, D_MODEL), F32) + 4 * _nbytes((tm, MLA_WIDTH), BF16) \
        + 2 * _nbytes((CONV_HALO, D_MODEL), F32) + sum(_nbytes(a.shape, a.dtype) for a in consts)
    ext = tm + CONV_PAD + CONV_HALO
    scratch = _nbytes((ext, D_MODEL), F32) + _nbytes((ext, D_MODEL), BF16) + _nbytes((tm + CONV_HALO, D_MODEL), BF16) \
        + 3 * _nbytes((tm, D_MODEL), F32) + _nbytes((tm, D_MODEL), BF16)
    wide = lambda index: pl.BlockSpec((1, tm, D_MODEL), index)
    half = lambda index: pl.BlockSpec((1, tm, MLA_WIDTH), index)
    const3 = lambda i, j: (0, 0, 0)
    return pl.pallas_call(
        functools.partial(_conv_kernel, tm=tm, pipelined=nt > 1),
        grid=(b, nt),
        in_specs=[wide(tokb), wide(nxtb), half(tokb), half(nxtb), half(tokb), half(nxtb),
                  pl.BlockSpec((1, CONV_HALO, D_MODEL), perb)]
                 + [pl.BlockSpec(a.shape, const3 if a.ndim == 3 else const) for a in consts],
        out_specs=[pl.BlockSpec((1, tm, D_MODEL), tokb), pl.BlockSpec((1, CONV_HALO, D_MODEL), perb)],
        out_shape=[jax.ShapeDtypeStruct((b, t, D_MODEL), F32),
                   jax.ShapeDtypeStruct((b, CONV_HALO, D_MODEL), F32)],
        scratch_shapes=[pltpu.VMEM((ext, D_MODEL), F32), pltpu.VMEM((ext, D_MODEL), BF16),
                        pltpu.VMEM((tm + CONV_HALO, D_MODEL), BF16), pltpu.VMEM((tm, D_MODEL), F32),
                        pltpu.VMEM((tm, D_MODEL), BF16), pltpu.VMEM((tm, D_MODEL), F32),
                        pltpu.VMEM((tm, D_MODEL), F32)],
        compiler_params=pltpu.CompilerParams(
            dimension_semantics=("arbitrary", "arbitrary"),
            vmem_limit_bytes=_vmem_limit(blk, scratch, temp_bytes=8 * _nbytes((tm, D_MODEL), F32))),
        name="conv_layer")(x, x, mla, mla, s5m, s5m, past32, *consts)


def _head_groups(nope, rope, third):
    return jnp.concatenate([nope, rope, third], axis=-1).reshape(nope.shape[0], HEAD_PAD)


def _half_swap(a):
    half = ROPE_DIM // 2
    return jnp.concatenate([a[..., half:], a[..., :half]], axis=-1)


def _lane_pad(a, lo, width=LANES):
    pad = [(0, 0)] * (a.ndim - 1) + [(lo, width - lo - a.shape[-1])]
    return jnp.pad(a, pad)


def _prep_ab_weights(norm_g, w_in, g_q_lat, w_uq, g_kv_lat, w_uk, w_uv, g_q_nope, g_q_rope, g_k_nope, g_k_rope,
                     w_out):
    o_kr = Q_LORA + KV_LORA
    kr_cols = w_in[:, o_kr:o_kr + ROPE_DIM]
    kr_group = jnp.concatenate([kr_cols, _half_swap(kr_cols), jnp.zeros((D_MODEL, LANES - 2 * ROPE_DIM), F32)], -1)
    w_in_p = jnp.concatenate([w_in[:, :o_kr], kr_group, w_in[:, o_kr + ROPE_DIM:]], axis=-1).astype(BF16)
    uq = w_uq.reshape(Q_LORA, MLA_HEADS, NOPE_DIM + ROPE_DIM)
    uq_r = uq[..., NOPE_DIM:]
    w_uq_p = _head_groups(uq[..., :NOPE_DIM], uq_r, _half_swap(uq_r)).astype(BF16)
    zeros_kv = jnp.zeros((KV_LORA, MLA_HEADS, LANES - NOPE_DIM), F32)
    w_uk_p = jnp.concatenate([w_uk, zeros_kv], axis=-1).reshape(KV_LORA, HEAD_PAD).astype(BF16)
    w_uv_t = w_uv.reshape(KV_LORA, MLA_WIDTH).T.astype(BF16)
    r = np.arange(2 * LANES)
    same = (r[:, None] // LANES) == (r[None, :] // LANES)
    ri, ci = r[:, None] % LANES, r[None, :] % LANES
    e2 = np.where(same & (ri < NOPE_DIM) & (ci < NOPE_DIM), 1.0 / NOPE_DIM, 0.0) \
        + np.where(same & (ri >= NOPE_DIM) & (ri < NOPE_DIM + ROPE_DIM) & (ci >= NOPE_DIM), 1.0 / ROPE_DIM, 0.0)
    place = np.zeros((ROPE_DIM, LANES), np.float32)
    place[np.arange(ROPE_DIM), NOPE_DIM + np.arange(ROPE_DIM)] = 1.0
    row = lambda a: a.reshape(1, -1)
    return dict(
        norm_g=row(norm_g), w_in=w_in_p, g_q_lat=row(g_q_lat), w_uq=w_uq_p, g_kv=row(g_kv_lat),
        w_uk=w_uk_p, w_uvt=w_uv_t, e2=jnp.asarray(e2, BF16), place=jnp.asarray(place, BF16),
        gq=row(jnp.concatenate([g_q_nope, g_q_rope, jnp.zeros((ROPE_DIM,), F32)])),
        gqs=row(_lane_pad(_half_swap(g_q_rope), NOPE_DIM)),
        gk=row(_lane_pad(g_k_rope, 0)), gks=row(_lane_pad(_half_swap(g_k_rope), 0)),
        gkn=row(_lane_pad(g_k_nope, 0)), gkn2=row(jnp.tile(_lane_pad(g_k_nope, 0), 2)),
        w_out_a=w_out[:MLA_WIDTH].astype(BF16), w_out_s=w_out[MLA_WIDTH:].astype(BF16))


def _rope_tables(pos, attn_block):
    half = ROPE_DIM // 2
    inv = ROPE_BASE ** (-jnp.arange(half, dtype=F32) / half)
    ang = pos.astype(F32)[:, None] * inv[None, :]
    cos = jnp.cos(ang)
    sin = jnp.sin(ang)
    cosf = jnp.concatenate([cos, cos], axis=-1)
    sinf = jnp.concatenate([-sin, sin], axis=-1)
    ones = jnp.ones((pos.shape[0], NOPE_DIM), F32)
    cq = _lane_pad(jnp.concatenate([ones, cosf], axis=-1), 0)
    sq = _lane_pad(sinf, NOPE_DIM)
    n_t = pos.shape[0]
    if attn_block is None:
        qm = km = jnp.zeros((n_t, LANES), F32)
    else:
        n_chunks = attn_block // CHUNK
        assert MASK_LANE0 + n_chunks <= LANES
        own = (jnp.arange(n_t, dtype=jnp.int32) % attn_block) // CHUNK
        c = jnp.arange(n_chunks, dtype=jnp.int32)
        km = _lane_pad((c[None, :] == own[:, None]).astype(F32), MASK_LANE0)
        qm = _lane_pad(jnp.where(c[None, :] <= own[:, None], 0.0, NEG_INF).astype(F32), MASK_LANE0)
    return cq, sq, _lane_pad(cosf, 0), _lane_pad(sinf, 0), qm, km


def _prep_s5_weights(lam_re, lam_im, log_dt, b_re, b_im, c_re, c_im, d_skip, w_glu, b_glu, *, max_tile):
    pos, neg, lam1, bbar = _s5_prep(lam_re, lam_im, log_dt, b_re, b_im, sub=S5_SUB)
    eye = jnp.eye(S5_HALF_GROUPS, dtype=F32)

    def b_block(bb):
        bb = bb.reshape(S5_GROUP, 2, S5_HALF_GROUPS, S5_STATE)
        return jnp.einsum('nhgp,gk->hgnkp', bb, eye).reshape(2, S5_HALF_GROUPS * S5_GROUP, S5_HALF_STATE)

    def c_block(cc):
        cc = cc.reshape(2, S5_HALF_GROUPS, S5_GROUP, S5_STATE)
        return jnp.einsum('hgnp,gk->hkpgn', cc, eye).reshape(2, S5_HALF_STATE, S5_HALF_GROUPS * S5_GROUP)

    n_t = S5_HALF_STATE // LANES
    b_re_blk, b_im_blk = b_block(bbar[0]), b_block(bbar[1])
    bblk = jnp.stack([b_re_blk.reshape(2, -1, n_t, LANES), b_im_blk.reshape(2, -1, n_t, LANES)], axis=3)
    bblk = bblk.reshape(2, S5_HALF_GROUPS * S5_GROUP, 2 * S5_HALF_STATE).astype(BF16)
    c_re_blk, c_im_blk = c_block(c_re), c_block(-c_im)
    cblk = jnp.stack([c_re_blk.reshape(2, n_t, LANES, -1), c_im_blk.reshape(2, n_t, LANES, -1)], axis=2)
    cblk = cblk.reshape(2, 2 * S5_HALF_STATE, S5_HALF_GROUPS * S5_GROUP).astype(BF16)

    def tiled(tab):
        rows = tab.shape[1]
        return jnp.transpose(tab.reshape(2, rows, 2 * n_t, LANES), (1, 2, 0, 3)).reshape(rows, 4 * S5_HALF_STATE)

    r = np.arange(max_tile)
    tri = ((r[:, None] // S5_SUB) == (r[None, :] // S5_SUB)) & (r[None, :] <= r[:, None])
    return dict(bblk=bblk, cblk=cblk, tri=jnp.asarray(tri, BF16), neg=tiled(neg), pos=tiled(pos),
                lam1=tiled(lam1[:, None, :]),
                d=d_skip.reshape(1, -1), w_glu=w_glu.astype(BF16), b_glu=b_glu.reshape(1, -1))


def _pack_state(s_re, s_im):
    b = s_re.shape[0]
    st = jnp.stack([s_re.reshape(b, -1, LANES), s_im.reshape(b, -1, LANES)], axis=2)
    return st.reshape(b, 1, 4 * S5_HALF_STATE)


def _unpack_state(st):
    b = st.shape[0]
    st = st.reshape(b, -1, 2, LANES)
    return (st[:, :, 0].reshape(b, S5_GROUPS, S5_STATE), st[:, :, 1].reshape(b, S5_GROUPS, S5_STATE))


def _ab_branches(x, pos, past, wa, ws, *, tm_in, tm_s5, tq, tk):
    b, t, _ = x.shape
    x2d = x.reshape(b * t, D_MODEL)
    tabs = _rope_tables(pos, tq if past is None else None)
    lat, kr, qp, kp, vt, gm, u, gs = _ab_in(x2d, tabs, wa, seq_len=t, tm=tm_in)
    r3 = lambda a: a.reshape(b, t, a.shape[-1])
    if past is None:
        mla = _attn_prompt(r3(qp), r3(kp), vt, r3(gm), tq=tq)
        s0 = jnp.zeros((b, 1, 4 * S5_HALF_STATE), F32)
    else:
        past_lat, past_kr, past_re, past_im = past
        vt_b = jnp.transpose(vt.reshape(MLA_WIDTH, b, t), (1, 0, 2))
        mla = _attn_sample(r3(qp), past_lat, past_kr, r3(kp), vt_b, r3(gm), wa, tk=tk)
        s0 = _pack_state(past_re, past_im)
    s5, sfin = _s5(r3(u), r3(gs), s0, ws, tm=tm_s5)
    fin_re, fin_im = _unpack_state(sfin)
    return mla, s5, r3(lat), r3(kr), fin_re, fin_im


def _conv_taps(conv_w):
    by_lag = jnp.pad(conv_w[::-1], ((0, CONV_HALO - conv_w.shape[0]), (0, 0))).astype(BF16)
    return jnp.broadcast_to(by_lag[:, None, :], (CONV_HALO, CONV_PAD, conv_w.shape[1]))


def _conv(x, mla, s5m, past, wc, *, tm):
    b = x.shape[0]
    if past is None:
        past32 = jnp.zeros((b, CONV_HALO, D_MODEL), F32)
    else:
        past32 = jnp.pad(past, ((0, 0), (CONV_HALO - (CONV_WIDTH - 1), 0), (0, 0)))
    y, newc = _conv_layer(x, mla, s5m, past32, wc, tm=tm)
    return y, newc[:, CONV_HALO - (CONV_WIDTH - 1):]


def kernel(x_prompt, x_sample, cache_mla_latent, cache_mla_krope, state_s5_re, state_s5_im, state_conv, norm_ab, w_in_ab, g_q_lat, w_uq, g_kv_lat, w_uk, w_uv, g_q_nope, g_q_rope, g_k_nope, g_k_rope, s5_lam_re, s5_lam_im, s5_log_dt, s5_b_re, s5_b_im, s5_c_re, s5_c_im, s5_d, s5_w_glu, s5_b_glu, w_out_ab, norm_c, w_in_c, conv_w, conv_b, ln_g, ln_b, w_out_c):
    t_p = x_prompt.shape[1]
    t_s = x_sample.shape[1]
    past_len = cache_mla_latent.shape[2]
    pos_p = jnp.arange(t_p, dtype=jnp.int32)
    pos_s = past_len + jnp.arange(t_s, dtype=jnp.int32)

    i = 0
    wa = _prep_ab_weights(norm_ab[i], w_in_ab[i], g_q_lat[i], w_uq[i], g_kv_lat[i], w_uk[i], w_uv[i],
                          g_q_nope[i], g_q_rope[i], g_k_nope[i], g_k_rope[i], w_out_ab[i])
    ws = _prep_s5_weights(s5_lam_re[i], s5_lam_im[i], s5_log_dt[i], s5_b_re[i], s5_b_im[i], s5_c_re[i],
                          s5_c_im[i], s5_d[i], s5_w_glu[i], s5_b_glu[i], max_tile=256)
    mla_p, s5_p, lat_p, kr_p, re_p, im_p = _ab_branches(
        x_prompt, pos_p, None, wa, ws, tm_in=256, tm_s5=256, tq=512, tk=512)
    mla_s, s5_s, lat_s, kr_s, re_s, im_s = _ab_branches(
        x_sample, pos_s, (cache_mla_latent[i], cache_mla_krope[i], state_s5_re[i], state_s5_im[i]), wa, ws,
        tm_in=256, tm_s5=t_s, tq=512, tk=1024)

    row = lambda a: a.reshape(1, -1)
    cw = _conv_taps(conv_w[i])
    wc = dict(w_out_a=wa['w_out_a'], w_out_s=wa['w_out_s'],
              norm_g=row(norm_c[i]), w_in=w_in_c[i].astype(BF16), conv_w=cw, conv_b=row(conv_b[i]),
              ln_g=row(ln_g[i]), ln_b=row(ln_b[i]), w_out=w_out_c[i].astype(BF16))
    yp, conv_p = _conv(x_prompt, mla_p, s5_p, None, wc, tm=256)
    ys, conv_s = _conv(x_sample, mla_s, s5_s, state_conv[i], wc, tm=t_s)

    st = lambda a: a[None]
    return (yp, ys, st(lat_p), st(kr_p), st(re_p), st(im_p), st(conv_p),
            st(lat_s), st(kr_s), st(re_s), st(im_s), st(conv_s))
```

```python
import functools
import math

import numpy as np
import jax
import jax.numpy as jnp
from jax import lax
from jax.experimental import pallas as pl
from jax.experimental.pallas import tpu as pltpu

F32 = jnp.float32
BF16 = jnp.bfloat16

D_MODEL = 1024
CHUNK = 64
MLA_HEADS = 8
Q_LORA = 384
KV_LORA = 256
NOPE_DIM = 64
ROPE_DIM = 32
V_DIM = 64
MLA_WIDTH = MLA_HEADS * V_DIM
ROPE_BASE = 10000.0
ATTN_SCALE = (NOPE_DIM + ROPE_DIM) ** -0.5
S5_WIDTH = 512
S5_GROUP = 16
S5_GROUPS = S5_WIDTH // S5_GROUP
S5_STATE = 64
CONV_WIDTH = 31
EPS = 1e-6
NEG_INF = -1e30
LOG2E = math.log2(math.e)
MASK_LANE0 = NOPE_DIM + ROPE_DIM

LANES = 128
SUBLANES = 8
HEAD_PAD = MLA_HEADS * LANES
S5_HALF_GROUPS = S5_GROUPS // 2
S5_HALF_STATE = S5_HALF_GROUPS * S5_STATE
S5_SUB = 32
CONV_HALO = 32
CONV_PAD = 16
V7X_VMEM_BYTES = 64 * 1024 * 1024


def _vmem_limit(block_bytes, scratch_bytes=0, temp_bytes=0):
    est = 2 * block_bytes + scratch_bytes + temp_bytes
    return int(min(max(est, 16 * 1024 * 1024), V7X_VMEM_BYTES - 8 * 1024 * 1024))


def _nbytes(shape, dtype):
    return int(np.prod(shape)) * jnp.dtype(dtype).itemsize


def _rms(x, g):
    return x * lax.rsqrt(jnp.mean(x * x, axis=-1, keepdims=True) + EPS) * g


def _silu(x):
    return x * jax.nn.sigmoid(x)


def _dot(a, b):
    return jnp.dot(a, b, preferred_element_type=F32)


def _dot_nt(a, b):
    return lax.dot_general(a, b, (((1,), (1,)), ((), ())), preferred_element_type=F32)


def _ab_in_kernel(x_ref, ng_ref, win_ref, gql_ref, wuq_ref, gkv_ref, wuk_ref, wuv_ref, e2_ref,
                  cq_ref, sq_ref, ck_ref, sk_ref, qm_ref, km_ref, gq_ref, gqs_ref, gk_ref, gks_ref, gkn_ref,
                  lat_ref, kr_ref, qp_ref, kp_ref, vt_ref, gm_ref, u_ref, gs_ref):
    x = x_ref[...]
    h = _rms(x, ng_ref[...]).astype(BF16)

    def proj(lo, hi):
        return _dot(h, win_ref[:, lo:hi])

    q_lat = proj(0, Q_LORA)
    c_kv = proj(Q_LORA, Q_LORA + KV_LORA)
    krg = proj(640, 768)
    gm_ref[...] = proj(768, 1280)
    u_ref[...] = proj(1280, 1792)
    gs_ref[...] = proj(1792, 2304)

    c_n = _rms(c_kv, gkv_ref[...])
    lat_ref[...] = c_n
    cb = c_n.astype(BF16)

    lane = lax.broadcasted_iota(jnp.int32, (1, LANES), 1)
    ms = jnp.sum(jnp.where(lane < ROPE_DIM, krg * krg, 0.0), axis=-1, keepdims=True) * (1.0 / ROPE_DIM)
    kr = lax.rsqrt(ms + EPS) * (krg * (ck_ref[...] * gk_ref[...])
                                + pltpu.roll(krg, LANES - ROPE_DIM, 1) * (sk_ref[...] * gks_ref[...]))
    kr_ref[...] = kr[:, :ROPE_DIM]
    kr_mask = pltpu.roll(kr, NOPE_DIM, 1) + km_ref[...]

    qn = _rms(q_lat, gql_ref[...]).astype(BF16)
    qa_tab = cq_ref[...] * (gq_ref[...] * (ATTN_SCALE * LOG2E))
    qb_tab = sq_ref[...] * (gqs_ref[...] * (ATTN_SCALE * LOG2E))
    n_pairs = MLA_HEADS // 2
    pair_cols = [slice(2 * LANES * p, 2 * LANES * (p + 1)) for p in range(n_pairs)]
    qas = [_dot(qn, wuq_ref[:, cols]) for cols in pair_cols]
    kas = [_dot(cb, wuk_ref[:, cols]) for cols in pair_cols]
    vt_ref[...] = _dot_nt(wuv_ref[...], cb).astype(BF16)
    q_ms = [_dot((qa * qa).astype(BF16), e2_ref[...]) for qa in qas]
    k_ms = [_dot((ka * ka).astype(BF16), e2_ref[...]) for ka in kas]
    for p in range(n_pairs):
        lo = 2 * LANES * p
        qs = qas[p] * lax.rsqrt(q_ms[p] + EPS)
        ks = kas[p] * lax.rsqrt(k_ms[p] + EPS)
        for j in range(2):
            sl = slice(LANES * j, LANES * (j + 1))
            s = qs[:, sl]
            qp_ref[:, lo + LANES * j:lo + LANES * (j + 1)] = (
                s * qa_tab + pltpu.roll(s, LANES - ROPE_DIM, 1) * qb_tab + qm_ref[...]).astype(BF16)
            kp_ref[:, lo + LANES * j:lo + LANES * (j + 1)] = (
                ks[:, sl] * gkn_ref[...] + kr_mask).astype(BF16)


def _ab_in(x2d, pos_tabs, w, *, seq_len, tm):
    n_tok = x2d.shape[0]
    if tm > seq_len:
        pos_tabs = tuple(jnp.tile(p, (tm // seq_len, 1)) for p in pos_tabs)
    n_pos = max(seq_len // tm, 1)
    tok = lambda i: (i, 0)
    pos = lambda i: (i % n_pos, 0)
    const = lambda i: (0, 0)

    def full(a):
        return pl.BlockSpec(a.shape, const, pipeline_mode=pl.Buffered(1))

    ins = [x2d, w['norm_g'], w['w_in'], w['g_q_lat'], w['w_uq'], w['g_kv'], w['w_uk'], w['w_uvt'], w['e2'],
           *pos_tabs, w['gq'], w['gqs'], w['gk'], w['gks'], w['gkn']]
    n_tabs = len(pos_tabs)
    in_specs = [pl.BlockSpec((tm, D_MODEL), tok)] + [full(a) for a in ins[1:9]] \
        + [pl.BlockSpec((tm, LANES), pos)] * n_tabs + [full(a) for a in ins[9 + n_tabs:]]
    outs = [((n_tok, KV_LORA), F32), ((n_tok, ROPE_DIM), F32), ((n_tok, HEAD_PAD), BF16),
            ((n_tok, HEAD_PAD), BF16), ((MLA_WIDTH, n_tok), BF16), ((n_tok, MLA_WIDTH), F32),
            ((n_tok, S5_WIDTH), F32), ((n_tok, S5_WIDTH), F32)]
    vt_index = 4
    out_specs = [pl.BlockSpec((MLA_WIDTH, tm), lambda i: (0, i)) if n == vt_index
                 else pl.BlockSpec((tm, s[1]), tok) for n, (s, _) in enumerate(outs)]
    out_shape = [jax.ShapeDtypeStruct(s, d) for s, d in outs]
    blk = sum(_nbytes(a.shape, a.dtype) for a in ins[1:9]) + _nbytes((tm, D_MODEL), F32) \
        + n_tabs * _nbytes((tm, LANES), F32) + sum(_nbytes(s, d) for s, d in outs) * tm // n_tok
    return pl.pallas_call(
        _ab_in_kernel, grid=(n_tok // tm,), in_specs=in_specs, out_specs=out_specs, out_shape=out_shape,
        compiler_params=pltpu.CompilerParams(
            dimension_semantics=("arbitrary",),
            vmem_limit_bytes=_vmem_limit(blk, temp_bytes=8 * _nbytes((tm, D_MODEL), F32))),
        name="ab_in")(*ins)


def _ones_row(n, rows=V_DIM):
    return (lax.broadcasted_iota(jnp.int32, (rows, n), 0) == 0).astype(BF16)


VAUG_ROWS = V_DIM + CONV_PAD


def _attn_prompt_kernel(qi_ref, kj_ref, q_ref, k_ref, vt_ref, g_ref, o_ref, vaug_sc, s_sc, acc_sc, *, tq,
                        seq_len):
    n_steps = qi_ref.shape[0]
    ones_row = _ones_row(seq_len, VAUG_ROWS - V_DIM)
    for hh in range(2):
        vaug_sc[hh, 0:V_DIM, :] = vt_ref[hh * V_DIM:(hh + 1) * V_DIM, :]
        vaug_sc[hh, V_DIM:VAUG_ROWS, :] = ones_row
    lane = lax.broadcasted_iota(jnp.int32, (1, LANES), 1)
    mask_lane_off = jnp.logical_or(lane < MASK_LANE0, lane >= MASK_LANE0 + tq // CHUNK)

    def produce(n, slot):
        qi = qi_ref[n]
        kj = kj_ref[n]
        r0 = pl.multiple_of(qi * tq, tq)
        k0 = pl.multiple_of(kj * tq, tq)
        q_keep = jnp.logical_or(mask_lane_off, kj == qi)
        bms = []
        for hh in range(2):
            hs = slice(hh * LANES, (hh + 1) * LANES)
            q = q_ref[0, pl.ds(r0, tq), hs]
            q = jnp.where(q_keep, q, jnp.zeros_like(q))
            s = _dot_nt(k_ref[0, pl.ds(k0, tq), hs], q)
            s_sc[slot, hh] = s
            bms.append(jnp.max(s, axis=0, keepdims=True))
        return tuple(bms)

    def consume(n, slot, bms, state):
        qi = qi_ref[n]
        kj = kj_ref[n]
        k0 = pl.multiple_of(kj * tq, tq)
        out = []
        for hh in range(2):
            m, acc = state[hh]
            m = jnp.where(kj == 0, -jnp.inf, m)
            m_new = jnp.maximum(m, bms[hh])
            p = jnp.exp2(s_sc[slot, hh] - m_new)
            alpha = jnp.exp2(m - m_new)
            acc = alpha * acc + _dot(vaug_sc[hh, :, pl.ds(k0, tq)], p.astype(BF16))
            acc_sc[qi, hh] = acc
            out.append((m_new, acc))
        return tuple(out)

    unroll = 4

    def multi_step(t, carry):
        bms, state = carry
        for u in range(unroll):
            n = unroll * t + u
            next_bms = produce(n + 1, (u + 1) % 2)
            state = consume(n, u % 2, bms, state)
            bms = next_bms
        return bms, state

    state = tuple((jnp.full((1, tq), -jnp.inf, F32), jnp.zeros((VAUG_ROWS, tq), F32)) for _ in range(2))
    n_loops = (n_steps - 1) // unroll
    bms, state = lax.fori_loop(0, n_loops, multi_step, (produce(0, 0), state))
    for n in range(unroll * n_loops, n_steps):
        next_bms = produce(n + 1, (n + 1) % 2) if n + 1 < n_steps else None
        state = consume(n, n % 2, bms, state)
        bms = next_bms

    def finalize(qi, carry):
        r0 = pl.multiple_of(qi * tq, tq)
        vals = []
        for hh in range(2):
            acc = acc_sc[qi, hh]
            vals.append(acc[0:V_DIM, :] * (1.0 / acc[V_DIM:V_DIM + 1, :]))
        o = jnp.concatenate(vals, axis=0).T
        o_ref[0, pl.ds(r0, tq), :] = (o * _silu(g_ref[0, pl.ds(r0, tq), :])).astype(BF16)
        return carry

    lax.fori_loop(0, seq_len // tq, finalize, 0)


def _attn_prompt(qp, kp, vt, gate, *, tq):
    b, t, _ = qp.shape
    nq = t // tq
    steps = [(qi, kj) for qi in range(nq) for kj in range(qi + 1)]
    qi_tab = jnp.asarray([s[0] for s in steps], jnp.int32)
    kj_tab = jnp.asarray([s[1] for s in steps], jnp.int32)
    pair = lambda i, j, qt, kt: (i, 0, j)
    blk = 2 * _nbytes((t, 2 * LANES), BF16) + _nbytes((LANES, t), BF16) + _nbytes((t, LANES), F32) \
        + _nbytes((t, LANES), BF16)
    scratch = _nbytes((2, VAUG_ROWS, t), BF16) + _nbytes((2, 2, tq, tq), F32) + _nbytes((nq, 2, VAUG_ROWS, tq), F32)
    return pl.pallas_call(
        functools.partial(_attn_prompt_kernel, tq=tq, seq_len=t),
        grid_spec=pltpu.PrefetchScalarGridSpec(
            num_scalar_prefetch=2,
            grid=(b, MLA_HEADS // 2),
            in_specs=[pl.BlockSpec((1, t, 2 * LANES), pair), pl.BlockSpec((1, t, 2 * LANES), pair),
                      pl.BlockSpec((LANES, t), lambda i, j, qt, kt: (j, i)), pl.BlockSpec((1, t, LANES), pair)],
            out_specs=pl.BlockSpec((1, t, LANES), pair),
            scratch_shapes=[pltpu.VMEM((2, VAUG_ROWS, t), BF16), pltpu.VMEM((2, 2, tq, tq), F32),
                            pltpu.VMEM((nq, 2, VAUG_ROWS, tq), F32)]),
        out_shape=jax.ShapeDtypeStruct((b, t, MLA_WIDTH), BF16),
        compiler_params=pltpu.CompilerParams(
            dimension_semantics=("arbitrary", "arbitrary"),
            vmem_limit_bytes=_vmem_limit(blk, scratch, temp_bytes=12 * _nbytes((tq, tq), F32))),
        name="attn_prompt")(qi_tab, kj_tab, qp, kp, vt, gate)


def _attn_sample_kernel(q_ref, lat_ref, kr_ref, kpn_ref, vtn_ref, g_ref, wuk_ref, wuvt_ref, e2_ref, gkn2_ref,
                        place_ref, o_ref, qbd_sc, m_sc, acc_sc, *, n_past_blocks):
    j = pl.program_id(1)
    n_pairs = MLA_HEADS // 2
    t = q_ref.shape[1]
    lane = lax.broadcasted_iota(jnp.int32, (1, LANES), 1)

    @pl.when(j == 0)
    def _():
        m_sc[...] = jnp.full(m_sc.shape, -jnp.inf, F32)
        acc_sc[...] = jnp.zeros(acc_sc.shape, F32)
        q = q_ref[0].astype(F32)
        q_t = jnp.concatenate([q, jnp.zeros((LANES - t, HEAD_PAD), F32)], axis=0).T
        for p in range(n_pairs):
            top = q_t[2 * LANES * p:2 * LANES * p + LANES, :]
            bot = pltpu.roll(q_t[2 * LANES * p + LANES:2 * LANES * (p + 1), :], t, 1)
            qbd_sc[p] = jnp.concatenate([top, bot], axis=0).astype(BF16)

    def update_all(kp_pairs, vt_pairs):
        n = kp_pairs[0].shape[0]
        ones_row = _ones_row(n)
        ss = [_dot(kp_pairs[p], qbd_sc[p]) for p in range(n_pairs)]
        for p in range(n_pairs):
            m_old = m_sc[p]
            m_new = jnp.maximum(m_old, jnp.max(ss[p], axis=0, keepdims=True))
            pr = jnp.exp2(ss[p] - m_new)
            alpha = jnp.exp2(m_old - m_new)
            vt_pair = vt_pairs[p]
            vaug = jnp.concatenate([vt_pair[0:V_DIM, :], ones_row, vt_pair[V_DIM:2 * V_DIM, :], ones_row], axis=0)
            acc_sc[p] = alpha * acc_sc[p] + _dot(vaug, pr.astype(BF16))
            m_sc[p] = m_new

    @pl.when(j < n_past_blocks)
    def _():
        latb = lat_ref[0].astype(BF16)
        kr128 = _dot(kr_ref[0].astype(BF16), place_ref[...])
        kr256 = jnp.concatenate([kr128, kr128], axis=1)
        vt_all = _dot_nt(wuvt_ref[...], latb).astype(BF16)
        kas = [_dot(latb, wuk_ref[:, 2 * LANES * p:2 * LANES * (p + 1)]) for p in range(n_pairs)]
        mss = [_dot((ka * ka).astype(BF16), e2_ref[...]) for ka in kas]
        kps = [(ka * lax.rsqrt(ms + EPS) * gkn2_ref[...] + kr256).astype(BF16) for ka, ms in zip(kas, mss)]
        update_all(kps, [vt_all[LANES * p:LANES * (p + 1), :] for p in range(n_pairs)])

    @pl.when(j == n_past_blocks)
    def _():
        update_all([kpn_ref[0, :, 2 * LANES * p:2 * LANES * (p + 1)] for p in range(n_pairs)],
                   [vtn_ref[0, LANES * p:LANES * (p + 1), :] for p in range(n_pairs)])
        for p in range(n_pairs):
            acc_t = acc_sc[p].T
            a0 = acc_t[0:t, 0:LANES]
            a1 = acc_t[t:2 * t, LANES:2 * LANES]
            o0 = a0 * (1.0 / a0[:, V_DIM:V_DIM + 1])
            o1 = a1 * (1.0 / a1[:, V_DIM:V_DIM + 1])
            o = jnp.where(lane < V_DIM, o0, pltpu.roll(o1, V_DIM, 1))
            g = g_ref[0, :, p * LANES:(p + 1) * LANES]
            o_ref[0, :, p * LANES:(p + 1) * LANES] = (o * _silu(g)).astype(BF16)


def _attn_sample(qp, past_lat, past_kr, kp_new, vt_new, gate, w, *, tk):
    b, t, _ = qp.shape
    assert 2 * t == LANES, "two heads' queries share one 128-lane group"
    n_past_blocks = past_lat.shape[1] // tk
    n_pairs = MLA_HEADS // 2
    cur = lambda i, j: (i, 0, 0)
    past = lambda i, j: (i, jnp.minimum(j, n_past_blocks - 1), 0)
    const = lambda i, j: (0, 0)
    consts = [w['w_uk'], w['w_uvt'], w['e2'], w['gkn2'], w['place']]
    blk = _nbytes((tk, KV_LORA), F32) + _nbytes((tk, LANES), F32) + 2 * _nbytes((t, HEAD_PAD), BF16) \
        + _nbytes((MLA_WIDTH, LANES), BF16) + 2 * _nbytes((t, MLA_WIDTH), F32) \
        + sum(_nbytes(a.shape, a.dtype) for a in consts)
    scratch = _nbytes((n_pairs, 2 * LANES, LANES), BF16) + _nbytes((n_pairs, SUBLANES, LANES), F32) \
        + _nbytes((n_pairs, 2 * LANES, LANES), F32)
    return pl.pallas_call(
        functools.partial(_attn_sample_kernel, n_past_blocks=n_past_blocks),
        grid=(b, n_past_blocks + 1),
        in_specs=[pl.BlockSpec((1, t, HEAD_PAD), cur),
                  pl.BlockSpec((1, tk, KV_LORA), past),
                  pl.BlockSpec((1, tk, ROPE_DIM), past),
                  pl.BlockSpec((1, t, HEAD_PAD), cur),
                  pl.BlockSpec((1, MLA_WIDTH, t), cur),
                  pl.BlockSpec((1, t, MLA_WIDTH), cur)] + [pl.BlockSpec(a.shape, const) for a in consts],
        out_specs=pl.BlockSpec((1, t, MLA_WIDTH), cur),
        out_shape=jax.ShapeDtypeStruct((b, t, MLA_WIDTH), BF16),
        scratch_shapes=[pltpu.VMEM((n_pairs, 2 * LANES, LANES), BF16),
                        pltpu.VMEM((n_pairs, 1, LANES), F32),
                        pltpu.VMEM((n_pairs, 2 * LANES, LANES), F32)],
        compiler_params=pltpu.CompilerParams(
            dimension_semantics=("arbitrary", "arbitrary"),
            vmem_limit_bytes=_vmem_limit(blk, scratch, temp_bytes=12 * _nbytes((tk, 2 * LANES), F32))),
        name="attn_sample")(qp, past_lat, past_kr, kp_new, vt_new, gate, *consts)


def _s5_prep_kernel(lr_ref, li_ref, ldt_ref, br_ref, bi_ref, pos_ref, neg_ref, lam1_ref, bbar_ref, *, sub):
    lr = lr_ref[...]
    li = li_ref[...]
    dt = jnp.exp(ldt_ref[...])
    ar = lr * dt
    ai = li * dt
    k = lax.broadcasted_iota(jnp.int32, (sub, 1), 0).astype(F32)
    mag_p = jnp.exp(k * ar)
    mag_n = jnp.exp(-k * ar)
    ang = k * ai
    c = jnp.cos(ang)
    s = jnp.sin(ang)
    pos_ref[0] = mag_p * c
    pos_ref[1] = mag_p * s
    neg_ref[0] = mag_n * c
    neg_ref[1] = -mag_n * s
    mag1 = jnp.exp(ar)
    l1r = mag1 * jnp.cos(ai)
    l1i = mag1 * jnp.sin(ai)
    lam1_ref[0:1, :] = l1r
    lam1_ref[1:2, :] = l1i
    inv = 1.0 / (lr * lr + li * li)
    nr = l1r - 1.0
    cr = (nr * lr + l1i * li) * inv
    ci = (l1i * lr - nr * li) * inv
    br = br_ref[...]
    bi = bi_ref[...]
    bbar_ref[0] = cr * br - ci * bi
    bbar_ref[1] = cr * bi + ci * br


def _s5_prep(lam_re, lam_im, log_dt, b_re, b_im, *, sub):
    gp = S5_GROUPS * S5_STATE
    lr = lam_re.reshape(1, gp)
    li = lam_im.reshape(1, gp)
    ldt = jnp.broadcast_to(log_dt[:, None], (S5_GROUPS, S5_STATE)).reshape(1, gp)
    br = jnp.transpose(b_re, (2, 0, 1)).reshape(S5_GROUP, gp)
    bi = jnp.transpose(b_im, (2, 0, 1)).reshape(S5_GROUP, gp)
    return pl.pallas_call(
        functools.partial(_s5_prep_kernel, sub=sub),
        out_shape=[jax.ShapeDtypeStruct((2, sub, gp), F32), jax.ShapeDtypeStruct((2, sub, gp), F32),
                   jax.ShapeDtypeStruct((2, gp), F32), jax.ShapeDtypeStruct((2, S5_GROUP, gp), F32)],
        name="s5_prep")(lr, li, ldt, br, bi)


def _s5_kernel(u_ref, gs_ref, s0_ref, bblk_ref, cblk_ref, tri_ref, neg_ref, pos_ref, lam1_ref, d_ref,
               wglu_ref, bglu_ref, o_ref, sfin_ref, carry_sc, *, tm, sub):
    ti = pl.program_id(1)
    tile = 2 * LANES
    tiles_per_half = 2 * S5_HALF_STATE // tile

    @pl.when(ti == 0)
    def _():
        carry_sc[...] = s0_ref[0]

    u = u_ref[0]
    ub = u.astype(BF16)
    n_tiles = 2 * tiles_per_half

    def lanes_of(t):
        lo = t * tile
        return slice(lo, lo + LANES), slice(lo + LANES, lo + tile)

    def stage_a(t):
        hf, q = divmod(t, tiles_per_half)
        re, im = lanes_of(t)
        bu = _dot(ub[:, hf * 256:(hf + 1) * 256], bblk_ref[hf, :, q * tile:(q + 1) * tile])
        nre, nim = neg_ref[:, re], neg_ref[:, im]
        xs = []
        for c in range(tm // sub):
            rs = slice(c * sub, (c + 1) * sub)
            bre, bim = bu[rs, :LANES], bu[rs, LANES:]
            xs.append(jnp.concatenate([nre * bre - nim * bim, nre * bim + nim * bre], axis=-1).astype(BF16))
        return jnp.concatenate(xs, axis=0)

    def stage_b(t, x):
        re, im = lanes_of(t)
        cs = _dot(tri_ref[...], x)
        cre, cim = carry_sc[:, re], carry_sc[:, im]
        l1r, l1i = lam1_ref[:, re], lam1_ref[:, im]
        pre, pim = pos_ref[:, re], pos_ref[:, im]
        ss = []
        for c in range(tm // sub):
            rs = slice(c * sub, (c + 1) * sub)
            tre = cs[rs, :LANES] + (l1r * cre - l1i * cim)
            tim = cs[rs, LANES:] + (l1r * cim + l1i * cre)
            sre = pre * tre - pim * tim
            sim = pre * tim + pim * tre
            ss.append(jnp.concatenate([sre, sim], axis=-1).astype(BF16))
            cre = sre[sub - 1:sub, :]
            cim = sim[sub - 1:sub, :]
        carry_sc[:, re] = cre
        carry_sc[:, im] = cim
        return jnp.concatenate(ss, axis=0)

    def stage_c(t, s):
        hf, q = divmod(t, tiles_per_half)
        return _dot(s, cblk_ref[hf, q * tile:(q + 1) * tile, :])

    xs_, ss_ = {}, {}
    accs = [None, None]
    skew = 1
    for step in range(n_tiles + 2 * skew):
        if step < n_tiles:
            xs_[step] = stage_a(step)
        if skew <= step < n_tiles + skew:
            ss_[step - skew] = stage_b(step - skew, xs_.pop(step - skew))
        if step >= 2 * skew:
            t = step - 2 * skew
            part = stage_c(t, ss_.pop(t))
            hf = t // tiles_per_half
            accs[hf] = part if accs[hf] is None else accs[hf] + part
    y = jnp.concatenate(accs, axis=-1) + d_ref[...] * u
    z = jax.nn.gelu(y)
    z = z * jax.nn.sigmoid(_dot(z.astype(BF16), wglu_ref[...]) + bglu_ref[...])
    o_ref[0] = (z * _silu(gs_ref[0])).astype(BF16)

    @pl.when(ti == pl.num_programs(1) - 1)
    def _():
        sfin_ref[0] = carry_sc[...]


def _s5(u, gate, s0, w, *, tm):
    b, t, _ = u.shape
    sub = S5_SUB
    tokb = lambda i, j: (i, j, 0)
    perb = lambda i, j: (i, 0, 0)
    c2 = lambda i, j: (0, 0)
    c3 = lambda i, j: (0, 0, 0)
    tri = w['tri'][:tm, :tm]
    consts = [w['bblk'], w['cblk'], tri, w['neg'], w['pos'], w['lam1'], w['d'], w['w_glu'], w['b_glu']]
    blk = 2 * _nbytes((tm, S5_WIDTH), F32) + _nbytes((tm, S5_WIDTH), BF16) + 2 * _nbytes((1, 4 * S5_HALF_STATE), F32) \
        + sum(_nbytes(a.shape, a.dtype) for a in consts)
    scratch = _nbytes((SUBLANES, 4 * S5_HALF_STATE), F32)
    return pl.pallas_call(
        functools.partial(_s5_kernel, tm=tm, sub=sub),
        grid=(b, t // tm),
        in_specs=[pl.BlockSpec((1, tm, S5_WIDTH), tokb), pl.BlockSpec((1, tm, S5_WIDTH), tokb),
                  pl.BlockSpec((1, 1, 4 * S5_HALF_STATE), perb)]
                 + [pl.BlockSpec(a.shape, c3 if a.ndim == 3 else c2) for a in consts],
        out_specs=[pl.BlockSpec((1, tm, S5_WIDTH), tokb), pl.BlockSpec((1, 1, 4 * S5_HALF_STATE), perb)],
        out_shape=[jax.ShapeDtypeStruct((b, t, S5_WIDTH), BF16),
                   jax.ShapeDtypeStruct((b, 1, 4 * S5_HALF_STATE), F32)],
        scratch_shapes=[pltpu.VMEM((1, 4 * S5_HALF_STATE), F32)],
        compiler_params=pltpu.CompilerParams(
            dimension_semantics=("arbitrary", "arbitrary"),
            vmem_limit_bytes=_vmem_limit(blk, scratch, temp_bytes=16 * _nbytes((tm, 2 * LANES), F32))),
        name="s5_scan")(u, gate, s0, *consts)


def _conv_kernel(x_ref, xn_ref, ma_ref, man_ref, ms_ref, msn_ref, past_ref, woa_ref, wos_ref, ng_ref, win_ref,
                 cw_ref, cb_ref, lng_ref, lnb_ref, wout_ref,
                 shift_ref, y_ref, newc_ref, vext_sc, vbe_sc, vbo_sc, stage_sc, hstage_sc, xstage_sc, yc_sc, *,
                 tm, pipelined):
    ti = pl.program_id(1)
    base = CONV_PAD + CONV_HALO

    def stage_tile(x0, mla, s5m):
        x1 = _dot(mla, woa_ref[...]) + _dot(s5m, wos_ref[...]) + x0
        xstage_sc[...] = x1
        h = _rms(x1, ng_ref[...]).astype(BF16)
        hstage_sc[...] = h
        stage_sc[...] = _dot(h, win_ref[:, 0:D_MODEL]) * jax.nn.sigmoid(_dot(h, win_ref[:, D_MODEL:2 * D_MODEL]))

    @pl.when(ti == 0)
    def _():
        vext_sc[0:CONV_PAD, :] = jnp.zeros((CONV_PAD, D_MODEL), F32)
        vext_sc[CONV_PAD:base, :] = past_ref[0]
        if pipelined:
            stage_tile(x_ref[0], ma_ref[0], ms_ref[0])

    if not pipelined:
        stage_tile(x_ref[0], ma_ref[0], ms_ref[0])
    x = xstage_sc[...]
    h = hstage_sc[...]
    vext_sc[base:base + tm, :] = stage_sc[...]
    if pipelined:
        stage_tile(xn_ref[0], man_ref[0], msn_ref[0])

    vbe_sc[...] = vext_sc[...].astype(BF16)
    vbo_sc[...] = vext_sc[SUBLANES:SUBLANES + tm + CONV_HALO, :].astype(BF16)
    rows = min(tm, 128)
    width = 2 * LANES
    win = rows + CONV_PAD
    for rb in range(tm // rows):
        r0 = rb * rows
        for lb in range(D_MODEL // width):
            ls = slice(lb * width, (lb + 1) * width)
            zs = []
            for r in range(SUBLANES):
                z = None
                for a in range(CONV_HALO // SUBLANES):
                    start = r0 + base - CONV_PAD - SUBLANES * a
                    if a % 2 == 0:
                        src = vbe_sc[start:start + win, ls]
                    else:
                        src = vbo_sc[start - SUBLANES:start - SUBLANES + win, ls]
                    term = src.reshape(win // CONV_PAD, CONV_PAD, width) * cw_ref[SUBLANES * a + r, :, ls][None]
                    z = term if z is None else z + term
                zs.append(z)
            zcat = jnp.concatenate(zs, axis=0).reshape(SUBLANES * win, width)
            yc_sc[r0:r0 + rows, ls] = _dot(shift_ref[...], zcat)

    yc = yc_sc[...] + cb_ref[...]
    mu = jnp.mean(yc, axis=-1, keepdims=True)
    xc = yc - mu
    var = jnp.mean(xc * xc, axis=-1, keepdims=True)
    yn = xc * lax.rsqrt(var + EPS) * lng_ref[...] + lnb_ref[...]
    gate = _dot(h, win_ref[:, 2 * D_MODEL:3 * D_MODEL])
    mixed = (_silu(yn) * _silu(gate)).astype(BF16)
    y_ref[0] = _dot(mixed, wout_ref[...]) + x

    tail = vext_sc[CONV_PAD + tm:base + tm, :]
    vext_sc[CONV_PAD:base, :] = tail
    newc_ref[0] = tail


def _conv_layer(x, mla, s5m, past32, w, *, tm):
    b, t, _ = x.shape
    nt = t // tm
    tokb = lambda i, j: (i, j, 0)
    nxtb = lambda i, j: (i, jnp.minimum(j + 1, nt - 1), 0)
    perb = lambda i, j: (i, 0, 0)
    const = lambda i, j: (0, 0)
    rows = min(tm, 128)
    win = rows + CONV_PAD
    shift = np.zeros((rows, SUBLANES * win), np.float32)
    for r in range(SUBLANES):
        shift[np.arange(rows), r * win + np.arange(rows) + CONV_PAD - r] = 1.0
    consts = [w['w_out_a'], w['w_out_s'], w['norm_g'], w['w_in'], w['conv_w'], w['conv_b'], w['ln_g'], w['ln_b'],
              w['w_out'], jnp.asarray(shift, BF16)]
    blk = 3 * _nbytes((tm, D_MODEL), F32) + 4 * _nbytes((tm, MLA_WIDTH), BF16) \
        + 2 * _nbytes((CONV_HALO, D_MODEL), F32) + sum(_nbytes(a.shape, a.dtype) for a in consts)
    ext = tm + CONV_PAD + CONV_HALO
    scratch = _nbytes((ext, D_MODEL), F32) + _nbytes((ext, D_MODEL), BF16) + _nbytes((tm + CONV_HALO, D_MODEL), BF16) \
        + 3 * _nbytes((tm, D_MODEL), F32) + _nbytes((tm, D_MODEL), BF16)
    wide = lambda index: pl.BlockSpec((1, tm, D_MODEL), index)
    half = lambda index: pl.BlockSpec((1, tm, MLA_WIDTH), index)
    const3 = lambda i, j: (0, 0, 0)
    return pl.pallas_call(
        functools.partial(_conv_kernel, tm=tm, pipelined=nt > 1),
        grid=(b, nt),
        in_specs=[wide(tokb), wide(nxtb), half(tokb), half(nxtb), half(tokb), half(nxtb),
                  pl.BlockSpec((1, CONV_HALO, D_MODEL), perb)]
                 + [pl.BlockSpec(a.shape, const3 if a.ndim == 3 else const) for a in consts],
        out_specs=[pl.BlockSpec((1, tm, D_MODEL), tokb), pl.BlockSpec((1, CONV_HALO, D_MODEL), perb)],
        out_shape=[jax.ShapeDtypeStruct((b, t, D_MODEL), F32),
                   jax.ShapeDtypeStruct((b, CONV_HALO, D_MODEL), F32)],
        scratch_shapes=[pltpu.VMEM((ext, D_MODEL), F32), pltpu.VMEM((ext, D_MODEL), BF16),
                        pltpu.VMEM((tm + CONV_HALO, D_MODEL), BF16), pltpu.VMEM((tm, D_MODEL), F32),
                        pltpu.VMEM((tm, D_MODEL), BF16), pltpu.VMEM((tm, D_MODEL), F32),
                        pltpu.VMEM((tm, D_MODEL), F32)],
        compiler_params=pltpu.CompilerParams(
            dimension_semantics=("arbitrary", "arbitrary"),
            vmem_limit_bytes=_vmem_limit(blk, scratch, temp_bytes=8 * _nbytes((tm, D_MODEL), F32))),
        name="conv_layer")(x, x, mla, mla, s5m, s5m, past32, *consts)


def _head_groups(nope, rope, third):
    return jnp.concatenate([nope, rope, third], axis=-1).reshape(nope.shape[0], HEAD_PAD)


def _half_swap(a):
    half = ROPE_DIM // 2
    return jnp.concatenate([a[..., half:], a[..., :half]], axis=-1)


def _lane_pad(a, lo, width=LANES):
    pad = [(0, 0)] * (a.ndim - 1) + [(lo, width - lo - a.shape[-1])]
    return jnp.pad(a, pad)


def _prep_ab_weights(norm_g, w_in, g_q_lat, w_uq, g_kv_lat, w_uk, w_uv, g_q_nope, g_q_rope, g_k_nope, g_k_rope,
                     w_out):
    o_kr = Q_LORA + KV_LORA
    kr_cols = w_in[:, o_kr:o_kr + ROPE_DIM]
    kr_group = jnp.concatenate([kr_cols, _half_swap(kr_cols), jnp.zeros((D_MODEL, LANES - 2 * ROPE_DIM), F32)], -1)
    w_in_p = jnp.concatenate([w_in[:, :o_kr], kr_group, w_in[:, o_kr + ROPE_DIM:]], axis=-1).astype(BF16)
    uq = w_uq.reshape(Q_LORA, MLA_HEADS, NOPE_DIM + ROPE_DIM)
    uq_r = uq[..., NOPE_DIM:]
    w_uq_p = _head_groups(uq[..., :NOPE_DIM], uq_r, _half_swap(uq_r)).astype(BF16)
    zeros_kv = jnp.zeros((KV_LORA, MLA_HEADS, LANES - NOPE_DIM), F32)
    w_uk_p = jnp.concatenate([w_uk, zeros_kv], axis=-1).reshape(KV_LORA, HEAD_PAD).astype(BF16)
    w_uv_t = w_uv.reshape(KV_LORA, MLA_WIDTH).T.astype(BF16)
    r = np.arange(2 * LANES)
    same = (r[:, None] // LANES) == (r[None, :] // LANES)
    ri, ci = r[:, None] % LANES, r[None, :] % LANES
    e2 = np.where(same & (ri < NOPE_DIM) & (ci < NOPE_DIM), 1.0 / NOPE_DIM, 0.0) \
        + np.where(same & (ri >= NOPE_DIM) & (ri < NOPE_DIM + ROPE_DIM) & (ci >= NOPE_DIM), 1.0 / ROPE_DIM, 0.0)
    place = np.zeros((ROPE_DIM, LANES), np.float32)
    place[np.arange(ROPE_DIM), NOPE_DIM + np.arange(ROPE_DIM)] = 1.0
    row = lambda a: a.reshape(1, -1)
    return dict(
        norm_g=row(norm_g), w_in=w_in_p, g_q_lat=row(g_q_lat), w_uq=w_uq_p, g_kv=row(g_kv_lat),
        w_uk=w_uk_p, w_uvt=w_uv_t, e2=jnp.asarray(e2, BF16), place=jnp.asarray(place, BF16),
        gq=row(jnp.concatenate([g_q_nope, g_q_rope, jnp.zeros((ROPE_DIM,), F32)])),
        gqs=row(_lane_pad(_half_swap(g_q_rope), NOPE_DIM)),
        gk=row(_lane_pad(g_k_rope, 0)), gks=row(_lane_pad(_half_swap(g_k_rope), 0)),
        gkn=row(_lane_pad(g_k_nope, 0)), gkn2=row(jnp.tile(_lane_pad(g_k_nope, 0), 2)),
        w_out_a=w_out[:MLA_WIDTH].astype(BF16), w_out_s=w_out[MLA_WIDTH:].astype(BF16))


def _rope_tables(pos, attn_block):
    half = ROPE_DIM // 2
    inv = ROPE_BASE ** (-jnp.arange(half, dtype=F32) / half)
    ang = pos.astype(F32)[:, None] * inv[None, :]
    cos = jnp.cos(ang)
    sin = jnp.sin(ang)
    cosf = jnp.concatenate([cos, cos], axis=-1)
    sinf = jnp.concatenate([-sin, sin], axis=-1)
    ones = jnp.ones((pos.shape[0], NOPE_DIM), F32)
    cq = _lane_pad(jnp.concatenate([ones, cosf], axis=-1), 0)
    sq = _lane_pad(sinf, NOPE_DIM)
    n_t = pos.shape[0]
    if attn_block is None:
        qm = km = jnp.zeros((n_t, LANES), F32)
    else:
        n_chunks = attn_block // CHUNK
        assert MASK_LANE0 + n_chunks <= LANES
        own = (jnp.arange(n_t, dtype=jnp.int32) % attn_block) // CHUNK
        c = jnp.arange(n_chunks, dtype=jnp.int32)
        km = _lane_pad((c[None, :] == own[:, None]).astype(F32), MASK_LANE0)
        qm = _lane_pad(jnp.where(c[None, :] <= own[:, None], 0.0, NEG_INF).astype(F32), MASK_LANE0)
    return cq, sq, _lane_pad(cosf, 0), _lane_pad(sinf, 0), qm, km


def _prep_s5_weights(lam_re, lam_im, log_dt, b_re, b_im, c_re, c_im, d_skip, w_glu, b_glu, *, max_tile):
    pos, neg, lam1, bbar = _s5_prep(lam_re, lam_im, log_dt, b_re, b_im, sub=S5_SUB)
    eye = jnp.eye(S5_HALF_GROUPS, dtype=F32)

    def b_block(bb):
        bb = bb.reshape(S5_GROUP, 2, S5_HALF_GROUPS, S5_STATE)
        return jnp.einsum('nhgp,gk->hgnkp', bb, eye).reshape(2, S5_HALF_GROUPS * S5_GROUP, S5_HALF_STATE)

    def c_block(cc):
        cc = cc.reshape(2, S5_HALF_GROUPS, S5_GROUP, S5_STATE)
        return jnp.einsum('hgnp,gk->hkpgn', cc, eye).reshape(2, S5_HALF_STATE, S5_HALF_GROUPS * S5_GROUP)

    n_t = S5_HALF_STATE // LANES
    b_re_blk, b_im_blk = b_block(bbar[0]), b_block(bbar[1])
    bblk = jnp.stack([b_re_blk.reshape(2, -1, n_t, LANES), b_im_blk.reshape(2, -1, n_t, LANES)], axis=3)
    bblk = bblk.reshape(2, S5_HALF_GROUPS * S5_GROUP, 2 * S5_HALF_STATE).astype(BF16)
    c_re_blk, c_im_blk = c_block(c_re), c_block(-c_im)
    cblk = jnp.stack([c_re_blk.reshape(2, n_t, LANES, -1), c_im_blk.reshape(2, n_t, LANES, -1)], axis=2)
    cblk = cblk.reshape(2, 2 * S5_HALF_STATE, S5_HALF_GROUPS * S5_GROUP).astype(BF16)

    def tiled(tab):
        rows = tab.shape[1]
        return jnp.transpose(tab.reshape(2, rows, 2 * n_t, LANES), (1, 2, 0, 3)).reshape(rows, 4 * S5_HALF_STATE)

    r = np.arange(max_tile)
    tri = ((r[:, None] // S5_SUB) == (r[None, :] // S5_SUB)) & (r[None, :] <= r[:, None])
    return dict(bblk=bblk, cblk=cblk, tri=jnp.asarray(tri, BF16), neg=tiled(neg), pos=tiled(pos),
                lam1=tiled(lam1[:, None, :]),
                d=d_skip.reshape(1, -1), w_glu=w_glu.astype(BF16), b_glu=b_glu.reshape(1, -1))


def _pack_state(s_re, s_im):
    b = s_re.shape[0]
    st = jnp.stack([s_re.reshape(b, -1, LANES), s_im.reshape(b, -1, LANES)], axis=2)
    return st.reshape(b, 1, 4 * S5_HALF_STATE)


def _unpack_state(st):
    b = st.shape[0]
    st = st.reshape(b, -1, 2, LANES)
    return (st[:, :, 0].reshape(b, S5_GROUPS, S5_STATE), st[:, :, 1].reshape(b, S5_GROUPS, S5_STATE))


def _ab_branches(x, pos, past, wa, ws, *, tm_in, tm_s5, tq, tk):
    b, t, _ = x.shape
    x2d = x.reshape(b * t, D_MODEL)
    tabs = _rope_tables(pos, tq if past is None else None)
    lat, kr, qp, kp, vt, gm, u, gs = _ab_in(x2d, tabs, wa, seq_len=t, tm=tm_in)
    r3 = lambda a: a.reshape(b, t, a.shape[-1])
    if past is None:
        mla = _attn_prompt(r3(qp), r3(kp), vt, r3(gm), tq=tq)
        s0 = jnp.zeros((b, 1, 4 * S5_HALF_STATE), F32)
    else:
        past_lat, past_kr, past_re, past_im = past
        vt_b = jnp.transpose(vt.reshape(MLA_WIDTH, b, t), (1, 0, 2))
        mla = _attn_sample(r3(qp), past_lat, past_kr, r3(kp), vt_b, r3(gm), wa, tk=tk)
        s0 = _pack_state(past_re, past_im)
    s5, sfin = _s5(r3(u), r3(gs), s0, ws, tm=tm_s5)
    fin_re, fin_im = _unpack_state(sfin)
    return mla, s5, r3(lat), r3(kr), fin_re, fin_im


def _conv_taps(conv_w):
    by_lag = jnp.pad(conv_w[::-1], ((0, CONV_HALO - conv_w.shape[0]), (0, 0))).astype(BF16)
    return jnp.broadcast_to(by_lag[:, None, :], (CONV_HALO, CONV_PAD, conv_w.shape[1]))


def _conv(x, mla, s5m, past, wc, *, tm):
    b = x.shape[0]
    if past is None:
        past32 = jnp.zeros((b, CONV_HALO, D_MODEL), F32)
    else:
        past32 = jnp.pad(past, ((0, 0), (CONV_HALO - (CONV_WIDTH - 1), 0), (0, 0)))
    y, newc = _conv_layer(x, mla, s5m, past32, wc, tm=tm)
    return y, newc[:, CONV_HALO - (CONV_WIDTH - 1):]


def kernel(x_prompt, x_sample, cache_mla_latent, cache_mla_krope, state_s5_re, state_s5_im, state_conv, norm_ab, w_in_ab, g_q_lat, w_uq, g_kv_lat, w_uk, w_uv, g_q_nope, g_q_rope, g_k_nope, g_k_rope, s5_lam_re, s5_lam_im, s5_log_dt, s5_b_re, s5_b_im, s5_c_re, s5_c_im, s5_d, s5_w_glu, s5_b_glu, w_out_ab, norm_c, w_in_c, conv_w, conv_b, ln_g, ln_b, w_out_c):
    t_p = x_prompt.shape[1]
    t_s = x_sample.shape[1]
    past_len = cache_mla_latent.shape[2]
    pos_p = jnp.arange(t_p, dtype=jnp.int32)
    pos_s = past_len + jnp.arange(t_s, dtype=jnp.int32)

    i = 0
    wa = _prep_ab_weights(norm_ab[i], w_in_ab[i], g_q_lat[i], w_uq[i], g_kv_lat[i], w_uk[i], w_uv[i],
                          g_q_nope[i], g_q_rope[i], g_k_nope[i], g_k_rope[i], w_out_ab[i])
    ws = _prep_s5_weights(s5_lam_re[i], s5_lam_im[i], s5_log_dt[i], s5_b_re[i], s5_b_im[i], s5_c_re[i],
                          s5_c_im[i], s5_d[i], s5_w_glu[i], s5_b_glu[i], max_tile=256)
    mla_p, s5_p, lat_p, kr_p, re_p, im_p = _ab_branches(
        x_prompt, pos_p, None, wa, ws, tm_in=512, tm_s5=256, tq=512, tk=512)
    mla_s, s5_s, lat_s, kr_s, re_s, im_s = _ab_branches(
        x_sample, pos_s, (cache_mla_latent[i], cache_mla_krope[i], state_s5_re[i], state_s5_im[i]), wa, ws,
        tm_in=256, tm_s5=t_s, tq=512, tk=1024)

    row = lambda a: a.reshape(1, -1)
    cw = _conv_taps(conv_w[i])
    wc = dict(w_out_a=wa['w_out_a'], w_out_s=wa['w_out_s'],
              norm_g=row(norm_c[i]), w_in=w_in_c[i].astype(BF16), conv_w=cw, conv_b=row(conv_b[i]),
              ln_g=row(ln_g[i]), ln_b=row(ln_b[i]), w_out=w_out_c[i].astype(BF16))
    yp, conv_p = _conv(x_prompt, mla_p, s5_p, None, wc, tm=256)
    ys, conv_s = _conv(x_sample, mla_s, s5_s, state_conv[i], wc, tm=t_s)

    st = lambda a: a[None]
    return (yp, ys, st(lat_p), st(kr_p), st(re_p), st(im_p), st(conv_p),
            st(lat_s), st(kr_s), st(re_s), st(im_s), st(conv_s))
```

```python
import functools
import math

import numpy as np
import jax
import jax.numpy as jnp
from jax import lax
from jax.experimental import pallas as pl
from jax.experimental.pallas import tpu as pltpu

F32 = jnp.float32
BF16 = jnp.bfloat16

D_MODEL = 1024
CHUNK = 64
MLA_HEADS = 8
Q_LORA = 384
KV_LORA = 256
NOPE_DIM = 64
ROPE_DIM = 32
V_DIM = 64
MLA_WIDTH = MLA_HEADS * V_DIM
ROPE_BASE = 10000.0
ATTN_SCALE = (NOPE_DIM + ROPE_DIM) ** -0.5
S5_WIDTH = 512
S5_GROUP = 16
S5_GROUPS = S5_WIDTH // S5_GROUP
S5_STATE = 64
CONV_WIDTH = 31
EPS = 1e-6
NEG_INF = -1e30
LOG2E = math.log2(math.e)
MASK_LANE0 = NOPE_DIM + ROPE_DIM

LANES = 128
SUBLANES = 8
HEAD_PAD = MLA_HEADS * LANES
S5_HALF_GROUPS = S5_GROUPS // 2
S5_HALF_STATE = S5_HALF_GROUPS * S5_STATE
S5_SUB = 32
CONV_HALO = 32
CONV_PAD = 16
V7X_VMEM_BYTES = 64 * 1024 * 1024


def _vmem_limit(block_bytes, scratch_bytes=0, temp_bytes=0):
    est = 2 * block_bytes + scratch_bytes + temp_bytes
    return int(min(max(est, 16 * 1024 * 1024), V7X_VMEM_BYTES - 8 * 1024 * 1024))


def _nbytes(shape, dtype):
    return int(np.prod(shape)) * jnp.dtype(dtype).itemsize


def _rms(x, g):
    return x * lax.rsqrt(jnp.mean(x * x, axis=-1, keepdims=True) + EPS) * g


def _silu(x):
    return x * jax.nn.sigmoid(x)


def _dot(a, b):
    return jnp.dot(a, b, preferred_element_type=F32)


def _dot_nt(a, b):
    return lax.dot_general(a, b, (((1,), (1,)), ((), ())), preferred_element_type=F32)


def _ab_in_kernel(x_ref, ng_ref, win_ref, gql_ref, wuq_ref, gkv_ref, wuk_ref, wuv_ref, e2_ref,
                  cq_ref, sq_ref, ck_ref, sk_ref, qm_ref, km_ref, gq_ref, gqs_ref, gk_ref, gks_ref, gkn_ref,
                  lat_ref, kr_ref, qp_ref, kp_ref, vt_ref, gm_ref, u_ref, gs_ref):
    x = x_ref[...]
    h = _rms(x, ng_ref[...]).astype(BF16)

    def proj(lo, hi):
        return _dot(h, win_ref[:, lo:hi])

    q_lat = proj(0, Q_LORA)
    c_kv = proj(Q_LORA, Q_LORA + KV_LORA)
    krg = proj(640, 768)
    gm_ref[...] = proj(768, 1280)
    u_ref[...] = proj(1280, 1792)
    gs_ref[...] = proj(1792, 2304)

    c_n = _rms(c_kv, gkv_ref[...])
    lat_ref[...] = c_n
    cb = c_n.astype(BF16)

    lane = lax.broadcasted_iota(jnp.int32, (1, LANES), 1)
    ms = jnp.sum(jnp.where(lane < ROPE_DIM, krg * krg, 0.0), axis=-1, keepdims=True) * (1.0 / ROPE_DIM)
    kr = lax.rsqrt(ms + EPS) * (krg * (ck_ref[...] * gk_ref[...])
                                + pltpu.roll(krg, LANES - ROPE_DIM, 1) * (sk_ref[...] * gks_ref[...]))
    kr_ref[...] = kr[:, :ROPE_DIM]
    kr_mask = pltpu.roll(kr, NOPE_DIM, 1) + km_ref[...]

    qn = _rms(q_lat, gql_ref[...]).astype(BF16)
    qa_tab = cq_ref[...] * (gq_ref[...] * (ATTN_SCALE * LOG2E))
    qb_tab = sq_ref[...] * (gqs_ref[...] * (ATTN_SCALE * LOG2E))
    n_pairs = MLA_HEADS // 2
    pair_cols = [slice(2 * LANES * p, 2 * LANES * (p + 1)) for p in range(n_pairs)]
    qas = [_dot(qn, wuq_ref[:, cols]) for cols in pair_cols]
    kas = [_dot(cb, wuk_ref[:, cols]) for cols in pair_cols]
    vt_ref[...] = _dot_nt(wuv_ref[...], cb).astype(BF16)
    q_ms = [_dot((qa * qa).astype(BF16), e2_ref[...]) for qa in qas]
    k_ms = [_dot((ka * ka).astype(BF16), e2_ref[...]) for ka in kas]
    for p in range(n_pairs):
        lo = 2 * LANES * p
        qs = qas[p] * lax.rsqrt(q_ms[p] + EPS)
        ks = kas[p] * lax.rsqrt(k_ms[p] + EPS)
        for j in range(2):
            sl = slice(LANES * j, LANES * (j + 1))
            s = qs[:, sl]
            qp_ref[:, lo + LANES * j:lo + LANES * (j + 1)] = (
                s * qa_tab + pltpu.roll(s, LANES - ROPE_DIM, 1) * qb_tab + qm_ref[...]).astype(BF16)
            kp_ref[:, lo + LANES * j:lo + LANES * (j + 1)] = (
                ks[:, sl] * gkn_ref[...] + kr_mask).astype(BF16)


def _ab_in(x2d, pos_tabs, w, *, seq_len, tm):
    n_tok = x2d.shape[0]
    if tm > seq_len:
        pos_tabs = tuple(jnp.tile(p, (tm // seq_len, 1)) for p in pos_tabs)
    n_pos = max(seq_len // tm, 1)
    tok = lambda i: (i, 0)
    pos = lambda i: (i % n_pos, 0)
    const = lambda i: (0, 0)

    def full(a):
        return pl.BlockSpec(a.shape, const, pipeline_mode=pl.Buffered(1))

    ins = [x2d, w['norm_g'], w['w_in'], w['g_q_lat'], w['w_uq'], w['g_kv'], w['w_uk'], w['w_uvt'], w['e2'],
           *pos_tabs, w['gq'], w['gqs'], w['gk'], w['gks'], w['gkn']]
    n_tabs = len(pos_tabs)
    in_specs = [pl.BlockSpec((tm, D_MODEL), tok)] + [full(a) for a in ins[1:9]] \
        + [pl.BlockSpec((tm, LANES), pos)] * n_tabs + [full(a) for a in ins[9 + n_tabs:]]
    outs = [((n_tok, KV_LORA), F32), ((n_tok, ROPE_DIM), F32), ((n_tok, HEAD_PAD), BF16),
            ((n_tok, HEAD_PAD), BF16), ((MLA_WIDTH, n_tok), BF16), ((n_tok, MLA_WIDTH), F32),
            ((n_tok, S5_WIDTH), F32), ((n_tok, S5_WIDTH), F32)]
    vt_index = 4
    out_specs = [pl.BlockSpec((MLA_WIDTH, tm), lambda i: (0, i)) if n == vt_index
                 else pl.BlockSpec((tm, s[1]), tok) for n, (s, _) in enumerate(outs)]
    out_shape = [jax.ShapeDtypeStruct(s, d) for s, d in outs]
    blk = sum(_nbytes(a.shape, a.dtype) for a in ins[1:9]) + _nbytes((tm, D_MODEL), F32) \
        + n_tabs * _nbytes((tm, LANES), F32) + sum(_nbytes(s, d) for s, d in outs) * tm // n_tok
    return pl.pallas_call(
        _ab_in_kernel, grid=(n_tok // tm,), in_specs=in_specs, out_specs=out_specs, out_shape=out_shape,
        compiler_params=pltpu.CompilerParams(
            dimension_semantics=("arbitrary",),
            vmem_limit_bytes=_vmem_limit(blk, temp_bytes=8 * _nbytes((tm, D_MODEL), F32))),
        name="ab_in")(*ins)


def _ones_row(n, rows=V_DIM):
    return (lax.broadcasted_iota(jnp.int32, (rows, n), 0) == 0).astype(BF16)


VAUG_ROWS = 2 * V_DIM


def _attn_prompt_kernel(qi_ref, kj_ref, q_ref, k_ref, vt_ref, g_ref, o_ref, vaug_sc, s_sc, acc_sc, *, tq,
                        seq_len):
    n_steps = qi_ref.shape[0]
    ones_row = _ones_row(seq_len, VAUG_ROWS - V_DIM)
    for hh in range(2):
        vaug_sc[hh, 0:V_DIM, :] = vt_ref[hh * V_DIM:(hh + 1) * V_DIM, :]
        vaug_sc[hh, V_DIM:VAUG_ROWS, :] = ones_row
    lane = lax.broadcasted_iota(jnp.int32, (1, LANES), 1)
    mask_lane_off = jnp.logical_or(lane < MASK_LANE0, lane >= MASK_LANE0 + tq // CHUNK)

    def produce(n, slot):
        qi = qi_ref[n]
        kj = kj_ref[n]
        r0 = pl.multiple_of(qi * tq, tq)
        k0 = pl.multiple_of(kj * tq, tq)
        q_keep = jnp.logical_or(mask_lane_off, kj == qi)
        bms = []
        for hh in range(2):
            hs = slice(hh * LANES, (hh + 1) * LANES)
            q = q_ref[0, pl.ds(r0, tq), hs]
            q = jnp.where(q_keep, q, jnp.zeros_like(q))
            s = _dot_nt(k_ref[0, pl.ds(k0, tq), hs], q)
            s_sc[slot, hh] = s
            bms.append(jnp.max(s, axis=0, keepdims=True))
        return tuple(bms)

    def consume(n, slot, bms, state):
        qi = qi_ref[n]
        kj = kj_ref[n]
        k0 = pl.multiple_of(kj * tq, tq)
        out = []
        for hh in range(2):
            m, acc = state[hh]
            m = jnp.where(kj == 0, -jnp.inf, m)
            m_new = jnp.maximum(m, bms[hh])
            p = jnp.exp2(s_sc[slot, hh] - m_new)
            alpha = jnp.exp2(m - m_new)
            acc = alpha * acc + _dot(vaug_sc[hh, :, pl.ds(k0, tq)], p.astype(BF16))
            acc_sc[qi, hh] = acc
            out.append((m_new, acc))
        return tuple(out)

    unroll = 8

    def multi_step(t, carry):
        bms, state = carry
        for u in range(unroll):
            n = unroll * t + u
            next_bms = produce(n + 1, (u + 1) % 2)
            state = consume(n, u % 2, bms, state)
            bms = next_bms
        return bms, state

    state = tuple((jnp.full((1, tq), -jnp.inf, F32), jnp.zeros((VAUG_ROWS, tq), F32)) for _ in range(2))
    n_loops = (n_steps - 1) // unroll
    bms, state = lax.fori_loop(0, n_loops, multi_step, (produce(0, 0), state))
    for n in range(unroll * n_loops, n_steps):
        next_bms = produce(n + 1, (n + 1) % 2) if n + 1 < n_steps else None
        state = consume(n, n % 2, bms, state)
        bms = next_bms

    def finalize(qi, carry):
        r0 = pl.multiple_of(qi * tq, tq)
        vals = []
        for hh in range(2):
            acc = acc_sc[qi, hh]
            vals.append(acc[0:V_DIM, :] * (1.0 / acc[V_DIM:V_DIM + 1, :]))
        o = jnp.concatenate(vals, axis=0).T
        o_ref[0, pl.ds(r0, tq), :] = (o * _silu(g_ref[0, pl.ds(r0, tq), :])).astype(BF16)
        return carry

    lax.fori_loop(0, seq_len // tq, finalize, 0)


def _attn_prompt(qp, kp, vt, gate, *, tq):
    b, t, _ = qp.shape
    nq = t // tq
    steps = [(qi, kj) for qi in range(nq) for kj in range(qi + 1)]
    qi_tab = jnp.asarray([s[0] for s in steps], jnp.int32)
    kj_tab = jnp.asarray([s[1] for s in steps], jnp.int32)
    pair = lambda i, j, qt, kt: (i, 0, j)
    blk = 2 * _nbytes((t, 2 * LANES), BF16) + _nbytes((LANES, t), BF16) + _nbytes((t, LANES), F32) \
        + _nbytes((t, LANES), BF16)
    scratch = _nbytes((2, VAUG_ROWS, t), BF16) + _nbytes((2, 2, tq, tq), F32) + _nbytes((nq, 2, VAUG_ROWS, tq), F32)
    return pl.pallas_call(
        functools.partial(_attn_prompt_kernel, tq=tq, seq_len=t),
        grid_spec=pltpu.PrefetchScalarGridSpec(
            num_scalar_prefetch=2,
            grid=(b, MLA_HEADS // 2),
            in_specs=[pl.BlockSpec((1, t, 2 * LANES), pair), pl.BlockSpec((1, t, 2 * LANES), pair),
                      pl.BlockSpec((LANES, t), lambda i, j, qt, kt: (j, i)), pl.BlockSpec((1, t, LANES), pair)],
            out_specs=pl.BlockSpec((1, t, LANES), pair),
            scratch_shapes=[pltpu.VMEM((2, VAUG_ROWS, t), BF16), pltpu.VMEM((2, 2, tq, tq), F32),
                            pltpu.VMEM((nq, 2, VAUG_ROWS, tq), F32)]),
        out_shape=jax.ShapeDtypeStruct((b, t, MLA_WIDTH), BF16),
        compiler_params=pltpu.CompilerParams(
            dimension_semantics=("arbitrary", "arbitrary"),
            vmem_limit_bytes=_vmem_limit(blk, scratch, temp_bytes=12 * _nbytes((tq, tq), F32))),
        name="attn_prompt")(qi_tab, kj_tab, qp, kp, vt, gate)


def _attn_sample_kernel(q_ref, lat_ref, kr_ref, kpn_ref, vtn_ref, g_ref, wuk_ref, wuvt_ref, e2_ref, gkn2_ref,
                        place_ref, o_ref, qbd_sc, m_sc, acc_sc, *, n_past_blocks):
    j = pl.program_id(1)
    n_pairs = MLA_HEADS // 2
    t = q_ref.shape[1]
    lane = lax.broadcasted_iota(jnp.int32, (1, LANES), 1)

    @pl.when(j == 0)
    def _():
        m_sc[...] = jnp.full(m_sc.shape, -jnp.inf, F32)
        acc_sc[...] = jnp.zeros(acc_sc.shape, F32)
        q = q_ref[0].astype(F32)
        q_t = jnp.concatenate([q, jnp.zeros((LANES - t, HEAD_PAD), F32)], axis=0).T
        for p in range(n_pairs):
            top = q_t[2 * LANES * p:2 * LANES * p + LANES, :]
            bot = pltpu.roll(q_t[2 * LANES * p + LANES:2 * LANES * (p + 1), :], t, 1)
            qbd_sc[p] = jnp.concatenate([top, bot], axis=0).astype(BF16)

    def update_all(kp_pairs, vt_pairs):
        n = kp_pairs[0].shape[0]
        ones_row = _ones_row(n)
        ss = [_dot(kp_pairs[p], qbd_sc[p]) for p in range(n_pairs)]
        for p in range(n_pairs):
            m_old = m_sc[p]
            m_new = jnp.maximum(m_old, jnp.max(ss[p], axis=0, keepdims=True))
            pr = jnp.exp2(ss[p] - m_new)
            alpha = jnp.exp2(m_old - m_new)
            vt_pair = vt_pairs[p]
            vaug = jnp.concatenate([vt_pair[0:V_DIM, :], ones_row, vt_pair[V_DIM:2 * V_DIM, :], ones_row], axis=0)
            acc_sc[p] = alpha * acc_sc[p] + _dot(vaug, pr.astype(BF16))
            m_sc[p] = m_new

    @pl.when(j < n_past_blocks)
    def _():
        latb = lat_ref[0].astype(BF16)
        kr128 = _dot(kr_ref[0].astype(BF16), place_ref[...])
        kr256 = jnp.concatenate([kr128, kr128], axis=1)
        vt_all = _dot_nt(wuvt_ref[...], latb).astype(BF16)
        kas = [_dot(latb, wuk_ref[:, 2 * LANES * p:2 * LANES * (p + 1)]) for p in range(n_pairs)]
        mss = [_dot((ka * ka).astype(BF16), e2_ref[...]) for ka in kas]
        kps = [(ka * lax.rsqrt(ms + EPS) * gkn2_ref[...] + kr256).astype(BF16) for ka, ms in zip(kas, mss)]
        update_all(kps, [vt_all[LANES * p:LANES * (p + 1), :] for p in range(n_pairs)])

    @pl.when(j == n_past_blocks)
    def _():
        update_all([kpn_ref[0, :, 2 * LANES * p:2 * LANES * (p + 1)] for p in range(n_pairs)],
                   [vtn_ref[0, LANES * p:LANES * (p + 1), :] for p in range(n_pairs)])
        for p in range(n_pairs):
            acc_t = acc_sc[p].T
            a0 = acc_t[0:t, 0:LANES]
            a1 = acc_t[t:2 * t, LANES:2 * LANES]
            o0 = a0 * (1.0 / a0[:, V_DIM:V_DIM + 1])
            o1 = a1 * (1.0 / a1[:, V_DIM:V_DIM + 1])
            o = jnp.where(lane < V_DIM, o0, pltpu.roll(o1, V_DIM, 1))
            g = g_ref[0, :, p * LANES:(p + 1) * LANES]
            o_ref[0, :, p * LANES:(p + 1) * LANES] = (o * _silu(g)).astype(BF16)


def _attn_sample(qp, past_lat, past_kr, kp_new, vt_new, gate, w, *, tk):
    b, t, _ = qp.shape
    assert 2 * t == LANES, "two heads' queries share one 128-lane group"
    n_past_blocks = past_lat.shape[1] // tk
    n_pairs = MLA_HEADS // 2
    cur = lambda i, j: (i, 0, 0)
    past = lambda i, j: (i, jnp.minimum(j, n_past_blocks - 1), 0)
    const = lambda i, j: (0, 0)
    consts = [w['w_uk'], w['w_uvt'], w['e2'], w['gkn2'], w['place']]
    blk = _nbytes((tk, KV_LORA), F32) + _nbytes((tk, LANES), F32) + 2 * _nbytes((t, HEAD_PAD), BF16) \
        + _nbytes((MLA_WIDTH, LANES), BF16) + 2 * _nbytes((t, MLA_WIDTH), F32) \
        + sum(_nbytes(a.shape, a.dtype) for a in consts)
    scratch = _nbytes((n_pairs, 2 * LANES, LANES), BF16) + _nbytes((n_pairs, SUBLANES, LANES), F32) \
        + _nbytes((n_pairs, 2 * LANES, LANES), F32)
    return pl.pallas_call(
        functools.partial(_attn_sample_kernel, n_past_blocks=n_past_blocks),
        grid=(b, n_past_blocks + 1),
        in_specs=[pl.BlockSpec((1, t, HEAD_PAD), cur),
                  pl.BlockSpec((1, tk, KV_LORA), past),
                  pl.BlockSpec((1, tk, ROPE_DIM), past),
                  pl.BlockSpec((1, t, HEAD_PAD), cur),
                  pl.BlockSpec((1, MLA_WIDTH, t), cur),
                  pl.BlockSpec((1, t, MLA_WIDTH), cur)] + [pl.BlockSpec(a.shape, const) for a in consts],
        out_specs=pl.BlockSpec((1, t, MLA_WIDTH), cur),
        out_shape=jax.ShapeDtypeStruct((b, t, MLA_WIDTH), BF16),
        scratch_shapes=[pltpu.VMEM((n_pairs, 2 * LANES, LANES), BF16),
                        pltpu.VMEM((n_pairs, 1, LANES), F32),
                        pltpu.VMEM((n_pairs, 2 * LANES, LANES), F32)],
        compiler_params=pltpu.CompilerParams(
            dimension_semantics=("arbitrary", "arbitrary"),
            vmem_limit_bytes=_vmem_limit(blk, scratch, temp_bytes=12 * _nbytes((tk, 2 * LANES), F32))),
        name="attn_sample")(qp, past_lat, past_kr, kp_new, vt_new, gate, *consts)


def _s5_prep_kernel(lr_ref, li_ref, ldt_ref, br_ref, bi_ref, pos_ref, neg_ref, lam1_ref, bbar_ref, *, sub):
    lr = lr_ref[...]
    li = li_ref[...]
    dt = jnp.exp(ldt_ref[...])
    ar = lr * dt
    ai = li * dt
    k = lax.broadcasted_iota(jnp.int32, (sub, 1), 0).astype(F32)
    mag_p = jnp.exp(k * ar)
    mag_n = jnp.exp(-k * ar)
    ang = k * ai
    c = jnp.cos(ang)
    s = jnp.sin(ang)
    pos_ref[0] = mag_p * c
    pos_ref[1] = mag_p * s
    neg_ref[0] = mag_n * c
    neg_ref[1] = -mag_n * s
    mag1 = jnp.exp(ar)
    l1r = mag1 * jnp.cos(ai)
    l1i = mag1 * jnp.sin(ai)
    lam1_ref[0:1, :] = l1r
    lam1_ref[1:2, :] = l1i
    inv = 1.0 / (lr * lr + li * li)
    nr = l1r - 1.0
    cr = (nr * lr + l1i * li) * inv
    ci = (l1i * lr - nr * li) * inv
    br = br_ref[...]
    bi = bi_ref[...]
    bbar_ref[0] = cr * br - ci * bi
    bbar_ref[1] = cr * bi + ci * br


def _s5_prep(lam_re, lam_im, log_dt, b_re, b_im, *, sub):
    gp = S5_GROUPS * S5_STATE
    lr = lam_re.reshape(1, gp)
    li = lam_im.reshape(1, gp)
    ldt = jnp.broadcast_to(log_dt[:, None], (S5_GROUPS, S5_STATE)).reshape(1, gp)
    br = jnp.transpose(b_re, (2, 0, 1)).reshape(S5_GROUP, gp)
    bi = jnp.transpose(b_im, (2, 0, 1)).reshape(S5_GROUP, gp)
    return pl.pallas_call(
        functools.partial(_s5_prep_kernel, sub=sub),
        out_shape=[jax.ShapeDtypeStruct((2, sub, gp), F32), jax.ShapeDtypeStruct((2, sub, gp), F32),
                   jax.ShapeDtypeStruct((2, gp), F32), jax.ShapeDtypeStruct((2, S5_GROUP, gp), F32)],
        name="s5_prep")(lr, li, ldt, br, bi)


def _s5_kernel(u_ref, gs_ref, s0_ref, bblk_ref, cblk_ref, tri_ref, neg_ref, pos_ref, lam1_ref, d_ref,
               wglu_ref, bglu_ref, o_ref, sfin_ref, carry_sc, *, tm, sub):
    ti = pl.program_id(1)
    tile = 2 * LANES
    tiles_per_half = 2 * S5_HALF_STATE // tile

    @pl.when(ti == 0)
    def _():
        carry_sc[...] = s0_ref[0]

    u = u_ref[0]
    ub = u.astype(BF16)
    n_tiles = 2 * tiles_per_half

    def lanes_of(t):
        lo = t * tile
        return slice(lo, lo + LANES), slice(lo + LANES, lo + tile)

    def stage_a(t):
        hf, q = divmod(t, tiles_per_half)
        re, im = lanes_of(t)
        bu = _dot(ub[:, hf * 256:(hf + 1) * 256], bblk_ref[hf, :, q * tile:(q + 1) * tile])
        nre, nim = neg_ref[:, re], neg_ref[:, im]
        xs = []
        for c in range(tm // sub):
            rs = slice(c * sub, (c + 1) * sub)
            bre, bim = bu[rs, :LANES], bu[rs, LANES:]
            xs.append(jnp.concatenate([nre * bre - nim * bim, nre * bim + nim * bre], axis=-1).astype(BF16))
        return jnp.concatenate(xs, axis=0)

    def stage_b(t, x):
        re, im = lanes_of(t)
        cs = _dot(tri_ref[...], x)
        cre, cim = carry_sc[:, re], carry_sc[:, im]
        l1r, l1i = lam1_ref[:, re], lam1_ref[:, im]
        pre, pim = pos_ref[:, re], pos_ref[:, im]
        ss = []
        for c in range(tm // sub):
            rs = slice(c * sub, (c + 1) * sub)
            tre = cs[rs, :LANES] + (l1r * cre - l1i * cim)
            tim = cs[rs, LANES:] + (l1r * cim + l1i * cre)
            sre = pre * tre - pim * tim
            sim = pre * tim + pim * tre
            ss.append(jnp.concatenate([sre, sim], axis=-1).astype(BF16))
            cre = sre[sub - 1:sub, :]
            cim = sim[sub - 1:sub, :]
        carry_sc[:, re] = cre
        carry_sc[:, im] = cim
        return jnp.concatenate(ss, axis=0)

    def stage_c(t, s):
        hf, q = divmod(t, tiles_per_half)
        return _dot(s, cblk_ref[hf, q * tile:(q + 1) * tile, :])

    xs_, ss_ = {}, {}
    accs = [None, None]
    skew = 1
    for step in range(n_tiles + 2 * skew):
        if step < n_tiles:
            xs_[step] = stage_a(step)
        if skew <= step < n_tiles + skew:
            ss_[step - skew] = stage_b(step - skew, xs_.pop(step - skew))
        if step >= 2 * skew:
            t = step - 2 * skew
            part = stage_c(t, ss_.pop(t))
            hf = t // tiles_per_half
            accs[hf] = part if accs[hf] is None else accs[hf] + part
    y = jnp.concatenate(accs, axis=-1) + d_ref[...] * u
    z = jax.nn.gelu(y)
    z = z * jax.nn.sigmoid(_dot(z.astype(BF16), wglu_ref[...]) + bglu_ref[...])
    o_ref[0] = (z * _silu(gs_ref[0])).astype(BF16)

    @pl.when(ti == pl.num_programs(1) - 1)
    def _():
        sfin_ref[0] = carry_sc[...]


def _s5(u, gate, s0, w, *, tm):
    b, t, _ = u.shape
    sub = S5_SUB
    tokb = lambda i, j: (i, j, 0)
    perb = lambda i, j: (i, 0, 0)
    c2 = lambda i, j: (0, 0)
    c3 = lambda i, j: (0, 0, 0)
    tri = w['tri'][:tm, :tm]
    consts = [w['bblk'], w['cblk'], tri, w['neg'], w['pos'], w['lam1'], w['d'], w['w_glu'], w['b_glu']]
    blk = 2 * _nbytes((tm, S5_WIDTH), F32) + _nbytes((tm, S5_WIDTH), BF16) + 2 * _nbytes((1, 4 * S5_HALF_STATE), F32) \
        + sum(_nbytes(a.shape, a.dtype) for a in consts)
    scratch = _nbytes((SUBLANES, 4 * S5_HALF_STATE), F32)
    return pl.pallas_call(
        functools.partial(_s5_kernel, tm=tm, sub=sub),
        grid=(b, t // tm),
        in_specs=[pl.BlockSpec((1, tm, S5_WIDTH), tokb), pl.BlockSpec((1, tm, S5_WIDTH), tokb),
                  pl.BlockSpec((1, 1, 4 * S5_HALF_STATE), perb)]
                 + [pl.BlockSpec(a.shape, c3 if a.ndim == 3 else c2) for a in consts],
        out_specs=[pl.BlockSpec((1, tm, S5_WIDTH), tokb), pl.BlockSpec((1, 1, 4 * S5_HALF_STATE), perb)],
        out_shape=[jax.ShapeDtypeStruct((b, t, S5_WIDTH), BF16),
                   jax.ShapeDtypeStruct((b, 1, 4 * S5_HALF_STATE), F32)],
        scratch_shapes=[pltpu.VMEM((1, 4 * S5_HALF_STATE), F32)],
        compiler_params=pltpu.CompilerParams(
            dimension_semantics=("arbitrary", "arbitrary"),
            vmem_limit_bytes=_vmem_limit(blk, scratch, temp_bytes=16 * _nbytes((tm, 2 * LANES), F32))),
        name="s5_scan")(u, gate, s0, *consts)


def _conv_kernel(x_ref, xn_ref, ma_ref, man_ref, ms_ref, msn_ref, past_ref, woa_ref, wos_ref, ng_ref, win_ref,
                 cw_ref, cb_ref, lng_ref, lnb_ref, wout_ref,
                 shift_ref, y_ref, newc_ref, vext_sc, vbe_sc, vbo_sc, stage_sc, hstage_sc, xstage_sc, yc_sc, *,
                 tm, pipelined):
    ti = pl.program_id(1)
    base = CONV_PAD + CONV_HALO

    def stage_tile(x0, mla, s5m):
        x1 = _dot(mla, woa_ref[...]) + _dot(s5m, wos_ref[...]) + x0
        xstage_sc[...] = x1
        h = _rms(x1, ng_ref[...]).astype(BF16)
        hstage_sc[...] = h
        stage_sc[...] = _dot(h, win_ref[:, 0:D_MODEL]) * jax.nn.sigmoid(_dot(h, win_ref[:, D_MODEL:2 * D_MODEL]))

    @pl.when(ti == 0)
    def _():
        vext_sc[0:CONV_PAD, :] = jnp.zeros((CONV_PAD, D_MODEL), F32)
        vext_sc[CONV_PAD:base, :] = past_ref[0]
        if pipelined:
            stage_tile(x_ref[0], ma_ref[0], ms_ref[0])

    if not pipelined:
        stage_tile(x_ref[0], ma_ref[0], ms_ref[0])
    x = xstage_sc[...]
    h = hstage_sc[...]
    vext_sc[base:base + tm, :] = stage_sc[...]
    if pipelined:
        stage_tile(xn_ref[0], man_ref[0], msn_ref[0])

    vbe_sc[...] = vext_sc[...].astype(BF16)
    vbo_sc[...] = vext_sc[SUBLANES:SUBLANES + tm + CONV_HALO, :].astype(BF16)
    rows = min(tm, 128)
    width = 2 * LANES
    win = rows + CONV_PAD
    for rb in range(tm // rows):
        r0 = rb * rows
        for lb in range(D_MODEL // width):
            ls = slice(lb * width, (lb + 1) * width)
            zs = []
            for r in range(SUBLANES):
                z = None
                for a in range(CONV_HALO // SUBLANES):
                    start = r0 + base - CONV_PAD - SUBLANES * a
                    if a % 2 == 0:
                        src = vbe_sc[start:start + win, ls]
                    else:
                        src = vbo_sc[start - SUBLANES:start - SUBLANES + win, ls]
                    term = src.reshape(win // CONV_PAD, CONV_PAD, width) * cw_ref[SUBLANES * a + r, :, ls][None]
                    z = term if z is None else z + term
                zs.append(z)
            zcat = jnp.concatenate(zs, axis=0).reshape(SUBLANES * win, width)
            yc_sc[r0:r0 + rows, ls] = _dot(shift_ref[...], zcat)

    yc = yc_sc[...] + cb_ref[...]
    mu = jnp.mean(yc, axis=-1, keepdims=True)
    xc = yc - mu
    var = jnp.mean(xc * xc, axis=-1, keepdims=True)
    yn = xc * lax.rsqrt(var + EPS) * lng_ref[...] + lnb_ref[...]
    gate = _dot(h, win_ref[:, 2 * D_MODEL:3 * D_MODEL])
    mixed = (_silu(yn) * _silu(gate)).astype(BF16)
    y_ref[0] = _dot(mixed, wout_ref[...]) + x

    tail = vext_sc[CONV_PAD + tm:base + tm, :]
    vext_sc[CONV_PAD:base, :] = tail
    newc_ref[0] = tail


def _conv_layer(x, mla, s5m, past32, w, *, tm):
    b, t, _ = x.shape
    nt = t // tm
    tokb = lambda i, j: (i, j, 0)
    nxtb = lambda i, j: (i, jnp.minimum(j + 1, nt - 1), 0)
    perb = lambda i, j: (i, 0, 0)
    const = lambda i, j: (0, 0)
    rows = min(tm, 128)
    win = rows + CONV_PAD
    shift = np.zeros((rows, SUBLANES * win), np.float32)
    for r in range(SUBLANES):
        shift[np.arange(rows), r * win + np.arange(rows) + CONV_PAD - r] = 1.0
    consts = [w['w_out_a'], w['w_out_s'], w['norm_g'], w['w_in'], w['conv_w'], w['conv_b'], w['ln_g'], w['ln_b'],
              w['w_out'], jnp.asarray(shift, BF16)]
    blk = 3 * _nbytes((tm, D_MODEL), F32) + 4 * _nbytes((tm, MLA_WIDTH), BF16) \
        + 2 * _nbytes((CONV_HALO, D_MODEL), F32) + sum(_nbytes(a.shape, a.dtype) for a in consts)
    ext = tm + CONV_PAD + CONV_HALO
    scratch = _nbytes((ext, D_MODEL), F32) + _nbytes((ext, D_MODEL), BF16) + _nbytes((tm + CONV_HALO, D_MODEL), BF16) \
        + 3 * _nbytes((tm, D_MODEL), F32) + _nbytes((tm, D_MODEL), BF16)
    wide = lambda index: pl.BlockSpec((1, tm, D_MODEL), index)
    half = lambda index: pl.BlockSpec((1, tm, MLA_WIDTH), index)
    const3 = lambda i, j: (0, 0, 0)
    return pl.pallas_call(
        functools.partial(_conv_kernel, tm=tm, pipelined=nt > 1),
        grid=(b, nt),
        in_specs=[wide(tokb), wide(nxtb), half(tokb), half(nxtb), half(tokb), half(nxtb),
                  pl.BlockSpec((1, CONV_HALO, D_MODEL), perb)]
                 + [pl.BlockSpec(a.shape, const3 if a.ndim == 3 else const) for a in consts],
        out_specs=[pl.BlockSpec((1, tm, D_MODEL), tokb), pl.BlockSpec((1, CONV_HALO, D_MODEL), perb)],
        out_shape=[jax.ShapeDtypeStruct((b, t, D_MODEL), F32),
                   jax.ShapeDtypeStruct((b, CONV_HALO, D_MODEL), F32)],
        scratch_shapes=[pltpu.VMEM((ext, D_MODEL), F32), pltpu.VMEM((ext, D_MODEL), BF16),
                        pltpu.VMEM((tm + CONV_HALO, D_MODEL), BF16), pltpu.VMEM((tm, D_MODEL), F32),
                        pltpu.VMEM((tm, D_MODEL), BF16), pltpu.VMEM((tm, D_MODEL), F32),
                        pltpu.VMEM((tm, D_MODEL), F32)],
        compiler_params=pltpu.CompilerParams(
            dimension_semantics=("arbitrary", "arbitrary"),
            vmem_limit_bytes=_vmem_limit(blk, scratch, temp_bytes=8 * _nbytes((tm, D_MODEL), F32))),
        name="conv_layer")(x, x, mla, mla, s5m, s5m, past32, *consts)


def _head_groups(nope, rope, third):
    return jnp.concatenate([nope, rope, third], axis=-1).reshape(nope.shape[0], HEAD_PAD)


def _half_swap(a):
    half = ROPE_DIM // 2
    return jnp.concatenate([a[..., half:], a[..., :half]], axis=-1)


def _lane_pad(a, lo, width=LANES):
    pad = [(0, 0)] * (a.ndim - 1) + [(lo, width - lo - a.shape[-1])]
    return jnp.pad(a, pad)


def _prep_ab_weights(norm_g, w_in, g_q_lat, w_uq, g_kv_lat, w_uk, w_uv, g_q_nope, g_q_rope, g_k_nope, g_k_rope,
                     w_out):
    o_kr = Q_LORA + KV_LORA
    kr_cols = w_in[:, o_kr:o_kr + ROPE_DIM]
    kr_group = jnp.concatenate([kr_cols, _half_swap(kr_cols), jnp.zeros((D_MODEL, LANES - 2 * ROPE_DIM), F32)], -1)
    w_in_p = jnp.concatenate([w_in[:, :o_kr], kr_group, w_in[:, o_kr + ROPE_DIM:]], axis=-1).astype(BF16)
    uq = w_uq.reshape(Q_LORA, MLA_HEADS, NOPE_DIM + ROPE_DIM)
    uq_r = uq[..., NOPE_DIM:]
    w_uq_p = _head_groups(uq[..., :NOPE_DIM], uq_r, _half_swap(uq_r)).astype(BF16)
    zeros_kv = jnp.zeros((KV_LORA, MLA_HEADS, LANES - NOPE_DIM), F32)
    w_uk_p = jnp.concatenate([w_uk, zeros_kv], axis=-1).reshape(KV_LORA, HEAD_PAD).astype(BF16)
    w_uv_t = w_uv.reshape(KV_LORA, MLA_WIDTH).T.astype(BF16)
    r = np.arange(2 * LANES)
    same = (r[:, None] // LANES) == (r[None, :] // LANES)
    ri, ci = r[:, None] % LANES, r[None, :] % LANES
    e2 = np.where(same & (ri < NOPE_DIM) & (ci < NOPE_DIM), 1.0 / NOPE_DIM, 0.0) \
        + np.where(same & (ri >= NOPE_DIM) & (ri < NOPE_DIM + ROPE_DIM) & (ci >= NOPE_DIM), 1.0 / ROPE_DIM, 0.0)
    place = np.zeros((ROPE_DIM, LANES), np.float32)
    place[np.arange(ROPE_DIM), NOPE_DIM + np.arange(ROPE_DIM)] = 1.0
    row = lambda a: a.reshape(1, -1)
    return dict(
        norm_g=row(norm_g), w_in=w_in_p, g_q_lat=row(g_q_lat), w_uq=w_uq_p, g_kv=row(g_kv_lat),
        w_uk=w_uk_p, w_uvt=w_uv_t, e2=jnp.asarray(e2, BF16), place=jnp.asarray(place, BF16),
        gq=row(jnp.concatenate([g_q_nope, g_q_rope, jnp.zeros((ROPE_DIM,), F32)])),
        gqs=row(_lane_pad(_half_swap(g_q_rope), NOPE_DIM)),
        gk=row(_lane_pad(g_k_rope, 0)), gks=row(_lane_pad(_half_swap(g_k_rope), 0)),
        gkn=row(_lane_pad(g_k_nope, 0)), gkn2=row(jnp.tile(_lane_pad(g_k_nope, 0), 2)),
        w_out_a=w_out[:MLA_WIDTH].astype(BF16), w_out_s=w_out[MLA_WIDTH:].astype(BF16))


def _rope_tables(pos, attn_block):
    half = ROPE_DIM // 2
    inv = ROPE_BASE ** (-jnp.arange(half, dtype=F32) / half)
    ang = pos.astype(F32)[:, None] * inv[None, :]
    cos = jnp.cos(ang)
    sin = jnp.sin(ang)
    cosf = jnp.concatenate([cos, cos], axis=-1)
    sinf = jnp.concatenate([-sin, sin], axis=-1)
    ones = jnp.ones((pos.shape[0], NOPE_DIM), F32)
    cq = _lane_pad(jnp.concatenate([ones, cosf], axis=-1), 0)
    sq = _lane_pad(sinf, NOPE_DIM)
    n_t = pos.shape[0]
    if attn_block is None:
        qm = km = jnp.zeros((n_t, LANES), F32)
    else:
        n_chunks = attn_block // CHUNK
        assert MASK_LANE0 + n_chunks <= LANES
        own = (jnp.arange(n_t, dtype=jnp.int32) % attn_block) // CHUNK
        c = jnp.arange(n_chunks, dtype=jnp.int32)
        km = _lane_pad((c[None, :] == own[:, None]).astype(F32), MASK_LANE0)
        qm = _lane_pad(jnp.where(c[None, :] <= own[:, None], 0.0, NEG_INF).astype(F32), MASK_LANE0)
    return cq, sq, _lane_pad(cosf, 0), _lane_pad(sinf, 0), qm, km


def _prep_s5_weights(lam_re, lam_im, log_dt, b_re, b_im, c_re, c_im, d_skip, w_glu, b_glu, *, max_tile):
    pos, neg, lam1, bbar = _s5_prep(lam_re, lam_im, log_dt, b_re, b_im, sub=S5_SUB)
    eye = jnp.eye(S5_HALF_GROUPS, dtype=F32)

    def b_block(bb):
        bb = bb.reshape(S5_GROUP, 2, S5_HALF_GROUPS, S5_STATE)
        return jnp.einsum('nhgp,gk->hgnkp', bb, eye).reshape(2, S5_HALF_GROUPS * S5_GROUP, S5_HALF_STATE)

    def c_block(cc):
        cc = cc.reshape(2, S5_HALF_GROUPS, S5_GROUP, S5_STATE)
        return jnp.einsum('hgnp,gk->hkpgn', cc, eye).reshape(2, S5_HALF_STATE, S5_HALF_GROUPS * S5_GROUP)

    n_t = S5_HALF_STATE // LANES
    b_re_blk, b_im_blk = b_block(bbar[0]), b_block(bbar[1])
    bblk = jnp.stack([b_re_blk.reshape(2, -1, n_t, LANES), b_im_blk.reshape(2, -1, n_t, LANES)], axis=3)
    bblk = bblk.reshape(2, S5_HALF_GROUPS * S5_GROUP, 2 * S5_HALF_STATE).astype(BF16)
    c_re_blk, c_im_blk = c_block(c_re), c_block(-c_im)
    cblk = jnp.stack([c_re_blk.reshape(2, n_t, LANES, -1), c_im_blk.reshape(2, n_t, LANES, -1)], axis=2)
    cblk = cblk.reshape(2, 2 * S5_HALF_STATE, S5_HALF_GROUPS * S5_GROUP).astype(BF16)

    def tiled(tab):
        rows = tab.shape[1]
        return jnp.transpose(tab.reshape(2, rows, 2 * n_t, LANES), (1, 2, 0, 3)).reshape(rows, 4 * S5_HALF_STATE)

    r = np.arange(max_tile)
    tri = ((r[:, None] // S5_SUB) == (r[None, :] // S5_SUB)) & (r[None, :] <= r[:, None])
    return dict(bblk=bblk, cblk=cblk, tri=jnp.asarray(tri, BF16), neg=tiled(neg), pos=tiled(pos),
                lam1=tiled(lam1[:, None, :]),
                d=d_skip.reshape(1, -1), w_glu=w_glu.astype(BF16), b_glu=b_glu.reshape(1, -1))


def _pack_state(s_re, s_im):
    b = s_re.shape[0]
    st = jnp.stack([s_re.reshape(b, -1, LANES), s_im.reshape(b, -1, LANES)], axis=2)
    return st.reshape(b, 1, 4 * S5_HALF_STATE)


def _unpack_state(st):
    b = st.shape[0]
    st = st.reshape(b, -1, 2, LANES)
    return (st[:, :, 0].reshape(b, S5_GROUPS, S5_STATE), st[:, :, 1].reshape(b, S5_GROUPS, S5_STATE))


def _ab_branches(x, pos, past, wa, ws, *, tm_in, tm_s5, tq, tk):
    b, t, _ = x.shape
    x2d = x.reshape(b * t, D_MODEL)
    tabs = _rope_tables(pos, tq if past is None else None)
    lat, kr, qp, kp, vt, gm, u, gs = _ab_in(x2d, tabs, wa, seq_len=t, tm=tm_in)
    r3 = lambda a: a.reshape(b, t, a.shape[-1])
    if past is None:
        mla = _attn_prompt(r3(qp), r3(kp), vt, r3(gm), tq=tq)
        s0 = jnp.zeros((b, 1, 4 * S5_HALF_STATE), F32)
    else:
        past_lat, past_kr, past_re, past_im = past
        vt_b = jnp.transpose(vt.reshape(MLA_WIDTH, b, t), (1, 0, 2))
        mla = _attn_sample(r3(qp), past_lat, past_kr, r3(kp), vt_b, r3(gm), wa, tk=tk)
        s0 = _pack_state(past_re, past_im)
    s5, sfin = _s5(r3(u), r3(gs), s0, ws, tm=tm_s5)
    fin_re, fin_im = _unpack_state(sfin)
    return mla, s5, r3(lat), r3(kr), fin_re, fin_im


def _conv_taps(conv_w):
    by_lag = jnp.pad(conv_w[::-1], ((0, CONV_HALO - conv_w.shape[0]), (0, 0))).astype(BF16)
    return jnp.broadcast_to(by_lag[:, None, :], (CONV_HALO, CONV_PAD, conv_w.shape[1]))


def _conv(x, mla, s5m, past, wc, *, tm):
    b = x.shape[0]
    if past is None:
        past32 = jnp.zeros((b, CONV_HALO, D_MODEL), F32)
    else:
        past32 = jnp.pad(past, ((0, 0), (CONV_HALO - (CONV_WIDTH - 1), 0), (0, 0)))
    y, newc = _conv_layer(x, mla, s5m, past32, wc, tm=tm)
    return y, newc[:, CONV_HALO - (CONV_WIDTH - 1):]


def kernel(x_prompt, x_sample, cache_mla_latent, cache_mla_krope, state_s5_re, state_s5_im, state_conv, norm_ab, w_in_ab, g_q_lat, w_uq, g_kv_lat, w_uk, w_uv, g_q_nope, g_q_rope, g_k_nope, g_k_rope, s5_lam_re, s5_lam_im, s5_log_dt, s5_b_re, s5_b_im, s5_c_re, s5_c_im, s5_d, s5_w_glu, s5_b_glu, w_out_ab, norm_c, w_in_c, conv_w, conv_b, ln_g, ln_b, w_out_c):
    t_p = x_prompt.shape[1]
    t_s = x_sample.shape[1]
    past_len = cache_mla_latent.shape[2]
    pos_p = jnp.arange(t_p, dtype=jnp.int32)
    pos_s = past_len + jnp.arange(t_s, dtype=jnp.int32)

    i = 0
    wa = _prep_ab_weights(norm_ab[i], w_in_ab[i], g_q_lat[i], w_uq[i], g_kv_lat[i], w_uk[i], w_uv[i],
                          g_q_nope[i], g_q_rope[i], g_k_nope[i], g_k_rope[i], w_out_ab[i])
    ws = _prep_s5_weights(s5_lam_re[i], s5_lam_im[i], s5_log_dt[i], s5_b_re[i], s5_b_im[i], s5_c_re[i],
                          s5_c_im[i], s5_d[i], s5_w_glu[i], s5_b_glu[i], max_tile=256)
    mla_p, s5_p, lat_p, kr_p, re_p, im_p = _ab_branches(
        x_prompt, pos_p, None, wa, ws, tm_in=512, tm_s5=256, tq=512, tk=512)
    mla_s, s5_s, lat_s, kr_s, re_s, im_s = _ab_branches(
        x_sample, pos_s, (cache_mla_latent[i], cache_mla_krope[i], state_s5_re[i], state_s5_im[i]), wa, ws,
        tm_in=256, tm_s5=t_s, tq=512, tk=1024)

    row = lambda a: a.reshape(1, -1)
    cw = _conv_taps(conv_w[i])
    wc = dict(w_out_a=wa['w_out_a'], w_out_s=wa['w_out_s'],
              norm_g=row(norm_c[i]), w_in=w_in_c[i].astype(BF16), conv_w=cw, conv_b=row(conv_b[i]),
              ln_g=row(ln_g[i]), ln_b=row(ln_b[i]), w_out=w_out_c[i].astype(BF16))
    yp, conv_p = _conv(x_prompt, mla_p, s5_p, None, wc, tm=256)
    ys, conv_s = _conv(x_sample, mla_s, s5_s, state_conv[i], wc, tm=t_s)

    st = lambda a: a[None]
    return (yp, ys, st(lat_p), st(kr_p), st(re_p), st(im_p), st(conv_p),
            st(lat_s), st(kr_s), st(re_s), st(im_s), st(conv_s))
```

```python
import functools
import math

import numpy as np
import jax
import jax.numpy as jnp
from jax import lax
from jax.experimental import pallas as pl
from jax.experimental.pallas import tpu as pltpu

F32 = jnp.float32
BF16 = jnp.bfloat16

D_MODEL = 1024
CHUNK = 64
MLA_HEADS = 8
Q_LORA = 384
KV_LORA = 256
NOPE_DIM = 64
ROPE_DIM = 32
V_DIM = 64
MLA_WIDTH = MLA_HEADS * V_DIM
ROPE_BASE = 10000.0
ATTN_SCALE = (NOPE_DIM + ROPE_DIM) ** -0.5
S5_WIDTH = 512
S5_GROUP = 16
S5_GROUPS = S5_WIDTH // S5_GROUP
S5_STATE = 64
CONV_WIDTH = 31
EPS = 1e-6
NEG_INF = -1e30
LOG2E = math.log2(math.e)
MASK_LANE0 = NOPE_DIM + ROPE_DIM

LANES = 128
SUBLANES = 8
HEAD_PAD = MLA_HEADS * LANES
S5_HALF_GROUPS = S5_GROUPS // 2
S5_HALF_STATE = S5_HALF_GROUPS * S5_STATE
S5_SUB = 32
CONV_HALO = 32
CONV_PAD = 16
V7X_VMEM_BYTES = 64 * 1024 * 1024


def _vmem_limit(block_bytes, scratch_bytes=0, temp_bytes=0):
    est = 2 * block_bytes + scratch_bytes + temp_bytes
    return int(min(max(est, 16 * 1024 * 1024), V7X_VMEM_BYTES - 8 * 1024 * 1024))


def _nbytes(shape, dtype):
    return int(np.prod(shape)) * jnp.dtype(dtype).itemsize


def _rms(x, g):
    return x * lax.rsqrt(jnp.mean(x * x, axis=-1, keepdims=True) + EPS) * g


def _silu(x):
    return x * jax.nn.sigmoid(x)


def _dot(a, b):
    return jnp.dot(a, b, preferred_element_type=F32)


def _dot_nt(a, b):
    return lax.dot_general(a, b, (((1,), (1,)), ((), ())), preferred_element_type=F32)


def _ab_in_kernel(x_ref, ng_ref, win_ref, gql_ref, wuq_ref, gkv_ref, wuk_ref, wuv_ref, e2_ref,
                  cq_ref, sq_ref, ck_ref, sk_ref, qm_ref, km_ref, gq_ref, gqs_ref, gk_ref, gks_ref, gkn_ref,
                  lat_ref, kr_ref, qp_ref, kp_ref, vt_ref, gm_ref, u_ref, gs_ref):
    x = x_ref[...]
    h = _rms(x, ng_ref[...]).astype(BF16)

    def proj(lo, hi):
        return _dot(h, win_ref[:, lo:hi])

    q_lat = proj(0, Q_LORA)
    c_kv = proj(Q_LORA, Q_LORA + KV_LORA)
    krg = proj(640, 768)
    gm_ref[...] = proj(768, 1280)
    u_ref[...] = proj(1280, 1792)
    gs_ref[...] = proj(1792, 2304)

    c_n = _rms(c_kv, gkv_ref[...])
    lat_ref[...] = c_n
    cb = c_n.astype(BF16)

    lane = lax.broadcasted_iota(jnp.int32, (1, LANES), 1)
    ms = jnp.sum(jnp.where(lane < ROPE_DIM, krg * krg, 0.0), axis=-1, keepdims=True) * (1.0 / ROPE_DIM)
    kr = lax.rsqrt(ms + EPS) * (krg * (ck_ref[...] * gk_ref[...])
                                + pltpu.roll(krg, LANES - ROPE_DIM, 1) * (sk_ref[...] * gks_ref[...]))
    kr_ref[...] = kr[:, :ROPE_DIM]
    kr_mask = pltpu.roll(kr, NOPE_DIM, 1) + km_ref[...]

    qn = _rms(q_lat, gql_ref[...]).astype(BF16)
    qa_tab = cq_ref[...] * (gq_ref[...] * (ATTN_SCALE * LOG2E))
    qb_tab = sq_ref[...] * (gqs_ref[...] * (ATTN_SCALE * LOG2E))
    n_pairs = MLA_HEADS // 2
    pair_cols = [slice(2 * LANES * p, 2 * LANES * (p + 1)) for p in range(n_pairs)]
    qas = [_dot(qn, wuq_ref[:, cols]) for cols in pair_cols]
    kas = [_dot(cb, wuk_ref[:, cols]) for cols in pair_cols]
    vt_ref[...] = _dot_nt(wuv_ref[...], cb).astype(BF16)
    q_ms = [_dot((qa * qa).astype(BF16), e2_ref[...]) for qa in qas]
    k_ms = [_dot((ka * ka).astype(BF16), e2_ref[...]) for ka in kas]
    for p in range(n_pairs):
        lo = 2 * LANES * p
        qs = qas[p] * lax.rsqrt(q_ms[p] + EPS)
        ks = kas[p] * lax.rsqrt(k_ms[p] + EPS)
        for j in range(2):
            sl = slice(LANES * j, LANES * (j + 1))
            s = qs[:, sl]
            qp_ref[:, lo + LANES * j:lo + LANES * (j + 1)] = (
                s * qa_tab + pltpu.roll(s, LANES - ROPE_DIM, 1) * qb_tab + qm_ref[...]).astype(BF16)
            kp_ref[:, lo + LANES * j:lo + LANES * (j + 1)] = (
                ks[:, sl] * gkn_ref[...] + kr_mask).astype(BF16)


def _ab_in(x2d, pos_tabs, w, *, seq_len, tm):
    n_tok = x2d.shape[0]
    if tm > seq_len:
        pos_tabs = tuple(jnp.tile(p, (tm // seq_len, 1)) for p in pos_tabs)
    n_pos = max(seq_len // tm, 1)
    tok = lambda i: (i, 0)
    pos = lambda i: (i % n_pos, 0)
    const = lambda i: (0, 0)

    def full(a):
        return pl.BlockSpec(a.shape, const, pipeline_mode=pl.Buffered(1))

    ins = [x2d, w['norm_g'], w['w_in'], w['g_q_lat'], w['w_uq'], w['g_kv'], w['w_uk'], w['w_uvt'], w['e2'],
           *pos_tabs, w['gq'], w['gqs'], w['gk'], w['gks'], w['gkn']]
    n_tabs = len(pos_tabs)
    in_specs = [pl.BlockSpec((tm, D_MODEL), tok)] + [full(a) for a in ins[1:9]] \
        + [pl.BlockSpec((tm, LANES), pos)] * n_tabs + [full(a) for a in ins[9 + n_tabs:]]
    outs = [((n_tok, KV_LORA), F32), ((n_tok, ROPE_DIM), F32), ((n_tok, HEAD_PAD), BF16),
            ((n_tok, HEAD_PAD), BF16), ((MLA_WIDTH, n_tok), BF16), ((n_tok, MLA_WIDTH), F32),
            ((n_tok, S5_WIDTH), F32), ((n_tok, S5_WIDTH), F32)]
    vt_index = 4
    out_specs = [pl.BlockSpec((MLA_WIDTH, tm), lambda i: (0, i)) if n == vt_index
                 else pl.BlockSpec((tm, s[1]), tok) for n, (s, _) in enumerate(outs)]
    out_shape = [jax.ShapeDtypeStruct(s, d) for s, d in outs]
    blk = sum(_nbytes(a.shape, a.dtype) for a in ins[1:9]) + _nbytes((tm, D_MODEL), F32) \
        + n_tabs * _nbytes((tm, LANES), F32) + sum(_nbytes(s, d) for s, d in outs) * tm // n_tok
    return pl.pallas_call(
        _ab_in_kernel, grid=(n_tok // tm,), in_specs=in_specs, out_specs=out_specs, out_shape=out_shape,
        compiler_params=pltpu.CompilerParams(
            dimension_semantics=("arbitrary",),
            vmem_limit_bytes=_vmem_limit(blk, temp_bytes=8 * _nbytes((tm, D_MODEL), F32))),
        name="ab_in")(*ins)


def _ones_row(n, rows=V_DIM):
    return (lax.broadcasted_iota(jnp.int32, (rows, n), 0) == 0).astype(BF16)


VAUG_ROWS = 2 * V_DIM


def _attn_prompt_kernel(qi_ref, kj_ref, q_ref, k_ref, vt_ref, g_ref, o_ref, vaug_sc, s_sc, acc_sc, *, tq,
                        seq_len):
    n_steps = qi_ref.shape[0]
    ones_row = _ones_row(seq_len, VAUG_ROWS - V_DIM)
    for hh in range(2):
        vaug_sc[hh, 0:V_DIM, :] = vt_ref[hh * V_DIM:(hh + 1) * V_DIM, :]
        vaug_sc[hh, V_DIM:VAUG_ROWS, :] = ones_row
    lane = lax.broadcasted_iota(jnp.int32, (1, LANES), 1)
    mask_lane_off = jnp.logical_or(lane < MASK_LANE0, lane >= MASK_LANE0 + tq // CHUNK)

    def produce(n, slot):
        qi = qi_ref[n]
        kj = kj_ref[n]
        r0 = pl.multiple_of(qi * tq, tq)
        k0 = pl.multiple_of(kj * tq, tq)
        q_keep = jnp.logical_or(mask_lane_off, kj == qi)
        bms = []
        for hh in range(2):
            hs = slice(hh * LANES, (hh + 1) * LANES)
            q = q_ref[0, pl.ds(r0, tq), hs]
            q = jnp.where(q_keep, q, jnp.zeros_like(q))
            s = _dot_nt(k_ref[0, pl.ds(k0, tq), hs], q)
            s_sc[slot, hh] = s
            bms.append(jnp.max(s, axis=0, keepdims=True))
        return tuple(bms)

    def consume(n, slot, bms, state):
        qi = qi_ref[n]
        kj = kj_ref[n]
        k0 = pl.multiple_of(kj * tq, tq)
        out = []
        for hh in range(2):
            m, acc = state[hh]
            m = jnp.where(kj == 0, -jnp.inf, m)
            m_new = jnp.maximum(m, bms[hh])
            p = jnp.exp2(s_sc[slot, hh] - m_new)
            alpha = jnp.exp2(m - m_new)
            acc = alpha * acc + _dot(vaug_sc[hh, :, pl.ds(k0, tq)], p.astype(BF16))
            acc_sc[qi, hh] = acc
            out.append((m_new, acc))
        return tuple(out)

    unroll = 8

    def multi_step(t, carry):
        bms, state = carry
        for u in range(unroll):
            n = unroll * t + u
            next_bms = produce(n + 1, (u + 1) % 2)
            state = consume(n, u % 2, bms, state)
            bms = next_bms
        return bms, state

    state = tuple((jnp.full((1, tq), -jnp.inf, F32), jnp.zeros((VAUG_ROWS, tq), F32)) for _ in range(2))
    n_loops = (n_steps - 1) // unroll
    bms, state = lax.fori_loop(0, n_loops, multi_step, (produce(0, 0), state))
    for n in range(unroll * n_loops, n_steps):
        next_bms = produce(n + 1, (n + 1) % 2) if n + 1 < n_steps else None
        state = consume(n, n % 2, bms, state)
        bms = next_bms

    def finalize(qi, carry):
        r0 = pl.multiple_of(qi * tq, tq)
        vals = []
        for hh in range(2):
            acc = acc_sc[qi, hh]
            vals.append(acc[0:V_DIM, :] * (1.0 / acc[V_DIM:V_DIM + 1, :]))
        o = jnp.concatenate(vals, axis=0).T
        o_ref[0, pl.ds(r0, tq), :] = (o * _silu(g_ref[0, pl.ds(r0, tq), :])).astype(BF16)
        return carry

    lax.fori_loop(0, seq_len // tq, finalize, 0)


def _attn_prompt(qp, kp, vt, gate, *, tq):
    b, t, _ = qp.shape
    nq = t // tq
    steps = [(qi, kj) for qi in range(nq) for kj in range(qi + 1)]
    qi_tab = jnp.asarray([s[0] for s in steps], jnp.int32)
    kj_tab = jnp.asarray([s[1] for s in steps], jnp.int32)
    pair = lambda i, j, qt, kt: (i, 0, j)
    blk = 2 * _nbytes((t, 2 * LANES), BF16) + _nbytes((LANES, t), BF16) + _nbytes((t, LANES), F32) \
        + _nbytes((t, LANES), BF16)
    scratch = _nbytes((2, VAUG_ROWS, t), BF16) + _nbytes((2, 2, tq, tq), F32) + _nbytes((nq, 2, VAUG_ROWS, tq), F32)
    return pl.pallas_call(
        functools.partial(_attn_prompt_kernel, tq=tq, seq_len=t),
        grid_spec=pltpu.PrefetchScalarGridSpec(
            num_scalar_prefetch=2,
            grid=(b, MLA_HEADS // 2),
            in_specs=[pl.BlockSpec((1, t, 2 * LANES), pair), pl.BlockSpec((1, t, 2 * LANES), pair),
                      pl.BlockSpec((LANES, t), lambda i, j, qt, kt: (j, i)), pl.BlockSpec((1, t, LANES), pair)],
            out_specs=pl.BlockSpec((1, t, LANES), pair),
            scratch_shapes=[pltpu.VMEM((2, VAUG_ROWS, t), BF16), pltpu.VMEM((2, 2, tq, tq), F32),
                            pltpu.VMEM((nq, 2, VAUG_ROWS, tq), F32)]),
        out_shape=jax.ShapeDtypeStruct((b, t, MLA_WIDTH), BF16),
        compiler_params=pltpu.CompilerParams(
            dimension_semantics=("arbitrary", "arbitrary"),
            vmem_limit_bytes=_vmem_limit(blk, scratch, temp_bytes=12 * _nbytes((tq, tq), F32))),
        name="attn_prompt")(qi_tab, kj_tab, qp, kp, vt, gate)


def _attn_sample_kernel(q_ref, lat_ref, kr_ref, kpn_ref, vtn_ref, g_ref, wuk_ref, wuvt_ref, e2_ref, gkn2_ref,
                        place_ref, o_ref, qbd_sc, m_sc, acc_sc, *, n_past_blocks):
    j = pl.program_id(1)
    n_pairs = MLA_HEADS // 2
    t = q_ref.shape[1]
    lane = lax.broadcasted_iota(jnp.int32, (1, LANES), 1)

    @pl.when(j == 0)
    def _():
        m_sc[...] = jnp.full(m_sc.shape, -jnp.inf, F32)
        acc_sc[...] = jnp.zeros(acc_sc.shape, F32)
        q = q_ref[0].astype(F32)
        q_t = jnp.concatenate([q, jnp.zeros((LANES - t, HEAD_PAD), F32)], axis=0).T
        for p in range(n_pairs):
            top = q_t[2 * LANES * p:2 * LANES * p + LANES, :]
            bot = pltpu.roll(q_t[2 * LANES * p + LANES:2 * LANES * (p + 1), :], t, 1)
            qbd_sc[p] = jnp.concatenate([top, bot], axis=0).astype(BF16)

    def update_all(kp_pairs, vt_pairs):
        n = kp_pairs[0].shape[0]
        ones_row = _ones_row(n)
        ss = [_dot(kp_pairs[p], qbd_sc[p]) for p in range(n_pairs)]
        for p in range(n_pairs):
            m_old = m_sc[p]
            m_new = jnp.maximum(m_old, jnp.max(ss[p], axis=0, keepdims=True))
            pr = jnp.exp2(ss[p] - m_new)
            alpha = jnp.exp2(m_old - m_new)
            vt_pair = vt_pairs[p]
            vaug = jnp.concatenate([vt_pair[0:V_DIM, :], ones_row, vt_pair[V_DIM:2 * V_DIM, :], ones_row], axis=0)
            acc_sc[p] = alpha * acc_sc[p] + _dot(vaug, pr.astype(BF16))
            m_sc[p] = m_new

    @pl.when(j < n_past_blocks)
    def _():
        latb = lat_ref[0].astype(BF16)
        kr128 = _dot(kr_ref[0].astype(BF16), place_ref[...])
        kr256 = jnp.concatenate([kr128, kr128], axis=1)
        vt_all = _dot_nt(wuvt_ref[...], latb).astype(BF16)
        kas = [_dot(latb, wuk_ref[:, 2 * LANES * p:2 * LANES * (p + 1)]) for p in range(n_pairs)]
        mss = [_dot((ka * ka).astype(BF16), e2_ref[...]) for ka in kas]
        kps = [(ka * lax.rsqrt(ms + EPS) * gkn2_ref[...] + kr256).astype(BF16) for ka, ms in zip(kas, mss)]
        update_all(kps, [vt_all[LANES * p:LANES * (p + 1), :] for p in range(n_pairs)])

    @pl.when(j == n_past_blocks)
    def _():
        update_all([kpn_ref[0, :, 2 * LANES * p:2 * LANES * (p + 1)] for p in range(n_pairs)],
                   [vtn_ref[0, LANES * p:LANES * (p + 1), :] for p in range(n_pairs)])
        for p in range(n_pairs):
            acc_t = acc_sc[p].T
            a0 = acc_t[0:t, 0:LANES]
            a1 = acc_t[t:2 * t, LANES:2 * LANES]
            o0 = a0 * (1.0 / a0[:, V_DIM:V_DIM + 1])
            o1 = a1 * (1.0 / a1[:, V_DIM:V_DIM + 1])
            o = jnp.where(lane < V_DIM, o0, pltpu.roll(o1, V_DIM, 1))
            g = g_ref[0, :, p * LANES:(p + 1) * LANES]
            o_ref[0, :, p * LANES:(p + 1) * LANES] = (o * _silu(g)).astype(BF16)


def _attn_sample(qp, past_lat, past_kr, kp_new, vt_new, gate, w, *, tk):
    b, t, _ = qp.shape
    assert 2 * t == LANES, "two heads' queries share one 128-lane group"
    n_past_blocks = past_lat.shape[1] // tk
    n_pairs = MLA_HEADS // 2
    cur = lambda i, j: (i, 0, 0)
    past = lambda i, j: (i, jnp.minimum(j, n_past_blocks - 1), 0)
    const = lambda i, j: (0, 0)
    consts = [w['w_uk'], w['w_uvt'], w['e2'], w['gkn2'], w['place']]
    blk = _nbytes((tk, KV_LORA), F32) + _nbytes((tk, LANES), F32) + 2 * _nbytes((t, HEAD_PAD), BF16) \
        + _nbytes((MLA_WIDTH, LANES), BF16) + 2 * _nbytes((t, MLA_WIDTH), F32) \
        + sum(_nbytes(a.shape, a.dtype) for a in consts)
    scratch = _nbytes((n_pairs, 2 * LANES, LANES), BF16) + _nbytes((n_pairs, SUBLANES, LANES), F32) \
        + _nbytes((n_pairs, 2 * LANES, LANES), F32)
    return pl.pallas_call(
        functools.partial(_attn_sample_kernel, n_past_blocks=n_past_blocks),
        grid=(b, n_past_blocks + 1),
        in_specs=[pl.BlockSpec((1, t, HEAD_PAD), cur),
                  pl.BlockSpec((1, tk, KV_LORA), past),
                  pl.BlockSpec((1, tk, ROPE_DIM), past),
                  pl.BlockSpec((1, t, HEAD_PAD), cur),
                  pl.BlockSpec((1, MLA_WIDTH, t), cur),
                  pl.BlockSpec((1, t, MLA_WIDTH), cur)] + [pl.BlockSpec(a.shape, const) for a in consts],
        out_specs=pl.BlockSpec((1, t, MLA_WIDTH), cur),
        out_shape=jax.ShapeDtypeStruct((b, t, MLA_WIDTH), BF16),
        scratch_shapes=[pltpu.VMEM((n_pairs, 2 * LANES, LANES), BF16),
                        pltpu.VMEM((n_pairs, 1, LANES), F32),
                        pltpu.VMEM((n_pairs, 2 * LANES, LANES), F32)],
        compiler_params=pltpu.CompilerParams(
            dimension_semantics=("arbitrary", "arbitrary"),
            vmem_limit_bytes=_vmem_limit(blk, scratch, temp_bytes=12 * _nbytes((tk, 2 * LANES), F32))),
        name="attn_sample")(qp, past_lat, past_kr, kp_new, vt_new, gate, *consts)


def _s5_prep_kernel(lr_ref, li_ref, ldt_ref, br_ref, bi_ref, pos_ref, neg_ref, lam1_ref, bbar_ref, *, sub):
    lr = lr_ref[...]
    li = li_ref[...]
    dt = jnp.exp(ldt_ref[...])
    ar = lr * dt
    ai = li * dt
    k = lax.broadcasted_iota(jnp.int32, (sub, 1), 0).astype(F32)
    mag_p = jnp.exp(k * ar)
    mag_n = jnp.exp(-k * ar)
    ang = k * ai
    c = jnp.cos(ang)
    s = jnp.sin(ang)
    pos_ref[0] = mag_p * c
    pos_ref[1] = mag_p * s
    neg_ref[0] = mag_n * c
    neg_ref[1] = -mag_n * s
    mag1 = jnp.exp(ar)
    l1r = mag1 * jnp.cos(ai)
    l1i = mag1 * jnp.sin(ai)
    lam1_ref[0:1, :] = l1r
    lam1_ref[1:2, :] = l1i
    inv = 1.0 / (lr * lr + li * li)
    nr = l1r - 1.0
    cr = (nr * lr + l1i * li) * inv
    ci = (l1i * lr - nr * li) * inv
    br = br_ref[...]
    bi = bi_ref[...]
    bbar_ref[0] = cr * br - ci * bi
    bbar_ref[1] = cr * bi + ci * br


def _s5_prep(lam_re, lam_im, log_dt, b_re, b_im, *, sub):
    gp = S5_GROUPS * S5_STATE
    lr = lam_re.reshape(1, gp)
    li = lam_im.reshape(1, gp)
    ldt = jnp.broadcast_to(log_dt[:, None], (S5_GROUPS, S5_STATE)).reshape(1, gp)
    br = jnp.transpose(b_re, (2, 0, 1)).reshape(S5_GROUP, gp)
    bi = jnp.transpose(b_im, (2, 0, 1)).reshape(S5_GROUP, gp)
    return pl.pallas_call(
        functools.partial(_s5_prep_kernel, sub=sub),
        out_shape=[jax.ShapeDtypeStruct((2, sub, gp), F32), jax.ShapeDtypeStruct((2, sub, gp), F32),
                   jax.ShapeDtypeStruct((2, gp), F32), jax.ShapeDtypeStruct((2, S5_GROUP, gp), F32)],
        name="s5_prep")(lr, li, ldt, br, bi)


def _s5_kernel(u_ref, gs_ref, s0_ref, bblk_ref, cblk_ref, tri_ref, neg_ref, pos_ref, lam1_ref, d_ref,
               wglu_ref, bglu_ref, o_ref, sfin_ref, carry_sc, *, tm, sub):
    ti = pl.program_id(1)
    tile = 2 * LANES
    tiles_per_half = 2 * S5_HALF_STATE // tile

    @pl.when(ti == 0)
    def _():
        carry_sc[...] = s0_ref[0]

    u = u_ref[0]
    ub = u.astype(BF16)
    n_tiles = 2 * tiles_per_half

    def lanes_of(t):
        lo = t * tile
        return slice(lo, lo + LANES), slice(lo + LANES, lo + tile)

    def stage_a(t):
        hf, q = divmod(t, tiles_per_half)
        re, im = lanes_of(t)
        bu = _dot(ub[:, hf * 256:(hf + 1) * 256], bblk_ref[hf, :, q * tile:(q + 1) * tile])
        nre, nim = neg_ref[:, re], neg_ref[:, im]
        xs = []
        for c in range(tm // sub):
            rs = slice(c * sub, (c + 1) * sub)
            bre, bim = bu[rs, :LANES], bu[rs, LANES:]
            xs.append(jnp.concatenate([nre * bre - nim * bim, nre * bim + nim * bre], axis=-1).astype(BF16))
        return jnp.concatenate(xs, axis=0)

    def stage_b(t, x):
        re, im = lanes_of(t)
        tb = tri_ref.shape[0]
        cs = jnp.concatenate([_dot(tri_ref[...], x[i * tb:(i + 1) * tb, :]) for i in range(tm // tb)], axis=0)
        cre, cim = carry_sc[:, re], carry_sc[:, im]
        l1r, l1i = lam1_ref[:, re], lam1_ref[:, im]
        pre, pim = pos_ref[:, re], pos_ref[:, im]
        ss = []
        for c in range(tm // sub):
            rs = slice(c * sub, (c + 1) * sub)
            tre = cs[rs, :LANES] + (l1r * cre - l1i * cim)
            tim = cs[rs, LANES:] + (l1r * cim + l1i * cre)
            sre = pre * tre - pim * tim
            sim = pre * tim + pim * tre
            ss.append(jnp.concatenate([sre, sim], axis=-1).astype(BF16))
            cre = sre[sub - 1:sub, :]
            cim = sim[sub - 1:sub, :]
        carry_sc[:, re] = cre
        carry_sc[:, im] = cim
        return jnp.concatenate(ss, axis=0)

    def stage_c(t, s):
        hf, q = divmod(t, tiles_per_half)
        return _dot(s, cblk_ref[hf, q * tile:(q + 1) * tile, :])

    xs_, ss_ = {}, {}
    accs = [None, None]
    skew = 1
    for step in range(n_tiles + 2 * skew):
        if step < n_tiles:
            xs_[step] = stage_a(step)
        if skew <= step < n_tiles + skew:
            ss_[step - skew] = stage_b(step - skew, xs_.pop(step - skew))
        if step >= 2 * skew:
            t = step - 2 * skew
            part = stage_c(t, ss_.pop(t))
            hf = t // tiles_per_half
            accs[hf] = part if accs[hf] is None else accs[hf] + part
    y = jnp.concatenate(accs, axis=-1) + d_ref[...] * u
    z = jax.nn.gelu(y)
    z = z * jax.nn.sigmoid(_dot(z.astype(BF16), wglu_ref[...]) + bglu_ref[...])
    o_ref[0] = (z * _silu(gs_ref[0])).astype(BF16)

    @pl.when(ti == pl.num_programs(1) - 1)
    def _():
        sfin_ref[0] = carry_sc[...]


def _s5(u, gate, s0, w, *, tm):
    b, t, _ = u.shape
    sub = S5_SUB
    tokb = lambda i, j: (i, j, 0)
    perb = lambda i, j: (i, 0, 0)
    c2 = lambda i, j: (0, 0)
    c3 = lambda i, j: (0, 0, 0)
    tb = min(tm, w['tri'].shape[0])
    tri = w['tri'][:tb, :tb]
    consts = [w['bblk'], w['cblk'], tri, w['neg'], w['pos'], w['lam1'], w['d'], w['w_glu'], w['b_glu']]
    blk = 2 * _nbytes((tm, S5_WIDTH), F32) + _nbytes((tm, S5_WIDTH), BF16) + 2 * _nbytes((1, 4 * S5_HALF_STATE), F32) \
        + sum(_nbytes(a.shape, a.dtype) for a in consts)
    scratch = _nbytes((SUBLANES, 4 * S5_HALF_STATE), F32)
    return pl.pallas_call(
        functools.partial(_s5_kernel, tm=tm, sub=sub),
        grid=(b, t // tm),
        in_specs=[pl.BlockSpec((1, tm, S5_WIDTH), tokb), pl.BlockSpec((1, tm, S5_WIDTH), tokb),
                  pl.BlockSpec((1, 1, 4 * S5_HALF_STATE), perb)]
                 + [pl.BlockSpec(a.shape, c3 if a.ndim == 3 else c2) for a in consts],
        out_specs=[pl.BlockSpec((1, tm, S5_WIDTH), tokb), pl.BlockSpec((1, 1, 4 * S5_HALF_STATE), perb)],
        out_shape=[jax.ShapeDtypeStruct((b, t, S5_WIDTH), BF16),
                   jax.ShapeDtypeStruct((b, 1, 4 * S5_HALF_STATE), F32)],
        scratch_shapes=[pltpu.VMEM((1, 4 * S5_HALF_STATE), F32)],
        compiler_params=pltpu.CompilerParams(
            dimension_semantics=("arbitrary", "arbitrary"),
            vmem_limit_bytes=_vmem_limit(blk, scratch, temp_bytes=16 * _nbytes((tm, 2 * LANES), F32))),
        name="s5_scan")(u, gate, s0, *consts)


def _conv_kernel(x_ref, xn_ref, ma_ref, man_ref, ms_ref, msn_ref, past_ref, woa_ref, wos_ref, ng_ref, win_ref,
                 cw_ref, cb_ref, lng_ref, lnb_ref, wout_ref,
                 shift_ref, y_ref, newc_ref, vext_sc, vbe_sc, vbo_sc, stage_sc, hstage_sc, xstage_sc, yc_sc, *,
                 tm, pipelined):
    ti = pl.program_id(1)
    base = CONV_PAD + CONV_HALO

    def stage_tile(x0, mla, s5m):
        x1 = _dot(mla, woa_ref[...]) + _dot(s5m, wos_ref[...]) + x0
        xstage_sc[...] = x1
        h = _rms(x1, ng_ref[...]).astype(BF16)
        hstage_sc[...] = h
        stage_sc[...] = _dot(h, win_ref[:, 0:D_MODEL]) * jax.nn.sigmoid(_dot(h, win_ref[:, D_MODEL:2 * D_MODEL]))

    @pl.when(ti == 0)
    def _():
        vext_sc[0:CONV_PAD, :] = jnp.zeros((CONV_PAD, D_MODEL), F32)
        vext_sc[CONV_PAD:base, :] = past_ref[0]
        if pipelined:
            stage_tile(x_ref[0], ma_ref[0], ms_ref[0])

    if not pipelined:
        stage_tile(x_ref[0], ma_ref[0], ms_ref[0])
    x = xstage_sc[...]
    h = hstage_sc[...]
    vext_sc[base:base + tm, :] = stage_sc[...]
    if pipelined:
        stage_tile(xn_ref[0], man_ref[0], msn_ref[0])

    vbe_sc[...] = vext_sc[...].astype(BF16)
    vbo_sc[...] = vext_sc[SUBLANES:SUBLANES + tm + CONV_HALO, :].astype(BF16)
    rows = min(tm, 128)
    width = 2 * LANES
    win = rows + CONV_PAD
    for rb in range(tm // rows):
        r0 = rb * rows
        for lb in range(D_MODEL // width):
            ls = slice(lb * width, (lb + 1) * width)
            zs = []
            for r in range(SUBLANES):
                z = None
                for a in range(CONV_HALO // SUBLANES):
                    start = r0 + base - CONV_PAD - SUBLANES * a
                    if a % 2 == 0:
                        src = vbe_sc[start:start + win, ls]
                    else:
                        src = vbo_sc[start - SUBLANES:start - SUBLANES + win, ls]
                    term = src.reshape(win // CONV_PAD, CONV_PAD, width) * cw_ref[SUBLANES * a + r, :, ls][None]
                    z = term if z is None else z + term
                zs.append(z)
            zcat = jnp.concatenate(zs, axis=0).reshape(SUBLANES * win, width)
            yc_sc[r0:r0 + rows, ls] = _dot(shift_ref[...], zcat)

    yc = yc_sc[...] + cb_ref[...]
    mu = jnp.mean(yc, axis=-1, keepdims=True)
    xc = yc - mu
    var = jnp.mean(xc * xc, axis=-1, keepdims=True)
    yn = xc * lax.rsqrt(var + EPS) * lng_ref[...] + lnb_ref[...]
    gate = _dot(h, win_ref[:, 2 * D_MODEL:3 * D_MODEL])
    mixed = (_silu(yn) * _silu(gate)).astype(BF16)
    y_ref[0] = _dot(mixed, wout_ref[...]) + x

    tail = vext_sc[CONV_PAD + tm:base + tm, :]
    vext_sc[CONV_PAD:base, :] = tail
    newc_ref[0] = tail


def _conv_layer(x, mla, s5m, past32, w, *, tm):
    b, t, _ = x.shape
    nt = t // tm
    tokb = lambda i, j: (i, j, 0)
    nxtb = lambda i, j: (i, jnp.minimum(j + 1, nt - 1), 0)
    perb = lambda i, j: (i, 0, 0)
    const = lambda i, j: (0, 0)
    rows = min(tm, 128)
    win = rows + CONV_PAD
    shift = np.zeros((rows, SUBLANES * win), np.float32)
    for r in range(SUBLANES):
        shift[np.arange(rows), r * win + np.arange(rows) + CONV_PAD - r] = 1.0
    consts = [w['w_out_a'], w['w_out_s'], w['norm_g'], w['w_in'], w['conv_w'], w['conv_b'], w['ln_g'], w['ln_b'],
              w['w_out'], jnp.asarray(shift, BF16)]
    blk = 3 * _nbytes((tm, D_MODEL), F32) + 4 * _nbytes((tm, MLA_WIDTH), BF16) \
        + 2 * _nbytes((CONV_HALO, D_MODEL), F32) + sum(_nbytes(a.shape, a.dtype) for a in consts)
    ext = tm + CONV_PAD + CONV_HALO
    scratch = _nbytes((ext, D_MODEL), F32) + _nbytes((ext, D_MODEL), BF16) + _nbytes((tm + CONV_HALO, D_MODEL), BF16) \
        + 3 * _nbytes((tm, D_MODEL), F32) + _nbytes((tm, D_MODEL), BF16)
    wide = lambda index: pl.BlockSpec((1, tm, D_MODEL), index)
    half = lambda index: pl.BlockSpec((1, tm, MLA_WIDTH), index)
    const3 = lambda i, j: (0, 0, 0)
    return pl.pallas_call(
        functools.partial(_conv_kernel, tm=tm, pipelined=nt > 1),
        grid=(b, nt),
        in_specs=[wide(tokb), wide(nxtb), half(tokb), half(nxtb), half(tokb), half(nxtb),
                  pl.BlockSpec((1, CONV_HALO, D_MODEL), perb)]
                 + [pl.BlockSpec(a.shape, const3 if a.ndim == 3 else const) for a in consts],
        out_specs=[pl.BlockSpec((1, tm, D_MODEL), tokb), pl.BlockSpec((1, CONV_HALO, D_MODEL), perb)],
        out_shape=[jax.ShapeDtypeStruct((b, t, D_MODEL), F32),
                   jax.ShapeDtypeStruct((b, CONV_HALO, D_MODEL), F32)],
        scratch_shapes=[pltpu.VMEM((ext, D_MODEL), F32), pltpu.VMEM((ext, D_MODEL), BF16),
                        pltpu.VMEM((tm + CONV_HALO, D_MODEL), BF16), pltpu.VMEM((tm, D_MODEL), F32),
                        pltpu.VMEM((tm, D_MODEL), BF16), pltpu.VMEM((tm, D_MODEL), F32),
                        pltpu.VMEM((tm, D_MODEL), F32)],
        compiler_params=pltpu.CompilerParams(
            dimension_semantics=("arbitrary", "arbitrary"),
            vmem_limit_bytes=_vmem_limit(blk, scratch, temp_bytes=8 * _nbytes((tm, D_MODEL), F32))),
        name="conv_layer")(x, x, mla, mla, s5m, s5m, past32, *consts)


def _head_groups(nope, rope, third):
    return jnp.concatenate([nope, rope, third], axis=-1).reshape(nope.shape[0], HEAD_PAD)


def _half_swap(a):
    half = ROPE_DIM // 2
    return jnp.concatenate([a[..., half:], a[..., :half]], axis=-1)


def _lane_pad(a, lo, width=LANES):
    pad = [(0, 0)] * (a.ndim - 1) + [(lo, width - lo - a.shape[-1])]
    return jnp.pad(a, pad)


def _prep_ab_weights(norm_g, w_in, g_q_lat, w_uq, g_kv_lat, w_uk, w_uv, g_q_nope, g_q_rope, g_k_nope, g_k_rope,
                     w_out):
    o_kr = Q_LORA + KV_LORA
    kr_cols = w_in[:, o_kr:o_kr + ROPE_DIM]
    kr_group = jnp.concatenate([kr_cols, _half_swap(kr_cols), jnp.zeros((D_MODEL, LANES - 2 * ROPE_DIM), F32)], -1)
    w_in_p = jnp.concatenate([w_in[:, :o_kr], kr_group, w_in[:, o_kr + ROPE_DIM:]], axis=-1).astype(BF16)
    uq = w_uq.reshape(Q_LORA, MLA_HEADS, NOPE_DIM + ROPE_DIM)
    uq_r = uq[..., NOPE_DIM:]
    w_uq_p = _head_groups(uq[..., :NOPE_DIM], uq_r, _half_swap(uq_r)).astype(BF16)
    zeros_kv = jnp.zeros((KV_LORA, MLA_HEADS, LANES - NOPE_DIM), F32)
    w_uk_p = jnp.concatenate([w_uk, zeros_kv], axis=-1).reshape(KV_LORA, HEAD_PAD).astype(BF16)
    w_uv_t = w_uv.reshape(KV_LORA, MLA_WIDTH).T.astype(BF16)
    r = np.arange(2 * LANES)
    same = (r[:, None] // LANES) == (r[None, :] // LANES)
    ri, ci = r[:, None] % LANES, r[None, :] % LANES
    e2 = np.where(same & (ri < NOPE_DIM) & (ci < NOPE_DIM), 1.0 / NOPE_DIM, 0.0) \
        + np.where(same & (ri >= NOPE_DIM) & (ri < NOPE_DIM + ROPE_DIM) & (ci >= NOPE_DIM), 1.0 / ROPE_DIM, 0.0)
    place = np.zeros((ROPE_DIM, LANES), np.float32)
    place[np.arange(ROPE_DIM), NOPE_DIM + np.arange(ROPE_DIM)] = 1.0
    row = lambda a: a.reshape(1, -1)
    return dict(
        norm_g=row(norm_g), w_in=w_in_p, g_q_lat=row(g_q_lat), w_uq=w_uq_p, g_kv=row(g_kv_lat),
        w_uk=w_uk_p, w_uvt=w_uv_t, e2=jnp.asarray(e2, BF16), place=jnp.asarray(place, BF16),
        gq=row(jnp.concatenate([g_q_nope, g_q_rope, jnp.zeros((ROPE_DIM,), F32)])),
        gqs=row(_lane_pad(_half_swap(g_q_rope), NOPE_DIM)),
        gk=row(_lane_pad(g_k_rope, 0)), gks=row(_lane_pad(_half_swap(g_k_rope), 0)),
        gkn=row(_lane_pad(g_k_nope, 0)), gkn2=row(jnp.tile(_lane_pad(g_k_nope, 0), 2)),
        w_out_a=w_out[:MLA_WIDTH].astype(BF16), w_out_s=w_out[MLA_WIDTH:].astype(BF16))


def _rope_tables(pos, attn_block):
    half = ROPE_DIM // 2
    inv = ROPE_BASE ** (-jnp.arange(half, dtype=F32) / half)
    ang = pos.astype(F32)[:, None] * inv[None, :]
    cos = jnp.cos(ang)
    sin = jnp.sin(ang)
    cosf = jnp.concatenate([cos, cos], axis=-1)
    sinf = jnp.concatenate([-sin, sin], axis=-1)
    ones = jnp.ones((pos.shape[0], NOPE_DIM), F32)
    cq = _lane_pad(jnp.concatenate([ones, cosf], axis=-1), 0)
    sq = _lane_pad(sinf, NOPE_DIM)
    n_t = pos.shape[0]
    if attn_block is None:
        qm = km = jnp.zeros((n_t, LANES), F32)
    else:
        n_chunks = attn_block // CHUNK
        assert MASK_LANE0 + n_chunks <= LANES
        own = (jnp.arange(n_t, dtype=jnp.int32) % attn_block) // CHUNK
        c = jnp.arange(n_chunks, dtype=jnp.int32)
        km = _lane_pad((c[None, :] == own[:, None]).astype(F32), MASK_LANE0)
        qm = _lane_pad(jnp.where(c[None, :] <= own[:, None], 0.0, NEG_INF).astype(F32), MASK_LANE0)
    return cq, sq, _lane_pad(cosf, 0), _lane_pad(sinf, 0), qm, km


def _prep_s5_weights(lam_re, lam_im, log_dt, b_re, b_im, c_re, c_im, d_skip, w_glu, b_glu, *, max_tile):
    pos, neg, lam1, bbar = _s5_prep(lam_re, lam_im, log_dt, b_re, b_im, sub=S5_SUB)
    eye = jnp.eye(S5_HALF_GROUPS, dtype=F32)

    def b_block(bb):
        bb = bb.reshape(S5_GROUP, 2, S5_HALF_GROUPS, S5_STATE)
        return jnp.einsum('nhgp,gk->hgnkp', bb, eye).reshape(2, S5_HALF_GROUPS * S5_GROUP, S5_HALF_STATE)

    def c_block(cc):
        cc = cc.reshape(2, S5_HALF_GROUPS, S5_GROUP, S5_STATE)
        return jnp.einsum('hgnp,gk->hkpgn', cc, eye).reshape(2, S5_HALF_STATE, S5_HALF_GROUPS * S5_GROUP)

    n_t = S5_HALF_STATE // LANES
    b_re_blk, b_im_blk = b_block(bbar[0]), b_block(bbar[1])
    bblk = jnp.stack([b_re_blk.reshape(2, -1, n_t, LANES), b_im_blk.reshape(2, -1, n_t, LANES)], axis=3)
    bblk = bblk.reshape(2, S5_HALF_GROUPS * S5_GROUP, 2 * S5_HALF_STATE).astype(BF16)
    c_re_blk, c_im_blk = c_block(c_re), c_block(-c_im)
    cblk = jnp.stack([c_re_blk.reshape(2, n_t, LANES, -1), c_im_blk.reshape(2, n_t, LANES, -1)], axis=2)
    cblk = cblk.reshape(2, 2 * S5_HALF_STATE, S5_HALF_GROUPS * S5_GROUP).astype(BF16)

    def tiled(tab):
        rows = tab.shape[1]
        return jnp.transpose(tab.reshape(2, rows, 2 * n_t, LANES), (1, 2, 0, 3)).reshape(rows, 4 * S5_HALF_STATE)

    r = np.arange(max_tile)
    tri = ((r[:, None] // S5_SUB) == (r[None, :] // S5_SUB)) & (r[None, :] <= r[:, None])
    return dict(bblk=bblk, cblk=cblk, tri=jnp.asarray(tri, BF16), neg=tiled(neg), pos=tiled(pos),
                lam1=tiled(lam1[:, None, :]),
                d=d_skip.reshape(1, -1), w_glu=w_glu.astype(BF16), b_glu=b_glu.reshape(1, -1))


def _pack_state(s_re, s_im):
    b = s_re.shape[0]
    st = jnp.stack([s_re.reshape(b, -1, LANES), s_im.reshape(b, -1, LANES)], axis=2)
    return st.reshape(b, 1, 4 * S5_HALF_STATE)


def _unpack_state(st):
    b = st.shape[0]
    st = st.reshape(b, -1, 2, LANES)
    return (st[:, :, 0].reshape(b, S5_GROUPS, S5_STATE), st[:, :, 1].reshape(b, S5_GROUPS, S5_STATE))


def _ab_branches(x, pos, past, wa, ws, *, tm_in, tm_s5, tq, tk):
    b, t, _ = x.shape
    x2d = x.reshape(b * t, D_MODEL)
    tabs = _rope_tables(pos, tq if past is None else None)
    lat, kr, qp, kp, vt, gm, u, gs = _ab_in(x2d, tabs, wa, seq_len=t, tm=tm_in)
    r3 = lambda a: a.reshape(b, t, a.shape[-1])
    if past is None:
        mla = _attn_prompt(r3(qp), r3(kp), vt, r3(gm), tq=tq)
        s0 = jnp.zeros((b, 1, 4 * S5_HALF_STATE), F32)
    else:
        past_lat, past_kr, past_re, past_im = past
        vt_b = jnp.transpose(vt.reshape(MLA_WIDTH, b, t), (1, 0, 2))
        mla = _attn_sample(r3(qp), past_lat, past_kr, r3(kp), vt_b, r3(gm), wa, tk=tk)
        s0 = _pack_state(past_re, past_im)
    s5, sfin = _s5(r3(u), r3(gs), s0, ws, tm=tm_s5)
    fin_re, fin_im = _unpack_state(sfin)
    return mla, s5, r3(lat), r3(kr), fin_re, fin_im


def _conv_taps(conv_w):
    by_lag = jnp.pad(conv_w[::-1], ((0, CONV_HALO - conv_w.shape[0]), (0, 0))).astype(BF16)
    return jnp.broadcast_to(by_lag[:, None, :], (CONV_HALO, CONV_PAD, conv_w.shape[1]))


def _conv(x, mla, s5m, past, wc, *, tm):
    b = x.shape[0]
    if past is None:
        past32 = jnp.zeros((b, CONV_HALO, D_MODEL), F32)
    else:
        past32 = jnp.pad(past, ((0, 0), (CONV_HALO - (CONV_WIDTH - 1), 0), (0, 0)))
    y, newc = _conv_layer(x, mla, s5m, past32, wc, tm=tm)
    return y, newc[:, CONV_HALO - (CONV_WIDTH - 1):]


def kernel(x_prompt, x_sample, cache_mla_latent, cache_mla_krope, state_s5_re, state_s5_im, state_conv, norm_ab, w_in_ab, g_q_lat, w_uq, g_kv_lat, w_uk, w_uv, g_q_nope, g_q_rope, g_k_nope, g_k_rope, s5_lam_re, s5_lam_im, s5_log_dt, s5_b_re, s5_b_im, s5_c_re, s5_c_im, s5_d, s5_w_glu, s5_b_glu, w_out_ab, norm_c, w_in_c, conv_w, conv_b, ln_g, ln_b, w_out_c):
    t_p = x_prompt.shape[1]
    t_s = x_sample.shape[1]
    past_len = cache_mla_latent.shape[2]
    pos_p = jnp.arange(t_p, dtype=jnp.int32)
    pos_s = past_len + jnp.arange(t_s, dtype=jnp.int32)

    i = 0
    wa = _prep_ab_weights(norm_ab[i], w_in_ab[i], g_q_lat[i], w_uq[i], g_kv_lat[i], w_uk[i], w_uv[i],
                          g_q_nope[i], g_q_rope[i], g_k_nope[i], g_k_rope[i], w_out_ab[i])
    ws = _prep_s5_weights(s5_lam_re[i], s5_lam_im[i], s5_log_dt[i], s5_b_re[i], s5_b_im[i], s5_c_re[i],
                          s5_c_im[i], s5_d[i], s5_w_glu[i], s5_b_glu[i], max_tile=256)
    mla_p, s5_p, lat_p, kr_p, re_p, im_p = _ab_branches(
        x_prompt, pos_p, None, wa, ws, tm_in=512, tm_s5=512, tq=512, tk=512)
    mla_s, s5_s, lat_s, kr_s, re_s, im_s = _ab_branches(
        x_sample, pos_s, (cache_mla_latent[i], cache_mla_krope[i], state_s5_re[i], state_s5_im[i]), wa, ws,
        tm_in=256, tm_s5=t_s, tq=512, tk=1024)

    row = lambda a: a.reshape(1, -1)
    cw = _conv_taps(conv_w[i])
    wc = dict(w_out_a=wa['w_out_a'], w_out_s=wa['w_out_s'],
              norm_g=row(norm_c[i]), w_in=w_in_c[i].astype(BF16), conv_w=cw, conv_b=row(conv_b[i]),
              ln_g=row(ln_g[i]), ln_b=row(ln_b[i]), w_out=w_out_c[i].astype(BF16))
    yp, conv_p = _conv(x_prompt, mla_p, s5_p, None, wc, tm=256)
    ys, conv_s = _conv(x_sample, mla_s, s5_s, state_conv[i], wc, tm=t_s)

    st = lambda a: a[None]
    return (yp, ys, st(lat_p), st(kr_p), st(re_p), st(im_p), st(conv_p),
            st(lat_s), st(kr_s), st(re_s), st(im_s), st(conv_s))
```

```python
import functools
import math

import numpy as np
import jax
import jax.numpy as jnp
from jax import lax
from jax.experimental import pallas as pl
from jax.experimental.pallas import tpu as pltpu

F32 = jnp.float32
BF16 = jnp.bfloat16

D_MODEL = 1024
CHUNK = 64
MLA_HEADS = 8
Q_LORA = 384
KV_LORA = 256
NOPE_DIM = 64
ROPE_DIM = 32
V_DIM = 64
MLA_WIDTH = MLA_HEADS * V_DIM
ROPE_BASE = 10000.0
ATTN_SCALE = (NOPE_DIM + ROPE_DIM) ** -0.5
S5_WIDTH = 512
S5_GROUP = 16
S5_GROUPS = S5_WIDTH // S5_GROUP
S5_STATE = 64
CONV_WIDTH = 31
EPS = 1e-6
NEG_INF = -1e30
LOG2E = math.log2(math.e)
MASK_LANE0 = NOPE_DIM + ROPE_DIM

LANES = 128
SUBLANES = 8
HEAD_PAD = MLA_HEADS * LANES
S5_HALF_GROUPS = S5_GROUPS // 2
S5_HALF_STATE = S5_HALF_GROUPS * S5_STATE
S5_SUB = 32
CONV_HALO = 32
CONV_PAD = 16
V7X_VMEM_BYTES = 64 * 1024 * 1024


def _vmem_limit(block_bytes, scratch_bytes=0, temp_bytes=0):
    est = 2 * block_bytes + scratch_bytes + temp_bytes
    return int(min(max(est, 16 * 1024 * 1024), V7X_VMEM_BYTES - 8 * 1024 * 1024))


def _nbytes(shape, dtype):
    return int(np.prod(shape)) * jnp.dtype(dtype).itemsize


def _rms(x, g):
    return x * lax.rsqrt(jnp.mean(x * x, axis=-1, keepdims=True) + EPS) * g


def _silu(x):
    return x * jax.nn.sigmoid(x)


def _dot(a, b):
    return jnp.dot(a, b, preferred_element_type=F32)


def _dot_nt(a, b):
    return lax.dot_general(a, b, (((1,), (1,)), ((), ())), preferred_element_type=F32)


def _ab_in_kernel(x_ref, ng_ref, win_ref, gql_ref, wuq_ref, gkv_ref, wuk_ref, wuv_ref, e2_ref,
                  cq_ref, sq_ref, ck_ref, sk_ref, qm_ref, km_ref, gq_ref, gqs_ref, gk_ref, gks_ref, gkn_ref,
                  lat_ref, kr_ref, qp_ref, kp_ref, vt_ref, gm_ref, u_ref, gs_ref):
    x = x_ref[...]
    h = _rms(x, ng_ref[...]).astype(BF16)

    def proj(lo, hi):
        return _dot(h, win_ref[:, lo:hi])

    q_lat = proj(0, Q_LORA)
    c_kv = proj(Q_LORA, Q_LORA + KV_LORA)
    krg = proj(640, 768)
    gm_ref[...] = proj(768, 1280)
    u_ref[...] = proj(1280, 1792)
    gs_ref[...] = proj(1792, 2304)

    c_n = _rms(c_kv, gkv_ref[...])
    lat_ref[...] = c_n
    cb = c_n.astype(BF16)

    lane = lax.broadcasted_iota(jnp.int32, (1, LANES), 1)
    ms = jnp.sum(jnp.where(lane < ROPE_DIM, krg * krg, 0.0), axis=-1, keepdims=True) * (1.0 / ROPE_DIM)
    kr = lax.rsqrt(ms + EPS) * (krg * (ck_ref[...] * gk_ref[...])
                                + pltpu.roll(krg, LANES - ROPE_DIM, 1) * (sk_ref[...] * gks_ref[...]))
    kr_ref[...] = kr[:, :ROPE_DIM]
    kr_mask = pltpu.roll(kr, NOPE_DIM, 1) + km_ref[...]

    qn = _rms(q_lat, gql_ref[...]).astype(BF16)
    qa_tab = cq_ref[...] * (gq_ref[...] * (ATTN_SCALE * LOG2E))
    qb_tab = sq_ref[...] * (gqs_ref[...] * (ATTN_SCALE * LOG2E))
    n_pairs = MLA_HEADS // 2
    pair_cols = [slice(2 * LANES * p, 2 * LANES * (p + 1)) for p in range(n_pairs)]
    qas = [_dot(qn, wuq_ref[:, cols]) for cols in pair_cols]
    kas = [_dot(cb, wuk_ref[:, cols]) for cols in pair_cols]
    vt_ref[...] = _dot_nt(wuv_ref[...], cb).astype(BF16)
    q_ms = [_dot((qa * qa).astype(BF16), e2_ref[...]) for qa in qas]
    k_ms = [_dot((ka * ka).astype(BF16), e2_ref[...]) for ka in kas]
    for p in range(n_pairs):
        lo = 2 * LANES * p
        qs = qas[p] * lax.rsqrt(q_ms[p] + EPS)
        ks = kas[p] * lax.rsqrt(k_ms[p] + EPS)
        for j in range(2):
            sl = slice(LANES * j, LANES * (j + 1))
            s = qs[:, sl]
            qp_ref[:, lo + LANES * j:lo + LANES * (j + 1)] = (
                s * qa_tab + pltpu.roll(s, LANES - ROPE_DIM, 1) * qb_tab + qm_ref[...]).astype(BF16)
            kp_ref[:, lo + LANES * j:lo + LANES * (j + 1)] = (
                ks[:, sl] * gkn_ref[...] + kr_mask).astype(BF16)


def _ab_in(x2d, pos_tabs, w, *, seq_len, tm):
    n_tok = x2d.shape[0]
    if tm > seq_len:
        pos_tabs = tuple(jnp.tile(p, (tm // seq_len, 1)) for p in pos_tabs)
    n_pos = max(seq_len // tm, 1)
    tok = lambda i: (i, 0)
    pos = lambda i: (i % n_pos, 0)
    const = lambda i: (0, 0)

    def full(a):
        return pl.BlockSpec(a.shape, const, pipeline_mode=pl.Buffered(1))

    ins = [x2d, w['norm_g'], w['w_in'], w['g_q_lat'], w['w_uq'], w['g_kv'], w['w_uk'], w['w_uvt'], w['e2'],
           *pos_tabs, w['gq'], w['gqs'], w['gk'], w['gks'], w['gkn']]
    n_tabs = len(pos_tabs)
    in_specs = [pl.BlockSpec((tm, D_MODEL), tok)] + [full(a) for a in ins[1:9]] \
        + [pl.BlockSpec((tm, LANES), pos)] * n_tabs + [full(a) for a in ins[9 + n_tabs:]]
    outs = [((n_tok, KV_LORA), F32), ((n_tok, ROPE_DIM), F32), ((n_tok, HEAD_PAD), BF16),
            ((n_tok, HEAD_PAD), BF16), ((MLA_WIDTH, n_tok), BF16), ((n_tok, MLA_WIDTH), F32),
            ((n_tok, S5_WIDTH), F32), ((n_tok, S5_WIDTH), F32)]
    vt_index = 4
    out_specs = [pl.BlockSpec((MLA_WIDTH, tm), lambda i: (0, i)) if n == vt_index
                 else pl.BlockSpec((tm, s[1]), tok) for n, (s, _) in enumerate(outs)]
    out_shape = [jax.ShapeDtypeStruct(s, d) for s, d in outs]
    blk = sum(_nbytes(a.shape, a.dtype) for a in ins[1:9]) + _nbytes((tm, D_MODEL), F32) \
        + n_tabs * _nbytes((tm, LANES), F32) + sum(_nbytes(s, d) for s, d in outs) * tm // n_tok
    return pl.pallas_call(
        _ab_in_kernel, grid=(n_tok // tm,), in_specs=in_specs, out_specs=out_specs, out_shape=out_shape,
        compiler_params=pltpu.CompilerParams(
            dimension_semantics=("arbitrary",),
            vmem_limit_bytes=_vmem_limit(blk, temp_bytes=8 * _nbytes((tm, D_MODEL), F32))),
        name="ab_in")(*ins)


def _ones_row(n, rows=V_DIM):
    return (lax.broadcasted_iota(jnp.int32, (rows, n), 0) == 0).astype(BF16)


VAUG_ROWS = 2 * V_DIM


def _attn_prompt_kernel(qi_ref, kj_ref, q_ref, k_ref, vt_ref, g_ref, o_ref, vaug_sc, s_sc, acc_sc, *, tq,
                        seq_len):
    n_steps = qi_ref.shape[0]
    ones_row = _ones_row(seq_len, VAUG_ROWS - V_DIM)
    for hh in range(2):
        vaug_sc[hh, 0:V_DIM, :] = vt_ref[hh * V_DIM:(hh + 1) * V_DIM, :]
        vaug_sc[hh, V_DIM:VAUG_ROWS, :] = ones_row
    lane = lax.broadcasted_iota(jnp.int32, (1, LANES), 1)
    mask_lane_off = jnp.logical_or(lane < MASK_LANE0, lane >= MASK_LANE0 + tq // CHUNK)

    def produce(n, slot):
        qi = qi_ref[n]
        kj = kj_ref[n]
        r0 = pl.multiple_of(qi * tq, tq)
        k0 = pl.multiple_of(kj * tq, tq)
        q_keep = jnp.logical_or(mask_lane_off, kj == qi)
        bms = []
        for hh in range(2):
            hs = slice(hh * LANES, (hh + 1) * LANES)
            q = q_ref[0, pl.ds(r0, tq), hs]
            q = jnp.where(q_keep, q, jnp.zeros_like(q))
            s = _dot_nt(k_ref[0, pl.ds(k0, tq), hs], q)
            s_sc[slot, hh] = s
            bms.append(jnp.max(s, axis=0, keepdims=True))
        return tuple(bms)

    def consume(n, slot, bms, state):
        qi = qi_ref[n]
        kj = kj_ref[n]
        k0 = pl.multiple_of(kj * tq, tq)
        out = []
        for hh in range(2):
            m, acc = state[hh]
            m = jnp.where(kj == 0, -jnp.inf, m)
            m_new = jnp.maximum(m, bms[hh])
            p = jnp.exp2(s_sc[slot, hh] - m_new)
            alpha = jnp.exp2(m - m_new)
            acc = alpha * acc + _dot(vaug_sc[hh, :, pl.ds(k0, tq)], p.astype(BF16))
            acc_sc[qi, hh] = acc
            out.append((m_new, acc))
        return tuple(out)

    unroll = 8

    def multi_step(t, carry):
        bms, state = carry
        for u in range(unroll):
            n = unroll * t + u
            next_bms = produce(n + 1, (u + 1) % 2)
            state = consume(n, u % 2, bms, state)
            bms = next_bms
        return bms, state

    state = tuple((jnp.full((1, tq), -jnp.inf, F32), jnp.zeros((VAUG_ROWS, tq), F32)) for _ in range(2))
    n_loops = (n_steps - 1) // unroll
    bms, state = lax.fori_loop(0, n_loops, multi_step, (produce(0, 0), state))
    for n in range(unroll * n_loops, n_steps):
        next_bms = produce(n + 1, (n + 1) % 2) if n + 1 < n_steps else None
        state = consume(n, n % 2, bms, state)
        bms = next_bms

    def finalize(qi, carry):
        r0 = pl.multiple_of(qi * tq, tq)
        vals = []
        for hh in range(2):
            acc = acc_sc[qi, hh]
            vals.append(acc[0:V_DIM, :] * (1.0 / acc[V_DIM:V_DIM + 1, :]))
        o = jnp.concatenate(vals, axis=0).T
        o_ref[0, pl.ds(r0, tq), :] = (o * _silu(g_ref[0, pl.ds(r0, tq), :])).astype(BF16)
        return carry

    lax.fori_loop(0, seq_len // tq, finalize, 0)


def _attn_prompt(qp, kp, vt, gate, *, tq):
    b, t, _ = qp.shape
    nq = t // tq
    steps = [(qi, kj) for qi in range(nq) for kj in range(qi + 1)]
    qi_tab = jnp.asarray([s[0] for s in steps], jnp.int32)
    kj_tab = jnp.asarray([s[1] for s in steps], jnp.int32)
    pair = lambda i, j, qt, kt: (i, 0, j)
    blk = 2 * _nbytes((t, 2 * LANES), BF16) + _nbytes((LANES, t), BF16) + _nbytes((t, LANES), F32) \
        + _nbytes((t, LANES), BF16)
    scratch = _nbytes((2, VAUG_ROWS, t), BF16) + _nbytes((2, 2, tq, tq), F32) + _nbytes((nq, 2, VAUG_ROWS, tq), F32)
    return pl.pallas_call(
        functools.partial(_attn_prompt_kernel, tq=tq, seq_len=t),
        grid_spec=pltpu.PrefetchScalarGridSpec(
            num_scalar_prefetch=2,
            grid=(b, MLA_HEADS // 2),
            in_specs=[pl.BlockSpec((1, t, 2 * LANES), pair), pl.BlockSpec((1, t, 2 * LANES), pair),
                      pl.BlockSpec((LANES, t), lambda i, j, qt, kt: (j, i)), pl.BlockSpec((1, t, LANES), pair)],
            out_specs=pl.BlockSpec((1, t, LANES), pair),
            scratch_shapes=[pltpu.VMEM((2, VAUG_ROWS, t), BF16), pltpu.VMEM((2, 2, tq, tq), F32),
                            pltpu.VMEM((nq, 2, VAUG_ROWS, tq), F32)]),
        out_shape=jax.ShapeDtypeStruct((b, t, MLA_WIDTH), BF16),
        compiler_params=pltpu.CompilerParams(
            dimension_semantics=("arbitrary", "arbitrary"),
            vmem_limit_bytes=_vmem_limit(blk, scratch, temp_bytes=12 * _nbytes((tq, tq), F32))),
        name="attn_prompt")(qi_tab, kj_tab, qp, kp, vt, gate)


def _attn_sample_kernel(q_ref, lat_ref, kr_ref, kpn_ref, vtn_ref, g_ref, wuk_ref, wuvt_ref, e2_ref, gkn2_ref,
                        place_ref, o_ref, qbd_sc, m_sc, acc_sc, *, n_past_blocks):
    j = pl.program_id(1)
    n_pairs = MLA_HEADS // 2
    t = q_ref.shape[1]
    lane = lax.broadcasted_iota(jnp.int32, (1, LANES), 1)

    @pl.when(j == 0)
    def _():
        m_sc[...] = jnp.full(m_sc.shape, -jnp.inf, F32)
        acc_sc[...] = jnp.zeros(acc_sc.shape, F32)
        q = q_ref[0].astype(F32)
        q_t = jnp.concatenate([q, jnp.zeros((LANES - t, HEAD_PAD), F32)], axis=0).T
        for p in range(n_pairs):
            top = q_t[2 * LANES * p:2 * LANES * p + LANES, :]
            bot = pltpu.roll(q_t[2 * LANES * p + LANES:2 * LANES * (p + 1), :], t, 1)
            qbd_sc[p] = jnp.concatenate([top, bot], axis=0).astype(BF16)

    def update_all(kp_pairs, vt_pairs):
        n = kp_pairs[0].shape[0]
        ones_row = _ones_row(n)
        ss = [_dot(kp_pairs[p], qbd_sc[p]) for p in range(n_pairs)]
        for p in range(n_pairs):
            m_old = m_sc[p]
            m_new = jnp.maximum(m_old, jnp.max(ss[p], axis=0, keepdims=True))
            pr = jnp.exp2(ss[p] - m_new)
            alpha = jnp.exp2(m_old - m_new)
            vt_pair = vt_pairs[p]
            vaug = jnp.concatenate([vt_pair[0:V_DIM, :], ones_row, vt_pair[V_DIM:2 * V_DIM, :], ones_row], axis=0)
            acc_sc[p] = alpha * acc_sc[p] + _dot(vaug, pr.astype(BF16))
            m_sc[p] = m_new

    @pl.when(j < n_past_blocks)
    def _():
        latb = lat_ref[0].astype(BF16)
        kr128 = _dot(kr_ref[0].astype(BF16), place_ref[...])
        kr256 = jnp.concatenate([kr128, kr128], axis=1)
        vt_all = _dot_nt(wuvt_ref[...], latb).astype(BF16)
        kas = [_dot(latb, wuk_ref[:, 2 * LANES * p:2 * LANES * (p + 1)]) for p in range(n_pairs)]
        mss = [_dot((ka * ka).astype(BF16), e2_ref[...]) for ka in kas]
        kps = [(ka * lax.rsqrt(ms + EPS) * gkn2_ref[...] + kr256).astype(BF16) for ka, ms in zip(kas, mss)]
        update_all(kps, [vt_all[LANES * p:LANES * (p + 1), :] for p in range(n_pairs)])

    @pl.when(j == n_past_blocks)
    def _():
        update_all([kpn_ref[0, :, 2 * LANES * p:2 * LANES * (p + 1)] for p in range(n_pairs)],
                   [vtn_ref[0, LANES * p:LANES * (p + 1), :] for p in range(n_pairs)])
        for p in range(n_pairs):
            acc_t = acc_sc[p].T
            a0 = acc_t[0:t, 0:LANES]
            a1 = acc_t[t:2 * t, LANES:2 * LANES]
            o0 = a0 * (1.0 / a0[:, V_DIM:V_DIM + 1])
            o1 = a1 * (1.0 / a1[:, V_DIM:V_DIM + 1])
            o = jnp.where(lane < V_DIM, o0, pltpu.roll(o1, V_DIM, 1))
            g = g_ref[0, :, p * LANES:(p + 1) * LANES]
            o_ref[0, :, p * LANES:(p + 1) * LANES] = (o * _silu(g)).astype(BF16)


def _attn_sample(qp, past_lat, past_kr, kp_new, vt_new, gate, w, *, tk):
    b, t, _ = qp.shape
    assert 2 * t == LANES, "two heads' queries share one 128-lane group"
    n_past_blocks = past_lat.shape[1] // tk
    n_pairs = MLA_HEADS // 2
    cur = lambda i, j: (i, 0, 0)
    past = lambda i, j: (i, jnp.minimum(j, n_past_blocks - 1), 0)
    const = lambda i, j: (0, 0)
    consts = [w['w_uk'], w['w_uvt'], w['e2'], w['gkn2'], w['place']]
    blk = _nbytes((tk, KV_LORA), F32) + _nbytes((tk, LANES), F32) + 2 * _nbytes((t, HEAD_PAD), BF16) \
        + _nbytes((MLA_WIDTH, LANES), BF16) + 2 * _nbytes((t, MLA_WIDTH), F32) \
        + sum(_nbytes(a.shape, a.dtype) for a in consts)
    scratch = _nbytes((n_pairs, 2 * LANES, LANES), BF16) + _nbytes((n_pairs, SUBLANES, LANES), F32) \
        + _nbytes((n_pairs, 2 * LANES, LANES), F32)
    return pl.pallas_call(
        functools.partial(_attn_sample_kernel, n_past_blocks=n_past_blocks),
        grid=(b, n_past_blocks + 1),
        in_specs=[pl.BlockSpec((1, t, HEAD_PAD), cur),
                  pl.BlockSpec((1, tk, KV_LORA), past),
                  pl.BlockSpec((1, tk, ROPE_DIM), past),
                  pl.BlockSpec((1, t, HEAD_PAD), cur),
                  pl.BlockSpec((1, MLA_WIDTH, t), cur),
                  pl.BlockSpec((1, t, MLA_WIDTH), cur)] + [pl.BlockSpec(a.shape, const) for a in consts],
        out_specs=pl.BlockSpec((1, t, MLA_WIDTH), cur),
        out_shape=jax.ShapeDtypeStruct((b, t, MLA_WIDTH), BF16),
        scratch_shapes=[pltpu.VMEM((n_pairs, 2 * LANES, LANES), BF16),
                        pltpu.VMEM((n_pairs, 1, LANES), F32),
                        pltpu.VMEM((n_pairs, 2 * LANES, LANES), F32)],
        compiler_params=pltpu.CompilerParams(
            dimension_semantics=("arbitrary", "arbitrary"),
            vmem_limit_bytes=_vmem_limit(blk, scratch, temp_bytes=12 * _nbytes((tk, 2 * LANES), F32))),
        name="attn_sample")(qp, past_lat, past_kr, kp_new, vt_new, gate, *consts)


def _s5_prep_kernel(lr_ref, li_ref, ldt_ref, br_ref, bi_ref, pos_ref, neg_ref, lam1_ref, bbar_ref, *, sub):
    lr = lr_ref[...]
    li = li_ref[...]
    dt = jnp.exp(ldt_ref[...])
    ar = lr * dt
    ai = li * dt
    k = lax.broadcasted_iota(jnp.int32, (sub, 1), 0).astype(F32)
    mag_p = jnp.exp(k * ar)
    mag_n = jnp.exp(-k * ar)
    ang = k * ai
    c = jnp.cos(ang)
    s = jnp.sin(ang)
    pos_ref[0] = mag_p * c
    pos_ref[1] = mag_p * s
    neg_ref[0] = mag_n * c
    neg_ref[1] = -mag_n * s
    mag1 = jnp.exp(ar)
    l1r = mag1 * jnp.cos(ai)
    l1i = mag1 * jnp.sin(ai)
    lam1_ref[0:1, :] = l1r
    lam1_ref[1:2, :] = l1i
    inv = 1.0 / (lr * lr + li * li)
    nr = l1r - 1.0
    cr = (nr * lr + l1i * li) * inv
    ci = (l1i * lr - nr * li) * inv
    br = br_ref[...]
    bi = bi_ref[...]
    bbar_ref[0] = cr * br - ci * bi
    bbar_ref[1] = cr * bi + ci * br


def _s5_prep(lam_re, lam_im, log_dt, b_re, b_im, *, sub):
    gp = S5_GROUPS * S5_STATE
    lr = lam_re.reshape(1, gp)
    li = lam_im.reshape(1, gp)
    ldt = jnp.broadcast_to(log_dt[:, None], (S5_GROUPS, S5_STATE)).reshape(1, gp)
    br = jnp.transpose(b_re, (2, 0, 1)).reshape(S5_GROUP, gp)
    bi = jnp.transpose(b_im, (2, 0, 1)).reshape(S5_GROUP, gp)
    return pl.pallas_call(
        functools.partial(_s5_prep_kernel, sub=sub),
        out_shape=[jax.ShapeDtypeStruct((2, sub, gp), F32), jax.ShapeDtypeStruct((2, sub, gp), F32),
                   jax.ShapeDtypeStruct((2, gp), F32), jax.ShapeDtypeStruct((2, S5_GROUP, gp), F32)],
        name="s5_prep")(lr, li, ldt, br, bi)


def _s5_kernel(u_ref, gs_ref, s0_ref, bblk_ref, cblk_ref, tri_ref, neg_ref, pos_ref, lam1_ref, d_ref,
               wglu_ref, bglu_ref, o_ref, sfin_ref, carry_sc, *, tm, sub):
    ti = pl.program_id(1)
    tile = 2 * LANES
    tiles_per_half = 2 * S5_HALF_STATE // tile

    @pl.when(ti == 0)
    def _():
        carry_sc[...] = s0_ref[0]

    u = u_ref[0]
    ub = u.astype(BF16)
    n_tiles = 2 * tiles_per_half

    def lanes_of(t):
        lo = t * tile
        return slice(lo, lo + LANES), slice(lo + LANES, lo + tile)

    def stage_a(t):
        hf, q = divmod(t, tiles_per_half)
        re, im = lanes_of(t)
        bu = _dot(ub[:, hf * 256:(hf + 1) * 256], bblk_ref[hf, :, q * tile:(q + 1) * tile])
        nre, nim = neg_ref[:, re], neg_ref[:, im]
        xs = []
        for c in range(tm // sub):
            rs = slice(c * sub, (c + 1) * sub)
            bre, bim = bu[rs, :LANES], bu[rs, LANES:]
            xs.append(jnp.concatenate([nre * bre - nim * bim, nre * bim + nim * bre], axis=-1).astype(BF16))
        return jnp.concatenate(xs, axis=0)

    def stage_b(t, x):
        re, im = lanes_of(t)
        tb = tri_ref.shape[0]
        cs = jnp.concatenate([_dot(tri_ref[...], x[i * tb:(i + 1) * tb, :]) for i in range(tm // tb)], axis=0)
        cre, cim = carry_sc[:, re], carry_sc[:, im]
        l1r, l1i = lam1_ref[:, re], lam1_ref[:, im]
        pre, pim = pos_ref[:, re], pos_ref[:, im]
        ss = []
        for c in range(tm // sub):
            rs = slice(c * sub, (c + 1) * sub)
            tre = cs[rs, :LANES] + (l1r * cre - l1i * cim)
            tim = cs[rs, LANES:] + (l1r * cim + l1i * cre)
            sre = pre * tre - pim * tim
            sim = pre * tim + pim * tre
            ss.append(jnp.concatenate([sre, sim], axis=-1).astype(BF16))
            cre = sre[sub - 1:sub, :]
            cim = sim[sub - 1:sub, :]
        carry_sc[:, re] = cre
        carry_sc[:, im] = cim
        return jnp.concatenate(ss, axis=0)

    def stage_c(t, s):
        hf, q = divmod(t, tiles_per_half)
        return _dot(s, cblk_ref[hf, q * tile:(q + 1) * tile, :])

    xs_, ss_ = {}, {}
    accs = [None, None]
    skew = 1
    for step in range(n_tiles + 2 * skew):
        if step < n_tiles:
            xs_[step] = stage_a(step)
        if skew <= step < n_tiles + skew:
            ss_[step - skew] = stage_b(step - skew, xs_.pop(step - skew))
        if step >= 2 * skew:
            t = step - 2 * skew
            part = stage_c(t, ss_.pop(t))
            hf = t // tiles_per_half
            accs[hf] = part if accs[hf] is None else accs[hf] + part
    y = jnp.concatenate(accs, axis=-1) + d_ref[...] * u
    z = jax.nn.gelu(y)
    z = z * jax.nn.sigmoid(_dot(z.astype(BF16), wglu_ref[...]) + bglu_ref[...])
    o_ref[0] = (z * _silu(gs_ref[0])).astype(BF16)

    @pl.when(ti == pl.num_programs(1) - 1)
    def _():
        sfin_ref[0] = carry_sc[...]


def _s5(u, gate, s0, w, *, tm):
    b, t, _ = u.shape
    sub = S5_SUB
    tokb = lambda i, j: (i, j, 0)
    perb = lambda i, j: (i, 0, 0)
    c2 = lambda i, j: (0, 0)
    c3 = lambda i, j: (0, 0, 0)
    tb = min(tm, w['tri'].shape[0])
    tri = w['tri'][:tb, :tb]
    consts = [w['bblk'], w['cblk'], tri, w['neg'], w['pos'], w['lam1'], w['d'], w['w_glu'], w['b_glu']]
    blk = 2 * _nbytes((tm, S5_WIDTH), F32) + _nbytes((tm, S5_WIDTH), BF16) + 2 * _nbytes((1, 4 * S5_HALF_STATE), F32) \
        + sum(_nbytes(a.shape, a.dtype) for a in consts)
    scratch = _nbytes((SUBLANES, 4 * S5_HALF_STATE), F32)
    return pl.pallas_call(
        functools.partial(_s5_kernel, tm=tm, sub=sub),
        grid=(b, t // tm),
        in_specs=[pl.BlockSpec((1, tm, S5_WIDTH), tokb), pl.BlockSpec((1, tm, S5_WIDTH), tokb),
                  pl.BlockSpec((1, 1, 4 * S5_HALF_STATE), perb)]
                 + [pl.BlockSpec(a.shape, c3 if a.ndim == 3 else c2) for a in consts],
        out_specs=[pl.BlockSpec((1, tm, S5_WIDTH), tokb), pl.BlockSpec((1, 1, 4 * S5_HALF_STATE), perb)],
        out_shape=[jax.ShapeDtypeStruct((b, t, S5_WIDTH), BF16),
                   jax.ShapeDtypeStruct((b, 1, 4 * S5_HALF_STATE), F32)],
        scratch_shapes=[pltpu.VMEM((1, 4 * S5_HALF_STATE), F32)],
        compiler_params=pltpu.CompilerParams(
            dimension_semantics=("arbitrary", "arbitrary"),
            vmem_limit_bytes=_vmem_limit(blk, scratch, temp_bytes=16 * _nbytes((tm, 2 * LANES), F32))),
        name="s5_scan")(u, gate, s0, *consts)


def _conv_kernel(x_ref, xn_ref, ma_ref, man_ref, ms_ref, msn_ref, past_ref, woa_ref, wos_ref, ng_ref, win_ref,
                 cw_ref, cb_ref, lng_ref, lnb_ref, wout_ref,
                 shift_ref, y_ref, newc_ref, vext_sc, vbe_sc, vbo_sc, stage_sc, hstage_sc, xstage_sc, yc_sc, *,
                 tm, pipelined):
    ti = pl.program_id(1)
    base = CONV_PAD + CONV_HALO

    def stage_tile(x0, mla, s5m):
        x1 = _dot(mla, woa_ref[...]) + _dot(s5m, wos_ref[...]) + x0
        xstage_sc[...] = x1
        h = _rms(x1, ng_ref[...]).astype(BF16)
        hstage_sc[...] = h
        stage_sc[...] = _dot(h, win_ref[:, 0:D_MODEL]) * jax.nn.sigmoid(_dot(h, win_ref[:, D_MODEL:2 * D_MODEL]))

    @pl.when(ti == 0)
    def _():
        vext_sc[0:CONV_PAD, :] = jnp.zeros((CONV_PAD, D_MODEL), F32)
        vext_sc[CONV_PAD:base, :] = past_ref[0]
        if pipelined:
            stage_tile(x_ref[0], ma_ref[0], ms_ref[0])

    if not pipelined:
        stage_tile(x_ref[0], ma_ref[0], ms_ref[0])
    x = xstage_sc[...]
    h = hstage_sc[...]
    vext_sc[base:base + tm, :] = stage_sc[...]
    if pipelined:
        stage_tile(xn_ref[0], man_ref[0], msn_ref[0])

    vbe_sc[...] = vext_sc[...].astype(BF16)
    vbo_sc[...] = vext_sc[SUBLANES:SUBLANES + tm + CONV_HALO, :].astype(BF16)
    rows = min(tm, 128)
    width = 2 * LANES
    win = rows + CONV_PAD
    for rb in range(tm // rows):
        r0 = rb * rows
        for lb in range(D_MODEL // width):
            ls = slice(lb * width, (lb + 1) * width)
            zs = []
            for r in range(SUBLANES):
                z = None
                for a in range(CONV_HALO // SUBLANES):
                    start = r0 + base - CONV_PAD - SUBLANES * a
                    if a % 2 == 0:
                        src = vbe_sc[start:start + win, ls]
                    else:
                        src = vbo_sc[start - SUBLANES:start - SUBLANES + win, ls]
                    term = src.reshape(win // CONV_PAD, CONV_PAD, width) * cw_ref[SUBLANES * a + r, :, ls][None]
                    z = term if z is None else z + term
                zs.append(z)
            zcat = jnp.concatenate(zs, axis=0).reshape(SUBLANES * win, width)
            yc_sc[r0:r0 + rows, ls] = _dot(shift_ref[...], zcat)

    yc = yc_sc[...] + cb_ref[...]
    mu = jnp.mean(yc, axis=-1, keepdims=True)
    xc = yc - mu
    var = jnp.mean(xc * xc, axis=-1, keepdims=True)
    yn = xc * lax.rsqrt(var + EPS) * lng_ref[...] + lnb_ref[...]
    gate = _dot(h, win_ref[:, 2 * D_MODEL:3 * D_MODEL])
    mixed = (_silu(yn) * _silu(gate)).astype(BF16)
    y_ref[0] = _dot(mixed, wout_ref[...]) + x

    tail = vext_sc[CONV_PAD + tm:base + tm, :]
    vext_sc[CONV_PAD:base, :] = tail
    newc_ref[0] = tail


def _conv_layer(x, mla, s5m, past32, w, *, tm):
    b, t, _ = x.shape
    nt = t // tm
    tokb = lambda i, j: (i, j, 0)
    nxtb = lambda i, j: (i, jnp.minimum(j + 1, nt - 1), 0)
    perb = lambda i, j: (i, 0, 0)
    const = lambda i, j: (0, 0)
    rows = min(tm, 128)
    win = rows + CONV_PAD
    shift = np.zeros((rows, SUBLANES * win), np.float32)
    for r in range(SUBLANES):
        shift[np.arange(rows), r * win + np.arange(rows) + CONV_PAD - r] = 1.0
    consts = [w['w_out_a'], w['w_out_s'], w['norm_g'], w['w_in'], w['conv_w'], w['conv_b'], w['ln_g'], w['ln_b'],
              w['w_out'], jnp.asarray(shift, BF16)]
    blk = 3 * _nbytes((tm, D_MODEL), F32) + 4 * _nbytes((tm, MLA_WIDTH), BF16) \
        + 2 * _nbytes((CONV_HALO, D_MODEL), F32) + sum(_nbytes(a.shape, a.dtype) for a in consts)
    ext = tm + CONV_PAD + CONV_HALO
    scratch = _nbytes((ext, D_MODEL), F32) + _nbytes((ext, D_MODEL), BF16) + _nbytes((tm + CONV_HALO, D_MODEL), BF16) \
        + 3 * _nbytes((tm, D_MODEL), F32) + _nbytes((tm, D_MODEL), BF16)
    wide = lambda index: pl.BlockSpec((1, tm, D_MODEL), index)
    half = lambda index: pl.BlockSpec((1, tm, MLA_WIDTH), index)
    const3 = lambda i, j: (0, 0, 0)
    return pl.pallas_call(
        functools.partial(_conv_kernel, tm=tm, pipelined=nt > 1),
        grid=(b, nt),
        in_specs=[wide(tokb), wide(nxtb), half(tokb), half(nxtb), half(tokb), half(nxtb),
                  pl.BlockSpec((1, CONV_HALO, D_MODEL), perb)]
                 + [pl.BlockSpec(a.shape, const3 if a.ndim == 3 else const, pipeline_mode=pl.Buffered(1))
                    for a in consts],
        out_specs=[pl.BlockSpec((1, tm, D_MODEL), tokb), pl.BlockSpec((1, CONV_HALO, D_MODEL), perb)],
        out_shape=[jax.ShapeDtypeStruct((b, t, D_MODEL), F32),
                   jax.ShapeDtypeStruct((b, CONV_HALO, D_MODEL), F32)],
        scratch_shapes=[pltpu.VMEM((ext, D_MODEL), F32), pltpu.VMEM((ext, D_MODEL), BF16),
                        pltpu.VMEM((tm + CONV_HALO, D_MODEL), BF16), pltpu.VMEM((tm, D_MODEL), F32),
                        pltpu.VMEM((tm, D_MODEL), BF16), pltpu.VMEM((tm, D_MODEL), F32),
                        pltpu.VMEM((tm, D_MODEL), F32)],
        compiler_params=pltpu.CompilerParams(
            dimension_semantics=("arbitrary", "arbitrary"),
            vmem_limit_bytes=_vmem_limit(blk, scratch, temp_bytes=8 * _nbytes((tm, D_MODEL), F32))),
        name="conv_layer")(x, x, mla, mla, s5m, s5m, past32, *consts)


def _head_groups(nope, rope, third):
    return jnp.concatenate([nope, rope, third], axis=-1).reshape(nope.shape[0], HEAD_PAD)


def _half_swap(a):
    half = ROPE_DIM // 2
    return jnp.concatenate([a[..., half:], a[..., :half]], axis=-1)


def _lane_pad(a, lo, width=LANES):
    pad = [(0, 0)] * (a.ndim - 1) + [(lo, width - lo - a.shape[-1])]
    return jnp.pad(a, pad)


def _prep_ab_weights(norm_g, w_in, g_q_lat, w_uq, g_kv_lat, w_uk, w_uv, g_q_nope, g_q_rope, g_k_nope, g_k_rope,
                     w_out):
    o_kr = Q_LORA + KV_LORA
    kr_cols = w_in[:, o_kr:o_kr + ROPE_DIM]
    kr_group = jnp.concatenate([kr_cols, _half_swap(kr_cols), jnp.zeros((D_MODEL, LANES - 2 * ROPE_DIM), F32)], -1)
    w_in_p = jnp.concatenate([w_in[:, :o_kr], kr_group, w_in[:, o_kr + ROPE_DIM:]], axis=-1).astype(BF16)
    uq = w_uq.reshape(Q_LORA, MLA_HEADS, NOPE_DIM + ROPE_DIM)
    uq_r = uq[..., NOPE_DIM:]
    w_uq_p = _head_groups(uq[..., :NOPE_DIM], uq_r, _half_swap(uq_r)).astype(BF16)
    zeros_kv = jnp.zeros((KV_LORA, MLA_HEADS, LANES - NOPE_DIM), F32)
    w_uk_p = jnp.concatenate([w_uk, zeros_kv], axis=-1).reshape(KV_LORA, HEAD_PAD).astype(BF16)
    w_uv_t = w_uv.reshape(KV_LORA, MLA_WIDTH).T.astype(BF16)
    r = np.arange(2 * LANES)
    same = (r[:, None] // LANES) == (r[None, :] // LANES)
    ri, ci = r[:, None] % LANES, r[None, :] % LANES
    e2 = np.where(same & (ri < NOPE_DIM) & (ci < NOPE_DIM), 1.0 / NOPE_DIM, 0.0) \
        + np.where(same & (ri >= NOPE_DIM) & (ri < NOPE_DIM + ROPE_DIM) & (ci >= NOPE_DIM), 1.0 / ROPE_DIM, 0.0)
    place = np.zeros((ROPE_DIM, LANES), np.float32)
    place[np.arange(ROPE_DIM), NOPE_DIM + np.arange(ROPE_DIM)] = 1.0
    row = lambda a: a.reshape(1, -1)
    return dict(
        norm_g=row(norm_g), w_in=w_in_p, g_q_lat=row(g_q_lat), w_uq=w_uq_p, g_kv=row(g_kv_lat),
        w_uk=w_uk_p, w_uvt=w_uv_t, e2=jnp.asarray(e2, BF16), place=jnp.asarray(place, BF16),
        gq=row(jnp.concatenate([g_q_nope, g_q_rope, jnp.zeros((ROPE_DIM,), F32)])),
        gqs=row(_lane_pad(_half_swap(g_q_rope), NOPE_DIM)),
        gk=row(_lane_pad(g_k_rope, 0)), gks=row(_lane_pad(_half_swap(g_k_rope), 0)),
        gkn=row(_lane_pad(g_k_nope, 0)), gkn2=row(jnp.tile(_lane_pad(g_k_nope, 0), 2)),
        w_out_a=w_out[:MLA_WIDTH].astype(BF16), w_out_s=w_out[MLA_WIDTH:].astype(BF16))


def _rope_tables(pos, attn_block):
    half = ROPE_DIM // 2
    inv = ROPE_BASE ** (-jnp.arange(half, dtype=F32) / half)
    ang = pos.astype(F32)[:, None] * inv[None, :]
    cos = jnp.cos(ang)
    sin = jnp.sin(ang)
    cosf = jnp.concatenate([cos, cos], axis=-1)
    sinf = jnp.concatenate([-sin, sin], axis=-1)
    ones = jnp.ones((pos.shape[0], NOPE_DIM), F32)
    cq = _lane_pad(jnp.concatenate([ones, cosf], axis=-1), 0)
    sq = _lane_pad(sinf, NOPE_DIM)
    n_t = pos.shape[0]
    if attn_block is None:
        qm = km = jnp.zeros((n_t, LANES), F32)
    else:
        n_chunks = attn_block // CHUNK
        assert MASK_LANE0 + n_chunks <= LANES
        own = (jnp.arange(n_t, dtype=jnp.int32) % attn_block) // CHUNK
        c = jnp.arange(n_chunks, dtype=jnp.int32)
        km = _lane_pad((c[None, :] == own[:, None]).astype(F32), MASK_LANE0)
        qm = _lane_pad(jnp.where(c[None, :] <= own[:, None], 0.0, NEG_INF).astype(F32), MASK_LANE0)
    return cq, sq, _lane_pad(cosf, 0), _lane_pad(sinf, 0), qm, km


def _prep_s5_weights(lam_re, lam_im, log_dt, b_re, b_im, c_re, c_im, d_skip, w_glu, b_glu, *, max_tile):
    pos, neg, lam1, bbar = _s5_prep(lam_re, lam_im, log_dt, b_re, b_im, sub=S5_SUB)
    eye = jnp.eye(S5_HALF_GROUPS, dtype=F32)

    def b_block(bb):
        bb = bb.reshape(S5_GROUP, 2, S5_HALF_GROUPS, S5_STATE)
        return jnp.einsum('nhgp,gk->hgnkp', bb, eye).reshape(2, S5_HALF_GROUPS * S5_GROUP, S5_HALF_STATE)

    def c_block(cc):
        cc = cc.reshape(2, S5_HALF_GROUPS, S5_GROUP, S5_STATE)
        return jnp.einsum('hgnp,gk->hkpgn', cc, eye).reshape(2, S5_HALF_STATE, S5_HALF_GROUPS * S5_GROUP)

    n_t = S5_HALF_STATE // LANES
    b_re_blk, b_im_blk = b_block(bbar[0]), b_block(bbar[1])
    bblk = jnp.stack([b_re_blk.reshape(2, -1, n_t, LANES), b_im_blk.reshape(2, -1, n_t, LANES)], axis=3)
    bblk = bblk.reshape(2, S5_HALF_GROUPS * S5_GROUP, 2 * S5_HALF_STATE).astype(BF16)
    c_re_blk, c_im_blk = c_block(c_re), c_block(-c_im)
    cblk = jnp.stack([c_re_blk.reshape(2, n_t, LANES, -1), c_im_blk.reshape(2, n_t, LANES, -1)], axis=2)
    cblk = cblk.reshape(2, 2 * S5_HALF_STATE, S5_HALF_GROUPS * S5_GROUP).astype(BF16)

    def tiled(tab):
        rows = tab.shape[1]
        return jnp.transpose(tab.reshape(2, rows, 2 * n_t, LANES), (1, 2, 0, 3)).reshape(rows, 4 * S5_HALF_STATE)

    r = np.arange(max_tile)
    tri = ((r[:, None] // S5_SUB) == (r[None, :] // S5_SUB)) & (r[None, :] <= r[:, None])
    return dict(bblk=bblk, cblk=cblk, tri=jnp.asarray(tri, BF16), neg=tiled(neg), pos=tiled(pos),
                lam1=tiled(lam1[:, None, :]),
                d=d_skip.reshape(1, -1), w_glu=w_glu.astype(BF16), b_glu=b_glu.reshape(1, -1))


def _pack_state(s_re, s_im):
    b = s_re.shape[0]
    st = jnp.stack([s_re.reshape(b, -1, LANES), s_im.reshape(b, -1, LANES)], axis=2)
    return st.reshape(b, 1, 4 * S5_HALF_STATE)


def _unpack_state(st):
    b = st.shape[0]
    st = st.reshape(b, -1, 2, LANES)
    return (st[:, :, 0].reshape(b, S5_GROUPS, S5_STATE), st[:, :, 1].reshape(b, S5_GROUPS, S5_STATE))


def _ab_branches(x, pos, past, wa, ws, *, tm_in, tm_s5, tq, tk):
    b, t, _ = x.shape
    x2d = x.reshape(b * t, D_MODEL)
    tabs = _rope_tables(pos, tq if past is None else None)
    lat, kr, qp, kp, vt, gm, u, gs = _ab_in(x2d, tabs, wa, seq_len=t, tm=tm_in)
    r3 = lambda a: a.reshape(b, t, a.shape[-1])
    if past is None:
        mla = _attn_prompt(r3(qp), r3(kp), vt, r3(gm), tq=tq)
        s0 = jnp.zeros((b, 1, 4 * S5_HALF_STATE), F32)
    else:
        past_lat, past_kr, past_re, past_im = past
        vt_b = jnp.transpose(vt.reshape(MLA_WIDTH, b, t), (1, 0, 2))
        mla = _attn_sample(r3(qp), past_lat, past_kr, r3(kp), vt_b, r3(gm), wa, tk=tk)
        s0 = _pack_state(past_re, past_im)
    s5, sfin = _s5(r3(u), r3(gs), s0, ws, tm=tm_s5)
    fin_re, fin_im = _unpack_state(sfin)
    return mla, s5, r3(lat), r3(kr), fin_re, fin_im


def _conv_taps(conv_w):
    by_lag = jnp.pad(conv_w[::-1], ((0, CONV_HALO - conv_w.shape[0]), (0, 0))).astype(BF16)
    return jnp.broadcast_to(by_lag[:, None, :], (CONV_HALO, CONV_PAD, conv_w.shape[1]))


def _conv(x, mla, s5m, past, wc, *, tm):
    b = x.shape[0]
    if past is None:
        past32 = jnp.zeros((b, CONV_HALO, D_MODEL), F32)
    else:
        past32 = jnp.pad(past, ((0, 0), (CONV_HALO - (CONV_WIDTH - 1), 0), (0, 0)))
    y, newc = _conv_layer(x, mla, s5m, past32, wc, tm=tm)
    return y, newc[:, CONV_HALO - (CONV_WIDTH - 1):]


def kernel(x_prompt, x_sample, cache_mla_latent, cache_mla_krope, state_s5_re, state_s5_im, state_conv, norm_ab, w_in_ab, g_q_lat, w_uq, g_kv_lat, w_uk, w_uv, g_q_nope, g_q_rope, g_k_nope, g_k_rope, s5_lam_re, s5_lam_im, s5_log_dt, s5_b_re, s5_b_im, s5_c_re, s5_c_im, s5_d, s5_w_glu, s5_b_glu, w_out_ab, norm_c, w_in_c, conv_w, conv_b, ln_g, ln_b, w_out_c):
    t_p = x_prompt.shape[1]
    t_s = x_sample.shape[1]
    past_len = cache_mla_latent.shape[2]
    pos_p = jnp.arange(t_p, dtype=jnp.int32)
    pos_s = past_len + jnp.arange(t_s, dtype=jnp.int32)

    i = 0
    wa = _prep_ab_weights(norm_ab[i], w_in_ab[i], g_q_lat[i], w_uq[i], g_kv_lat[i], w_uk[i], w_uv[i],
                          g_q_nope[i], g_q_rope[i], g_k_nope[i], g_k_rope[i], w_out_ab[i])
    ws = _prep_s5_weights(s5_lam_re[i], s5_lam_im[i], s5_log_dt[i], s5_b_re[i], s5_b_im[i], s5_c_re[i],
                          s5_c_im[i], s5_d[i], s5_w_glu[i], s5_b_glu[i], max_tile=256)
    mla_p, s5_p, lat_p, kr_p, re_p, im_p = _ab_branches(
        x_prompt, pos_p, None, wa, ws, tm_in=512, tm_s5=512, tq=512, tk=512)
    mla_s, s5_s, lat_s, kr_s, re_s, im_s = _ab_branches(
        x_sample, pos_s, (cache_mla_latent[i], cache_mla_krope[i], state_s5_re[i], state_s5_im[i]), wa, ws,
        tm_in=256, tm_s5=t_s, tq=512, tk=1024)

    row = lambda a: a.reshape(1, -1)
    cw = _conv_taps(conv_w[i])
    wc = dict(w_out_a=wa['w_out_a'], w_out_s=wa['w_out_s'],
              norm_g=row(norm_c[i]), w_in=w_in_c[i].astype(BF16), conv_w=cw, conv_b=row(conv_b[i]),
              ln_g=row(ln_g[i]), ln_b=row(ln_b[i]), w_out=w_out_c[i].astype(BF16))
    yp, conv_p = _conv(x_prompt, mla_p, s5_p, None, wc, tm=512)
    ys, conv_s = _conv(x_sample, mla_s, s5_s, state_conv[i], wc, tm=t_s)

    st = lambda a: a[None]
    return (yp, ys, st(lat_p), st(kr_p), st(re_p), st(im_p), st(conv_p),
            st(lat_s), st(kr_s), st(re_s), st(im_s), st(conv_s))
```

```python
import functools
import math

import numpy as np
import jax
import jax.numpy as jnp
from jax import lax
from jax.experimental import pallas as pl
from jax.experimental.pallas import tpu as pltpu

F32 = jnp.float32
BF16 = jnp.bfloat16

D_MODEL = 1024
CHUNK = 64
MLA_HEADS = 8
Q_LORA = 384
KV_LORA = 256
NOPE_DIM = 64
ROPE_DIM = 32
V_DIM = 64
MLA_WIDTH = MLA_HEADS * V_DIM
ROPE_BASE = 10000.0
ATTN_SCALE = (NOPE_DIM + ROPE_DIM) ** -0.5
S5_WIDTH = 512
S5_GROUP = 16
S5_GROUPS = S5_WIDTH // S5_GROUP
S5_STATE = 64
CONV_WIDTH = 31
EPS = 1e-6
NEG_INF = -1e30
LOG2E = math.log2(math.e)
MASK_LANE0 = NOPE_DIM + ROPE_DIM

LANES = 128
SUBLANES = 8
HEAD_PAD = MLA_HEADS * LANES
S5_HALF_GROUPS = S5_GROUPS // 2
S5_HALF_STATE = S5_HALF_GROUPS * S5_STATE
S5_SUB = 32
CONV_HALO = 32
CONV_PAD = 16
V7X_VMEM_BYTES = 64 * 1024 * 1024


def _vmem_limit(block_bytes, scratch_bytes=0, temp_bytes=0):
    est = 2 * block_bytes + scratch_bytes + temp_bytes
    return int(min(max(est, 16 * 1024 * 1024), V7X_VMEM_BYTES - 8 * 1024 * 1024))


def _nbytes(shape, dtype):
    return int(np.prod(shape)) * jnp.dtype(dtype).itemsize


def _rms(x, g):
    return x * lax.rsqrt(jnp.mean(x * x, axis=-1, keepdims=True) + EPS) * g


def _silu(x):
    return x * jax.nn.sigmoid(x)


def _dot(a, b):
    return jnp.dot(a, b, preferred_element_type=F32)


def _dot_nt(a, b):
    return lax.dot_general(a, b, (((1,), (1,)), ((), ())), preferred_element_type=F32)


def _ab_in_kernel(x_ref, ng_ref, win_ref, gql_ref, wuq_ref, gkv_ref, wuk_ref, wuv_ref, e2_ref,
                  cq_ref, sq_ref, ck_ref, sk_ref, qm_ref, km_ref, gq_ref, gqs_ref, gk_ref, gks_ref, gkn_ref,
                  lat_ref, kr_ref, qp_ref, kp_ref, vt_ref, gm_ref, u_ref, gs_ref):
    x = x_ref[...]
    h = _rms(x, ng_ref[...]).astype(BF16)

    def proj(lo, hi):
        return _dot(h, win_ref[:, lo:hi])

    q_lat = proj(0, Q_LORA)
    c_kv = proj(Q_LORA, Q_LORA + KV_LORA)
    krg = proj(640, 768)
    gm_ref[...] = proj(768, 1280)
    u_ref[...] = proj(1280, 1792)
    gs_ref[...] = proj(1792, 2304)

    c_n = _rms(c_kv, gkv_ref[...])
    lat_ref[...] = c_n
    cb = c_n.astype(BF16)

    lane = lax.broadcasted_iota(jnp.int32, (1, LANES), 1)
    ms = jnp.sum(jnp.where(lane < ROPE_DIM, krg * krg, 0.0), axis=-1, keepdims=True) * (1.0 / ROPE_DIM)
    kr = lax.rsqrt(ms + EPS) * (krg * (ck_ref[...] * gk_ref[...])
                                + pltpu.roll(krg, LANES - ROPE_DIM, 1) * (sk_ref[...] * gks_ref[...]))
    kr_ref[...] = kr[:, :ROPE_DIM]
    kr_mask = pltpu.roll(kr, NOPE_DIM, 1) + km_ref[...]

    qn = _rms(q_lat, gql_ref[...]).astype(BF16)
    qa_tab = cq_ref[...] * (gq_ref[...] * (ATTN_SCALE * LOG2E))
    qb_tab = sq_ref[...] * (gqs_ref[...] * (ATTN_SCALE * LOG2E))
    n_pairs = MLA_HEADS // 2
    pair_cols = [slice(2 * LANES * p, 2 * LANES * (p + 1)) for p in range(n_pairs)]
    qas = [_dot(qn, wuq_ref[:, cols]) for cols in pair_cols]
    kas = [_dot(cb, wuk_ref[:, cols]) for cols in pair_cols]
    vt_ref[...] = _dot_nt(wuv_ref[...], cb).astype(BF16)
    q_ms = [_dot((qa * qa).astype(BF16), e2_ref[...]) for qa in qas]
    k_ms = [_dot((ka * ka).astype(BF16), e2_ref[...]) for ka in kas]
    for p in range(n_pairs):
        lo = 2 * LANES * p
        qs = qas[p] * lax.rsqrt(q_ms[p] + EPS)
        ks = kas[p] * lax.rsqrt(k_ms[p] + EPS)
        for j in range(2):
            sl = slice(LANES * j, LANES * (j + 1))
            s = qs[:, sl]
            qp_ref[:, lo + LANES * j:lo + LANES * (j + 1)] = (
                s * qa_tab + pltpu.roll(s, LANES - ROPE_DIM, 1) * qb_tab + qm_ref[...]).astype(BF16)
            kp_ref[:, lo + LANES * j:lo + LANES * (j + 1)] = (
                ks[:, sl] * gkn_ref[...] + kr_mask).astype(BF16)


def _ab_in(x2d, pos_tabs, w, *, seq_len, tm):
    n_tok = x2d.shape[0]
    if tm > seq_len:
        pos_tabs = tuple(jnp.tile(p, (tm // seq_len, 1)) for p in pos_tabs)
    n_pos = max(seq_len // tm, 1)
    tok = lambda i: (i, 0)
    pos = lambda i: (i % n_pos, 0)
    const = lambda i: (0, 0)

    def full(a):
        return pl.BlockSpec(a.shape, const, pipeline_mode=pl.Buffered(1))

    ins = [x2d, w['norm_g'], w['w_in'], w['g_q_lat'], w['w_uq'], w['g_kv'], w['w_uk'], w['w_uvt'], w['e2'],
           *pos_tabs, w['gq'], w['gqs'], w['gk'], w['gks'], w['gkn']]
    n_tabs = len(pos_tabs)
    in_specs = [pl.BlockSpec((tm, D_MODEL), tok)] + [full(a) for a in ins[1:9]] \
        + [pl.BlockSpec((tm, LANES), pos)] * n_tabs + [full(a) for a in ins[9 + n_tabs:]]
    outs = [((n_tok, KV_LORA), F32), ((n_tok, ROPE_DIM), F32), ((n_tok, HEAD_PAD), BF16),
            ((n_tok, HEAD_PAD), BF16), ((MLA_WIDTH, n_tok), BF16), ((n_tok, MLA_WIDTH), F32),
            ((n_tok, S5_WIDTH), F32), ((n_tok, S5_WIDTH), F32)]
    vt_index = 4
    out_specs = [pl.BlockSpec((MLA_WIDTH, tm), lambda i: (0, i)) if n == vt_index
                 else pl.BlockSpec((tm, s[1]), tok) for n, (s, _) in enumerate(outs)]
    out_shape = [jax.ShapeDtypeStruct(s, d) for s, d in outs]
    blk = sum(_nbytes(a.shape, a.dtype) for a in ins[1:9]) + _nbytes((tm, D_MODEL), F32) \
        + n_tabs * _nbytes((tm, LANES), F32) + sum(_nbytes(s, d) for s, d in outs) * tm // n_tok
    return pl.pallas_call(
        _ab_in_kernel, grid=(n_tok // tm,), in_specs=in_specs, out_specs=out_specs, out_shape=out_shape,
        compiler_params=pltpu.CompilerParams(
            dimension_semantics=("arbitrary",),
            vmem_limit_bytes=_vmem_limit(blk, temp_bytes=8 * _nbytes((tm, D_MODEL), F32))),
        name="ab_in")(*ins)


def _ones_row(n, rows=V_DIM):
    return (lax.broadcasted_iota(jnp.int32, (rows, n), 0) == 0).astype(BF16)


VAUG_ROWS = 2 * V_DIM


def _attn_prompt_kernel(qi_ref, kj_ref, q_ref, k_ref, vt_ref, g_ref, o_ref, vaug_sc, s_sc, acc_sc, *, tq,
                        seq_len):
    n_steps = qi_ref.shape[0]
    ones_row = _ones_row(seq_len, VAUG_ROWS - V_DIM)
    for hh in range(2):
        vaug_sc[hh, 0:V_DIM, :] = vt_ref[hh * V_DIM:(hh + 1) * V_DIM, :]
        vaug_sc[hh, V_DIM:VAUG_ROWS, :] = ones_row
    lane = lax.broadcasted_iota(jnp.int32, (1, LANES), 1)
    mask_lane_off = jnp.logical_or(lane < MASK_LANE0, lane >= MASK_LANE0 + tq // CHUNK)

    def produce(n, slot):
        qi = qi_ref[n]
        kj = kj_ref[n]
        r0 = pl.multiple_of(qi * tq, tq)
        k0 = pl.multiple_of(kj * tq, tq)
        q_keep = jnp.logical_or(mask_lane_off, kj == qi)
        bms = []
        for hh in range(2):
            hs = slice(hh * LANES, (hh + 1) * LANES)
            q = q_ref[0, pl.ds(r0, tq), hs]
            q = jnp.where(q_keep, q, jnp.zeros_like(q))
            s = _dot_nt(k_ref[0, pl.ds(k0, tq), hs], q)
            s_sc[slot, hh] = s
            bms.append(jnp.max(s, axis=0, keepdims=True))
        return tuple(bms)

    def consume(n, slot, bms, state):
        qi = qi_ref[n]
        kj = kj_ref[n]
        k0 = pl.multiple_of(kj * tq, tq)
        out = []
        for hh in range(2):
            m, acc = state[hh]
            m = jnp.where(kj == 0, -jnp.inf, m)
            m_new = jnp.maximum(m, bms[hh])
            p = jnp.exp2(s_sc[slot, hh] - m_new)
            alpha = jnp.exp2(m - m_new)
            acc = alpha * acc + _dot(vaug_sc[hh, :, pl.ds(k0, tq)], p.astype(BF16))
            acc_sc[qi, hh] = acc
            out.append((m_new, acc))
        return tuple(out)

    unroll = 12

    def multi_step(t, carry):
        bms, state = carry
        for u in range(unroll):
            n = unroll * t + u
            next_bms = produce(n + 1, (u + 1) % 2)
            state = consume(n, u % 2, bms, state)
            bms = next_bms
        return bms, state

    state = tuple((jnp.full((1, tq), -jnp.inf, F32), jnp.zeros((VAUG_ROWS, tq), F32)) for _ in range(2))
    n_loops = (n_steps - 1) // unroll
    bms, state = lax.fori_loop(0, n_loops, multi_step, (produce(0, 0), state))
    for n in range(unroll * n_loops, n_steps):
        next_bms = produce(n + 1, (n + 1) % 2) if n + 1 < n_steps else None
        state = consume(n, n % 2, bms, state)
        bms = next_bms

    def finalize(qi, carry):
        r0 = pl.multiple_of(qi * tq, tq)
        vals = []
        for hh in range(2):
            acc = acc_sc[qi, hh]
            vals.append(acc[0:V_DIM, :] * (1.0 / acc[V_DIM:V_DIM + 1, :]))
        o = jnp.concatenate(vals, axis=0).T
        o_ref[0, pl.ds(r0, tq), :] = (o * _silu(g_ref[0, pl.ds(r0, tq), :])).astype(BF16)
        return carry

    lax.fori_loop(0, seq_len // tq, finalize, 0)


def _attn_prompt(qp, kp, vt, gate, *, tq):
    b, t, _ = qp.shape
    nq = t // tq
    steps = [(qi, kj) for qi in range(nq) for kj in range(qi + 1)]
    qi_tab = jnp.asarray([s[0] for s in steps], jnp.int32)
    kj_tab = jnp.asarray([s[1] for s in steps], jnp.int32)
    pair = lambda i, j, qt, kt: (i, 0, j)
    blk = 2 * _nbytes((t, 2 * LANES), BF16) + _nbytes((LANES, t), BF16) + _nbytes((t, LANES), F32) \
        + _nbytes((t, LANES), BF16)
    scratch = _nbytes((2, VAUG_ROWS, t), BF16) + _nbytes((2, 2, tq, tq), F32) + _nbytes((nq, 2, VAUG_ROWS, tq), F32)
    return pl.pallas_call(
        functools.partial(_attn_prompt_kernel, tq=tq, seq_len=t),
        grid_spec=pltpu.PrefetchScalarGridSpec(
            num_scalar_prefetch=2,
            grid=(b, MLA_HEADS // 2),
            in_specs=[pl.BlockSpec((1, t, 2 * LANES), pair), pl.BlockSpec((1, t, 2 * LANES), pair),
                      pl.BlockSpec((LANES, t), lambda i, j, qt, kt: (j, i)), pl.BlockSpec((1, t, LANES), pair)],
            out_specs=pl.BlockSpec((1, t, LANES), pair),
            scratch_shapes=[pltpu.VMEM((2, VAUG_ROWS, t), BF16), pltpu.VMEM((2, 2, tq, tq), F32),
                            pltpu.VMEM((nq, 2, VAUG_ROWS, tq), F32)]),
        out_shape=jax.ShapeDtypeStruct((b, t, MLA_WIDTH), BF16),
        compiler_params=pltpu.CompilerParams(
            dimension_semantics=("arbitrary", "arbitrary"),
            vmem_limit_bytes=_vmem_limit(blk, scratch, temp_bytes=12 * _nbytes((tq, tq), F32))),
        name="attn_prompt")(qi_tab, kj_tab, qp, kp, vt, gate)


def _attn_sample_kernel(q_ref, lat_ref, kr_ref, kpn_ref, vtn_ref, g_ref, wuk_ref, wuvt_ref, e2_ref, gkn2_ref,
                        place_ref, o_ref, qbd_sc, m_sc, acc_sc, *, n_past_blocks):
    j = pl.program_id(1)
    n_pairs = MLA_HEADS // 2
    t = q_ref.shape[1]
    lane = lax.broadcasted_iota(jnp.int32, (1, LANES), 1)

    @pl.when(j == 0)
    def _():
        m_sc[...] = jnp.full(m_sc.shape, -jnp.inf, F32)
        acc_sc[...] = jnp.zeros(acc_sc.shape, F32)
        q = q_ref[0].astype(F32)
        q_t = jnp.concatenate([q, jnp.zeros((LANES - t, HEAD_PAD), F32)], axis=0).T
        for p in range(n_pairs):
            top = q_t[2 * LANES * p:2 * LANES * p + LANES, :]
            bot = pltpu.roll(q_t[2 * LANES * p + LANES:2 * LANES * (p + 1), :], t, 1)
            qbd_sc[p] = jnp.concatenate([top, bot], axis=0).astype(BF16)

    def update_all(kp_pairs, vt_pairs):
        n = kp_pairs[0].shape[0]
        ones_row = _ones_row(n)
        ss = [_dot(kp_pairs[p], qbd_sc[p]) for p in range(n_pairs)]
        for p in range(n_pairs):
            m_old = m_sc[p]
            m_new = jnp.maximum(m_old, jnp.max(ss[p], axis=0, keepdims=True))
            pr = jnp.exp2(ss[p] - m_new)
            alpha = jnp.exp2(m_old - m_new)
            vt_pair = vt_pairs[p]
            vaug = jnp.concatenate([vt_pair[0:V_DIM, :], ones_row, vt_pair[V_DIM:2 * V_DIM, :], ones_row], axis=0)
            acc_sc[p] = alpha * acc_sc[p] + _dot(vaug, pr.astype(BF16))
            m_sc[p] = m_new

    @pl.when(j < n_past_blocks)
    def _():
        latb = lat_ref[0].astype(BF16)
        kr128 = _dot(kr_ref[0].astype(BF16), place_ref[...])
        kr256 = jnp.concatenate([kr128, kr128], axis=1)
        vt_all = _dot_nt(wuvt_ref[...], latb).astype(BF16)
        kas = [_dot(latb, wuk_ref[:, 2 * LANES * p:2 * LANES * (p + 1)]) for p in range(n_pairs)]
        mss = [_dot((ka * ka).astype(BF16), e2_ref[...]) for ka in kas]
        kps = [(ka * lax.rsqrt(ms + EPS) * gkn2_ref[...] + kr256).astype(BF16) for ka, ms in zip(kas, mss)]
        update_all(kps, [vt_all[LANES * p:LANES * (p + 1), :] for p in range(n_pairs)])

    @pl.when(j == n_past_blocks)
    def _():
        update_all([kpn_ref[0, :, 2 * LANES * p:2 * LANES * (p + 1)] for p in range(n_pairs)],
                   [vtn_ref[0, LANES * p:LANES * (p + 1), :] for p in range(n_pairs)])
        for p in range(n_pairs):
            acc_t = acc_sc[p].T
            a0 = acc_t[0:t, 0:LANES]
            a1 = acc_t[t:2 * t, LANES:2 * LANES]
            o0 = a0 * (1.0 / a0[:, V_DIM:V_DIM + 1])
            o1 = a1 * (1.0 / a1[:, V_DIM:V_DIM + 1])
            o = jnp.where(lane < V_DIM, o0, pltpu.roll(o1, V_DIM, 1))
            g = g_ref[0, :, p * LANES:(p + 1) * LANES]
            o_ref[0, :, p * LANES:(p + 1) * LANES] = (o * _silu(g)).astype(BF16)


def _attn_sample(qp, past_lat, past_kr, kp_new, vt_new, gate, w, *, tk):
    b, t, _ = qp.shape
    assert 2 * t == LANES, "two heads' queries share one 128-lane group"
    n_past_blocks = past_lat.shape[1] // tk
    n_pairs = MLA_HEADS // 2
    cur = lambda i, j: (i, 0, 0)
    past = lambda i, j: (i, jnp.minimum(j, n_past_blocks - 1), 0)
    const = lambda i, j: (0, 0)
    consts = [w['w_uk'], w['w_uvt'], w['e2'], w['gkn2'], w['place']]
    blk = _nbytes((tk, KV_LORA), F32) + _nbytes((tk, LANES), F32) + 2 * _nbytes((t, HEAD_PAD), BF16) \
        + _nbytes((MLA_WIDTH, LANES), BF16) + 2 * _nbytes((t, MLA_WIDTH), F32) \
        + sum(_nbytes(a.shape, a.dtype) for a in consts)
    scratch = _nbytes((n_pairs, 2 * LANES, LANES), BF16) + _nbytes((n_pairs, SUBLANES, LANES), F32) \
        + _nbytes((n_pairs, 2 * LANES, LANES), F32)
    return pl.pallas_call(
        functools.partial(_attn_sample_kernel, n_past_blocks=n_past_blocks),
        grid=(b, n_past_blocks + 1),
        in_specs=[pl.BlockSpec((1, t, HEAD_PAD), cur),
                  pl.BlockSpec((1, tk, KV_LORA), past),
                  pl.BlockSpec((1, tk, ROPE_DIM), past),
                  pl.BlockSpec((1, t, HEAD_PAD), cur),
                  pl.BlockSpec((1, MLA_WIDTH, t), cur),
                  pl.BlockSpec((1, t, MLA_WIDTH), cur)] + [pl.BlockSpec(a.shape, const) for a in consts],
        out_specs=pl.BlockSpec((1, t, MLA_WIDTH), cur),
        out_shape=jax.ShapeDtypeStruct((b, t, MLA_WIDTH), BF16),
        scratch_shapes=[pltpu.VMEM((n_pairs, 2 * LANES, LANES), BF16),
                        pltpu.VMEM((n_pairs, 1, LANES), F32),
                        pltpu.VMEM((n_pairs, 2 * LANES, LANES), F32)],
        compiler_params=pltpu.CompilerParams(
            dimension_semantics=("arbitrary", "arbitrary"),
            vmem_limit_bytes=_vmem_limit(blk, scratch, temp_bytes=12 * _nbytes((tk, 2 * LANES), F32))),
        name="attn_sample")(qp, past_lat, past_kr, kp_new, vt_new, gate, *consts)


def _s5_prep_kernel(lr_ref, li_ref, ldt_ref, br_ref, bi_ref, pos_ref, neg_ref, lam1_ref, bbar_ref, *, sub):
    lr = lr_ref[...]
    li = li_ref[...]
    dt = jnp.exp(ldt_ref[...])
    ar = lr * dt
    ai = li * dt
    k = lax.broadcasted_iota(jnp.int32, (sub, 1), 0).astype(F32)
    mag_p = jnp.exp(k * ar)
    mag_n = jnp.exp(-k * ar)
    ang = k * ai
    c = jnp.cos(ang)
    s = jnp.sin(ang)
    pos_ref[0] = mag_p * c
    pos_ref[1] = mag_p * s
    neg_ref[0] = mag_n * c
    neg_ref[1] = -mag_n * s
    mag1 = jnp.exp(ar)
    l1r = mag1 * jnp.cos(ai)
    l1i = mag1 * jnp.sin(ai)
    lam1_ref[0:1, :] = l1r
    lam1_ref[1:2, :] = l1i
    inv = 1.0 / (lr * lr + li * li)
    nr = l1r - 1.0
    cr = (nr * lr + l1i * li) * inv
    ci = (l1i * lr - nr * li) * inv
    br = br_ref[...]
    bi = bi_ref[...]
    bbar_ref[0] = cr * br - ci * bi
    bbar_ref[1] = cr * bi + ci * br


def _s5_prep(lam_re, lam_im, log_dt, b_re, b_im, *, sub):
    gp = S5_GROUPS * S5_STATE
    lr = lam_re.reshape(1, gp)
    li = lam_im.reshape(1, gp)
    ldt = jnp.broadcast_to(log_dt[:, None], (S5_GROUPS, S5_STATE)).reshape(1, gp)
    br = jnp.transpose(b_re, (2, 0, 1)).reshape(S5_GROUP, gp)
    bi = jnp.transpose(b_im, (2, 0, 1)).reshape(S5_GROUP, gp)
    return pl.pallas_call(
        functools.partial(_s5_prep_kernel, sub=sub),
        out_shape=[jax.ShapeDtypeStruct((2, sub, gp), F32), jax.ShapeDtypeStruct((2, sub, gp), F32),
                   jax.ShapeDtypeStruct((2, gp), F32), jax.ShapeDtypeStruct((2, S5_GROUP, gp), F32)],
        name="s5_prep")(lr, li, ldt, br, bi)


def _s5_kernel(u_ref, gs_ref, s0_ref, bblk_ref, cblk_ref, tri_ref, neg_ref, pos_ref, lam1_ref, d_ref,
               wglu_ref, bglu_ref, o_ref, sfin_ref, carry_sc, *, tm, sub):
    ti = pl.program_id(1)
    tile = 2 * LANES
    tiles_per_half = 2 * S5_HALF_STATE // tile

    @pl.when(ti == 0)
    def _():
        carry_sc[...] = s0_ref[0]

    u = u_ref[0]
    ub = u.astype(BF16)
    n_tiles = 2 * tiles_per_half

    def lanes_of(t):
        lo = t * tile
        return slice(lo, lo + LANES), slice(lo + LANES, lo + tile)

    def stage_a(t):
        hf, q = divmod(t, tiles_per_half)
        re, im = lanes_of(t)
        bu = _dot(ub[:, hf * 256:(hf + 1) * 256], bblk_ref[hf, :, q * tile:(q + 1) * tile])
        nre, nim = neg_ref[:, re], neg_ref[:, im]
        xs = []
        for c in range(tm // sub):
            rs = slice(c * sub, (c + 1) * sub)
            bre, bim = bu[rs, :LANES], bu[rs, LANES:]
            xs.append(jnp.concatenate([nre * bre - nim * bim, nre * bim + nim * bre], axis=-1).astype(BF16))
        return jnp.concatenate(xs, axis=0)

    def stage_b(t, x):
        re, im = lanes_of(t)
        tb = tri_ref.shape[0]
        cs = jnp.concatenate([_dot(tri_ref[...], x[i * tb:(i + 1) * tb, :]) for i in range(tm // tb)], axis=0)
        cre, cim = carry_sc[:, re], carry_sc[:, im]
        l1r, l1i = lam1_ref[:, re], lam1_ref[:, im]
        pre, pim = pos_ref[:, re], pos_ref[:, im]
        ss = []
        for c in range(tm // sub):
            rs = slice(c * sub, (c + 1) * sub)
            tre = cs[rs, :LANES] + (l1r * cre - l1i * cim)
            tim = cs[rs, LANES:] + (l1r * cim + l1i * cre)
            sre = pre * tre - pim * tim
            sim = pre * tim + pim * tre
            ss.append(jnp.concatenate([sre, sim], axis=-1).astype(BF16))
            cre = sre[sub - 1:sub, :]
            cim = sim[sub - 1:sub, :]
        carry_sc[:, re] = cre
        carry_sc[:, im] = cim
        return jnp.concatenate(ss, axis=0)

    def stage_c(t, s):
        hf, q = divmod(t, tiles_per_half)
        return _dot(s, cblk_ref[hf, q * tile:(q + 1) * tile, :])

    xs_, ss_ = {}, {}
    accs = [None, None]
    skew = 1
    for step in range(n_tiles + 2 * skew):
        if step < n_tiles:
            xs_[step] = stage_a(step)
        if skew <= step < n_tiles + skew:
            ss_[step - skew] = stage_b(step - skew, xs_.pop(step - skew))
        if step >= 2 * skew:
            t = step - 2 * skew
            part = stage_c(t, ss_.pop(t))
            hf = t // tiles_per_half
            accs[hf] = part if accs[hf] is None else accs[hf] + part
    y = jnp.concatenate(accs, axis=-1) + d_ref[...] * u
    z = jax.nn.gelu(y)
    z = z * jax.nn.sigmoid(_dot(z.astype(BF16), wglu_ref[...]) + bglu_ref[...])
    o_ref[0] = (z * _silu(gs_ref[0])).astype(BF16)

    @pl.when(ti == pl.num_programs(1) - 1)
    def _():
        sfin_ref[0] = carry_sc[...]


def _s5(u, gate, s0, w, *, tm):
    b, t, _ = u.shape
    sub = S5_SUB
    tokb = lambda i, j: (i, j, 0)
    perb = lambda i, j: (i, 0, 0)
    c2 = lambda i, j: (0, 0)
    c3 = lambda i, j: (0, 0, 0)
    tb = min(tm, w['tri'].shape[0])
    tri = w['tri'][:tb, :tb]
    consts = [w['bblk'], w['cblk'], tri, w['neg'], w['pos'], w['lam1'], w['d'], w['w_glu'], w['b_glu']]
    blk = 2 * _nbytes((tm, S5_WIDTH), F32) + _nbytes((tm, S5_WIDTH), BF16) + 2 * _nbytes((1, 4 * S5_HALF_STATE), F32) \
        + sum(_nbytes(a.shape, a.dtype) for a in consts)
    scratch = _nbytes((SUBLANES, 4 * S5_HALF_STATE), F32)
    return pl.pallas_call(
        functools.partial(_s5_kernel, tm=tm, sub=sub),
        grid=(b, t // tm),
        in_specs=[pl.BlockSpec((1, tm, S5_WIDTH), tokb), pl.BlockSpec((1, tm, S5_WIDTH), tokb),
                  pl.BlockSpec((1, 1, 4 * S5_HALF_STATE), perb)]
                 + [pl.BlockSpec(a.shape, c3 if a.ndim == 3 else c2) for a in consts],
        out_specs=[pl.BlockSpec((1, tm, S5_WIDTH), tokb), pl.BlockSpec((1, 1, 4 * S5_HALF_STATE), perb)],
        out_shape=[jax.ShapeDtypeStruct((b, t, S5_WIDTH), BF16),
                   jax.ShapeDtypeStruct((b, 1, 4 * S5_HALF_STATE), F32)],
        scratch_shapes=[pltpu.VMEM((1, 4 * S5_HALF_STATE), F32)],
        compiler_params=pltpu.CompilerParams(
            dimension_semantics=("arbitrary", "arbitrary"),
            vmem_limit_bytes=_vmem_limit(blk, scratch, temp_bytes=16 * _nbytes((tm, 2 * LANES), F32))),
        name="s5_scan")(u, gate, s0, *consts)


def _conv_kernel(x_ref, xn_ref, ma_ref, man_ref, ms_ref, msn_ref, past_ref, woa_ref, wos_ref, ng_ref, win_ref,
                 cw_ref, cb_ref, lng_ref, lnb_ref, wout_ref,
                 shift_ref, y_ref, newc_ref, vext_sc, vbe_sc, vbo_sc, stage_sc, hstage_sc, xstage_sc, yc_sc, *,
                 tm, pipelined):
    ti = pl.program_id(1)
    base = CONV_PAD + CONV_HALO

    def stage_tile(x0, mla, s5m):
        x1 = _dot(mla, woa_ref[...]) + _dot(s5m, wos_ref[...]) + x0
        xstage_sc[...] = x1
        h = _rms(x1, ng_ref[...]).astype(BF16)
        hstage_sc[...] = h
        stage_sc[...] = _dot(h, win_ref[:, 0:D_MODEL]) * jax.nn.sigmoid(_dot(h, win_ref[:, D_MODEL:2 * D_MODEL]))

    @pl.when(ti == 0)
    def _():
        vext_sc[0:CONV_PAD, :] = jnp.zeros((CONV_PAD, D_MODEL), F32)
        vext_sc[CONV_PAD:base, :] = past_ref[0]
        if pipelined:
            stage_tile(x_ref[0], ma_ref[0], ms_ref[0])

    if not pipelined:
        stage_tile(x_ref[0], ma_ref[0], ms_ref[0])
    x = xstage_sc[...]
    h = hstage_sc[...]
    vext_sc[base:base + tm, :] = stage_sc[...]
    if pipelined:
        stage_tile(xn_ref[0], man_ref[0], msn_ref[0])

    vbe_sc[...] = vext_sc[...].astype(BF16)
    vbo_sc[...] = vext_sc[SUBLANES:SUBLANES + tm + CONV_HALO, :].astype(BF16)
    rows = min(tm, 128)
    width = 2 * LANES
    win = rows + CONV_PAD
    for rb in range(tm // rows):
        r0 = rb * rows
        for lb in range(D_MODEL // width):
            ls = slice(lb * width, (lb + 1) * width)
            zs = []
            for r in range(SUBLANES):
                z = None
                for a in range(CONV_HALO // SUBLANES):
                    start = r0 + base - CONV_PAD - SUBLANES * a
                    if a % 2 == 0:
                        src = vbe_sc[start:start + win, ls]
                    else:
                        src = vbo_sc[start - SUBLANES:start - SUBLANES + win, ls]
                    term = src.reshape(win // CONV_PAD, CONV_PAD, width) * cw_ref[SUBLANES * a + r, :, ls][None]
                    z = term if z is None else z + term
                zs.append(z)
            zcat = jnp.concatenate(zs, axis=0).reshape(SUBLANES * win, width)
            yc_sc[r0:r0 + rows, ls] = _dot(shift_ref[...], zcat)

    yc = yc_sc[...] + cb_ref[...]
    mu = jnp.mean(yc, axis=-1, keepdims=True)
    xc = yc - mu
    var = jnp.mean(xc * xc, axis=-1, keepdims=True)
    yn = xc * lax.rsqrt(var + EPS) * lng_ref[...] + lnb_ref[...]
    gate = _dot(h, win_ref[:, 2 * D_MODEL:3 * D_MODEL])
    mixed = (_silu(yn) * _silu(gate)).astype(BF16)
    y_ref[0] = _dot(mixed, wout_ref[...]) + x

    tail = vext_sc[CONV_PAD + tm:base + tm, :]
    vext_sc[CONV_PAD:base, :] = tail
    newc_ref[0] = tail


def _conv_layer(x, mla, s5m, past32, w, *, tm):
    b, t, _ = x.shape
    nt = t // tm
    tokb = lambda i, j: (i, j, 0)
    nxtb = lambda i, j: (i, jnp.minimum(j + 1, nt - 1), 0)
    perb = lambda i, j: (i, 0, 0)
    const = lambda i, j: (0, 0)
    rows = min(tm, 128)
    win = rows + CONV_PAD
    shift = np.zeros((rows, SUBLANES * win), np.float32)
    for r in range(SUBLANES):
        shift[np.arange(rows), r * win + np.arange(rows) + CONV_PAD - r] = 1.0
    consts = [w['w_out_a'], w['w_out_s'], w['norm_g'], w['w_in'], w['conv_w'], w['conv_b'], w['ln_g'], w['ln_b'],
              w['w_out'], jnp.asarray(shift, BF16)]
    blk = 3 * _nbytes((tm, D_MODEL), F32) + 4 * _nbytes((tm, MLA_WIDTH), BF16) \
        + 2 * _nbytes((CONV_HALO, D_MODEL), F32) + sum(_nbytes(a.shape, a.dtype) for a in consts)
    ext = tm + CONV_PAD + CONV_HALO
    scratch = _nbytes((ext, D_MODEL), F32) + _nbytes((ext, D_MODEL), BF16) + _nbytes((tm + CONV_HALO, D_MODEL), BF16) \
        + 3 * _nbytes((tm, D_MODEL), F32) + _nbytes((tm, D_MODEL), BF16)
    wide = lambda index: pl.BlockSpec((1, tm, D_MODEL), index)
    half = lambda index: pl.BlockSpec((1, tm, MLA_WIDTH), index)
    const3 = lambda i, j: (0, 0, 0)
    return pl.pallas_call(
        functools.partial(_conv_kernel, tm=tm, pipelined=nt > 1),
        grid=(b, nt),
        in_specs=[wide(tokb), wide(nxtb), half(tokb), half(nxtb), half(tokb), half(nxtb),
                  pl.BlockSpec((1, CONV_HALO, D_MODEL), perb)]
                 + [pl.BlockSpec(a.shape, const3 if a.ndim == 3 else const, pipeline_mode=pl.Buffered(1))
                    for a in consts],
        out_specs=[pl.BlockSpec((1, tm, D_MODEL), tokb), pl.BlockSpec((1, CONV_HALO, D_MODEL), perb)],
        out_shape=[jax.ShapeDtypeStruct((b, t, D_MODEL), F32),
                   jax.ShapeDtypeStruct((b, CONV_HALO, D_MODEL), F32)],
        scratch_shapes=[pltpu.VMEM((ext, D_MODEL), F32), pltpu.VMEM((ext, D_MODEL), BF16),
                        pltpu.VMEM((tm + CONV_HALO, D_MODEL), BF16), pltpu.VMEM((tm, D_MODEL), F32),
                        pltpu.VMEM((tm, D_MODEL), BF16), pltpu.VMEM((tm, D_MODEL), F32),
                        pltpu.VMEM((tm, D_MODEL), F32)],
        compiler_params=pltpu.CompilerParams(
            dimension_semantics=("arbitrary", "arbitrary"),
            vmem_limit_bytes=_vmem_limit(blk, scratch, temp_bytes=8 * _nbytes((tm, D_MODEL), F32))),
        name="conv_layer")(x, x, mla, mla, s5m, s5m, past32, *consts)


def _head_groups(nope, rope, third):
    return jnp.concatenate([nope, rope, third], axis=-1).reshape(nope.shape[0], HEAD_PAD)


def _half_swap(a):
    half = ROPE_DIM // 2
    return jnp.concatenate([a[..., half:], a[..., :half]], axis=-1)


def _lane_pad(a, lo, width=LANES):
    pad = [(0, 0)] * (a.ndim - 1) + [(lo, width - lo - a.shape[-1])]
    return jnp.pad(a, pad)


def _prep_ab_weights(norm_g, w_in, g_q_lat, w_uq, g_kv_lat, w_uk, w_uv, g_q_nope, g_q_rope, g_k_nope, g_k_rope,
                     w_out):
    o_kr = Q_LORA + KV_LORA
    kr_cols = w_in[:, o_kr:o_kr + ROPE_DIM]
    kr_group = jnp.concatenate([kr_cols, _half_swap(kr_cols), jnp.zeros((D_MODEL, LANES - 2 * ROPE_DIM), F32)], -1)
    w_in_p = jnp.concatenate([w_in[:, :o_kr], kr_group, w_in[:, o_kr + ROPE_DIM:]], axis=-1).astype(BF16)
    uq = w_uq.reshape(Q_LORA, MLA_HEADS, NOPE_DIM + ROPE_DIM)
    uq_r = uq[..., NOPE_DIM:]
    w_uq_p = _head_groups(uq[..., :NOPE_DIM], uq_r, _half_swap(uq_r)).astype(BF16)
    zeros_kv = jnp.zeros((KV_LORA, MLA_HEADS, LANES - NOPE_DIM), F32)
    w_uk_p = jnp.concatenate([w_uk, zeros_kv], axis=-1).reshape(KV_LORA, HEAD_PAD).astype(BF16)
    w_uv_t = w_uv.reshape(KV_LORA, MLA_WIDTH).T.astype(BF16)
    r = np.arange(2 * LANES)
    same = (r[:, None] // LANES) == (r[None, :] // LANES)
    ri, ci = r[:, None] % LANES, r[None, :] % LANES
    e2 = np.where(same & (ri < NOPE_DIM) & (ci < NOPE_DIM), 1.0 / NOPE_DIM, 0.0) \
        + np.where(same & (ri >= NOPE_DIM) & (ri < NOPE_DIM + ROPE_DIM) & (ci >= NOPE_DIM), 1.0 / ROPE_DIM, 0.0)
    place = np.zeros((ROPE_DIM, LANES), np.float32)
    place[np.arange(ROPE_DIM), NOPE_DIM + np.arange(ROPE_DIM)] = 1.0
    row = lambda a: a.reshape(1, -1)
    return dict(
        norm_g=row(norm_g), w_in=w_in_p, g_q_lat=row(g_q_lat), w_uq=w_uq_p, g_kv=row(g_kv_lat),
        w_uk=w_uk_p, w_uvt=w_uv_t, e2=jnp.asarray(e2, BF16), place=jnp.asarray(place, BF16),
        gq=row(jnp.concatenate([g_q_nope, g_q_rope, jnp.zeros((ROPE_DIM,), F32)])),
        gqs=row(_lane_pad(_half_swap(g_q_rope), NOPE_DIM)),
        gk=row(_lane_pad(g_k_rope, 0)), gks=row(_lane_pad(_half_swap(g_k_rope), 0)),
        gkn=row(_lane_pad(g_k_nope, 0)), gkn2=row(jnp.tile(_lane_pad(g_k_nope, 0), 2)),
        w_out_a=w_out[:MLA_WIDTH].astype(BF16), w_out_s=w_out[MLA_WIDTH:].astype(BF16))


def _rope_tables(pos, attn_block):
    half = ROPE_DIM // 2
    inv = ROPE_BASE ** (-jnp.arange(half, dtype=F32) / half)
    ang = pos.astype(F32)[:, None] * inv[None, :]
    cos = jnp.cos(ang)
    sin = jnp.sin(ang)
    cosf = jnp.concatenate([cos, cos], axis=-1)
    sinf = jnp.concatenate([-sin, sin], axis=-1)
    ones = jnp.ones((pos.shape[0], NOPE_DIM), F32)
    cq = _lane_pad(jnp.concatenate([ones, cosf], axis=-1), 0)
    sq = _lane_pad(sinf, NOPE_DIM)
    n_t = pos.shape[0]
    if attn_block is None:
        qm = km = jnp.zeros((n_t, LANES), F32)
    else:
        n_chunks = attn_block // CHUNK
        assert MASK_LANE0 + n_chunks <= LANES
        own = (jnp.arange(n_t, dtype=jnp.int32) % attn_block) // CHUNK
        c = jnp.arange(n_chunks, dtype=jnp.int32)
        km = _lane_pad((c[None, :] == own[:, None]).astype(F32), MASK_LANE0)
        qm = _lane_pad(jnp.where(c[None, :] <= own[:, None], 0.0, NEG_INF).astype(F32), MASK_LANE0)
    return cq, sq, _lane_pad(cosf, 0), _lane_pad(sinf, 0), qm, km


def _prep_s5_weights(lam_re, lam_im, log_dt, b_re, b_im, c_re, c_im, d_skip, w_glu, b_glu, *, max_tile):
    pos, neg, lam1, bbar = _s5_prep(lam_re, lam_im, log_dt, b_re, b_im, sub=S5_SUB)
    eye = jnp.eye(S5_HALF_GROUPS, dtype=F32)

    def b_block(bb):
        bb = bb.reshape(S5_GROUP, 2, S5_HALF_GROUPS, S5_STATE)
        return jnp.einsum('nhgp,gk->hgnkp', bb, eye).reshape(2, S5_HALF_GROUPS * S5_GROUP, S5_HALF_STATE)

    def c_block(cc):
        cc = cc.reshape(2, S5_HALF_GROUPS, S5_GROUP, S5_STATE)
        return jnp.einsum('hgnp,gk->hkpgn', cc, eye).reshape(2, S5_HALF_STATE, S5_HALF_GROUPS * S5_GROUP)

    n_t = S5_HALF_STATE // LANES
    b_re_blk, b_im_blk = b_block(bbar[0]), b_block(bbar[1])
    bblk = jnp.stack([b_re_blk.reshape(2, -1, n_t, LANES), b_im_blk.reshape(2, -1, n_t, LANES)], axis=3)
    bblk = bblk.reshape(2, S5_HALF_GROUPS * S5_GROUP, 2 * S5_HALF_STATE).astype(BF16)
    c_re_blk, c_im_blk = c_block(c_re), c_block(-c_im)
    cblk = jnp.stack([c_re_blk.reshape(2, n_t, LANES, -1), c_im_blk.reshape(2, n_t, LANES, -1)], axis=2)
    cblk = cblk.reshape(2, 2 * S5_HALF_STATE, S5_HALF_GROUPS * S5_GROUP).astype(BF16)

    def tiled(tab):
        rows = tab.shape[1]
        return jnp.transpose(tab.reshape(2, rows, 2 * n_t, LANES), (1, 2, 0, 3)).reshape(rows, 4 * S5_HALF_STATE)

    r = np.arange(max_tile)
    tri = ((r[:, None] // S5_SUB) == (r[None, :] // S5_SUB)) & (r[None, :] <= r[:, None])
    return dict(bblk=bblk, cblk=cblk, tri=jnp.asarray(tri, BF16), neg=tiled(neg), pos=tiled(pos),
                lam1=tiled(lam1[:, None, :]),
                d=d_skip.reshape(1, -1), w_glu=w_glu.astype(BF16), b_glu=b_glu.reshape(1, -1))


def _pack_state(s_re, s_im):
    b = s_re.shape[0]
    st = jnp.stack([s_re.reshape(b, -1, LANES), s_im.reshape(b, -1, LANES)], axis=2)
    return st.reshape(b, 1, 4 * S5_HALF_STATE)


def _unpack_state(st):
    b = st.shape[0]
    st = st.reshape(b, -1, 2, LANES)
    return (st[:, :, 0].reshape(b, S5_GROUPS, S5_STATE), st[:, :, 1].reshape(b, S5_GROUPS, S5_STATE))


def _ab_branches(x, pos, past, wa, ws, *, tm_in, tm_s5, tq, tk):
    b, t, _ = x.shape
    x2d = x.reshape(b * t, D_MODEL)
    tabs = _rope_tables(pos, tq if past is None else None)
    lat, kr, qp, kp, vt, gm, u, gs = _ab_in(x2d, tabs, wa, seq_len=t, tm=tm_in)
    r3 = lambda a: a.reshape(b, t, a.shape[-1])
    if past is None:
        mla = _attn_prompt(r3(qp), r3(kp), vt, r3(gm), tq=tq)
        s0 = jnp.zeros((b, 1, 4 * S5_HALF_STATE), F32)
    else:
        past_lat, past_kr, past_re, past_im = past
        vt_b = jnp.transpose(vt.reshape(MLA_WIDTH, b, t), (1, 0, 2))
        mla = _attn_sample(r3(qp), past_lat, past_kr, r3(kp), vt_b, r3(gm), wa, tk=tk)
        s0 = _pack_state(past_re, past_im)
    s5, sfin = _s5(r3(u), r3(gs), s0, ws, tm=tm_s5)
    fin_re, fin_im = _unpack_state(sfin)
    return mla, s5, r3(lat), r3(kr), fin_re, fin_im


def _conv_taps(conv_w):
    by_lag = jnp.pad(conv_w[::-1], ((0, CONV_HALO - conv_w.shape[0]), (0, 0))).astype(BF16)
    return jnp.broadcast_to(by_lag[:, None, :], (CONV_HALO, CONV_PAD, conv_w.shape[1]))


def _conv(x, mla, s5m, past, wc, *, tm):
    b = x.shape[0]
    if past is None:
        past32 = jnp.zeros((b, CONV_HALO, D_MODEL), F32)
    else:
        past32 = jnp.pad(past, ((0, 0), (CONV_HALO - (CONV_WIDTH - 1), 0), (0, 0)))
    y, newc = _conv_layer(x, mla, s5m, past32, wc, tm=tm)
    return y, newc[:, CONV_HALO - (CONV_WIDTH - 1):]


def kernel(x_prompt, x_sample, cache_mla_latent, cache_mla_krope, state_s5_re, state_s5_im, state_conv, norm_ab, w_in_ab, g_q_lat, w_uq, g_kv_lat, w_uk, w_uv, g_q_nope, g_q_rope, g_k_nope, g_k_rope, s5_lam_re, s5_lam_im, s5_log_dt, s5_b_re, s5_b_im, s5_c_re, s5_c_im, s5_d, s5_w_glu, s5_b_glu, w_out_ab, norm_c, w_in_c, conv_w, conv_b, ln_g, ln_b, w_out_c):
    t_p = x_prompt.shape[1]
    t_s = x_sample.shape[1]
    past_len = cache_mla_latent.shape[2]
    pos_p = jnp.arange(t_p, dtype=jnp.int32)
    pos_s = past_len + jnp.arange(t_s, dtype=jnp.int32)

    i = 0
    wa = _prep_ab_weights(norm_ab[i], w_in_ab[i], g_q_lat[i], w_uq[i], g_kv_lat[i], w_uk[i], w_uv[i],
                          g_q_nope[i], g_q_rope[i], g_k_nope[i], g_k_rope[i], w_out_ab[i])
    ws = _prep_s5_weights(s5_lam_re[i], s5_lam_im[i], s5_log_dt[i], s5_b_re[i], s5_b_im[i], s5_c_re[i],
                          s5_c_im[i], s5_d[i], s5_w_glu[i], s5_b_glu[i], max_tile=256)
    mla_p, s5_p, lat_p, kr_p, re_p, im_p = _ab_branches(
        x_prompt, pos_p, None, wa, ws, tm_in=512, tm_s5=512, tq=512, tk=512)
    mla_s, s5_s, lat_s, kr_s, re_s, im_s = _ab_branches(
        x_sample, pos_s, (cache_mla_latent[i], cache_mla_krope[i], state_s5_re[i], state_s5_im[i]), wa, ws,
        tm_in=256, tm_s5=t_s, tq=512, tk=1024)

    row = lambda a: a.reshape(1, -1)
    cw = _conv_taps(conv_w[i])
    wc = dict(w_out_a=wa['w_out_a'], w_out_s=wa['w_out_s'],
              norm_g=row(norm_c[i]), w_in=w_in_c[i].astype(BF16), conv_w=cw, conv_b=row(conv_b[i]),
              ln_g=row(ln_g[i]), ln_b=row(ln_b[i]), w_out=w_out_c[i].astype(BF16))
    yp, conv_p = _conv(x_prompt, mla_p, s5_p, None, wc, tm=512)
    ys, conv_s = _conv(x_sample, mla_s, s5_s, state_conv[i], wc, tm=t_s)

    st = lambda a: a[None]
    return (yp, ys, st(lat_p), st(kr_p), st(re_p), st(im_p), st(conv_p),
            st(lat_s), st(kr_s), st(re_s), st(im_s), st(conv_s))
```
